```python
import math
import jax
import jax.numpy as jnp
from jax import lax
import numpy as np

D_MODEL = 1024
BATCH = 2
SEQ = 8192
DEPTH = 4
DEC_BATCH = 128
DEC_SEQ = 8
PAST_LEN = 2048
PAGE_SIZE = 128

N_MIXERS = 3
N_SSD = (DEPTH + 2) // 3
N_POOL = (DEPTH + 1) // 3
N_NSA = DEPTH // 3
EPS = 1e-6
D_FF = 4 * D_MODEL

SSD_EXPAND = 2
SSD_D_INNER = SSD_EXPAND * D_MODEL
SSD_HEAD_DIM = 64
SSD_HEADS = SSD_D_INNER // SSD_HEAD_DIM
SSD_GROUPS = 4
SSD_HG = SSD_HEADS // SSD_GROUPS
SSD_STATE = 128
SSD_CONV = 4
SSD_CHUNK = 128
SSD_CONV_DIM = SSD_D_INNER + 2 * SSD_GROUPS * SSD_STATE
SSD_IN_DIM = SSD_D_INNER + SSD_CONV_DIM + SSD_HEADS
DT_MIN = 1e-3
DT_MAX = 1e-1

POOL_WINDOWS = (2, 4, 8, 16)
POOL_GROUPS = len(POOL_WINDOWS)
POOL_GC = D_MODEL // POOL_GROUPS
POOL_BUF = max(POOL_WINDOWS) - 1

ATT_HEADS = 16
ATT_HEAD_DIM = 64
ATT_KV = 4
ATT_HG = ATT_HEADS // ATT_KV
Q_DIM = ATT_HEADS * ATT_HEAD_DIM
KV_DIM = 6 * ATT_KV * ATT_HEAD_DIM
NSA_IN_DIM = Q_DIM + KV_DIM + 3 * ATT_HEADS
CMP_BLOCK = 32
CMP_STRIDE = 16
CMP_HIDDEN = 128
SEL_BLOCK = 64
N_SEL = 8
N_LOCAL = 2
FORCE_BONUS = 1000.0
WINDOW = 512
Q_BLOCK = 128

N_BUCKETS = 32
MAX_DISTANCE = 128

kernel_name = 'hybrid_ssd_pool_nsa_decoder_step'


def rmsnorm(x, g):
    xf = x.astype(jnp.float32)
    y = xf * lax.rsqrt(jnp.mean(xf * xf, axis=-1, keepdims=True) + EPS)
    return (y * g.astype(jnp.float32)).astype(x.dtype)


def sqrelu_mlp(h, w_up, w_down):
    a = jax.nn.relu(h @ w_up)
    return (a * a) @ w_down


def rel_bucket(dist):
    n = jnp.maximum(dist, 0)
    max_exact = N_BUCKETS // 2
    nf = jnp.maximum(n, 1).astype(jnp.float32)
    large = max_exact + (jnp.log(nf / max_exact) / math.log(MAX_DISTANCE / max_exact)
                         * (N_BUCKETS - max_exact)).astype(jnp.int32)
    large = jnp.minimum(large, N_BUCKETS - 1)
    return jnp.where(n < max_exact, n, large)


def masked_softmax(s, valid):
    s = jnp.where(valid, s.astype(jnp.float32), -jnp.inf)
    m = jnp.max(s, axis=-1, keepdims=True)
    m = jnp.where(jnp.isfinite(m), m, 0.0)
    e = jnp.where(valid, jnp.exp(s - m), 0.0)
    return e / jnp.maximum(jnp.sum(e, axis=-1, keepdims=True), 1e-30)


def causal_dwconv(u, buf, w, b):
    up = jnp.concatenate([buf, u], axis=1)
    L = u.shape[1]
    out = b
    for k in range(SSD_CONV):
        out = out + up[:, k:k + L] * w[k]
    return out, up[:, -(SSD_CONV - 1):]


def ssd_chunked(x, dt, a, bm, cm, h0, chunk):
    bsz, L = x.shape[:2]
    nc = L // chunk
    r = lambda t: t.reshape((bsz, nc, chunk) + t.shape[2:])
    x, dt, bm, cm = r(x), r(dt), r(bm), r(cm)
    acum = jnp.cumsum(dt * a, axis=2)
    seg = acum[:, :, :, None] - acum[:, :, None, :]
    causal = jnp.tril(jnp.ones((chunk, chunk), bool))[:, :, None, None]
    decay = jnp.exp(jnp.where(causal, seg, -jnp.inf))
    cb = jnp.einsum('bctgn,bcsgn->bctsg', cm, bm)
    wts = cb[..., None] * decay * dt[:, :, None]
    y_intra = jnp.einsum('bctsgh,bcsghp->bctghp', wts, x)
    decay_end = jnp.exp(acum[:, :, -1:] - acum)
    s_chunk = jnp.einsum('bcsgh,bcsgn,bcsghp->bcghpn', decay_end * dt, bm, x)
    chunk_decay = jnp.exp(acum[:, :, -1])

    def step(h, inp):
        s_c, d_c = inp
        return h * d_c[..., None, None] + s_c, h

    h_last, h_in = lax.scan(step, h0.astype(jnp.float32),
                            (jnp.moveaxis(s_chunk, 1, 0).astype(jnp.float32),
                             jnp.moveaxis(chunk_decay, 1, 0)))
    h_in = jnp.moveaxis(h_in, 0, 1)
    y_inter = jnp.einsum('bctgn,bcghpn->bctghp', cm, h_in) * jnp.exp(acum)[..., None]
    y = (y_intra + y_inter).reshape(bsz, L, x.shape[3], x.shape[4], x.shape[5])
    return y, h_last


def ssd_mixer(h, conv_buf, ssm0, w_in, conv_w, conv_b, dt_bias, a_log, d_skip, norm_g, w_out):
    bsz, L, _ = h.shape
    G, HG, P, N = SSD_GROUPS, SSD_HG, SSD_HEAD_DIM, SSD_STATE
    proj = h @ w_in
    z = proj[..., :SSD_D_INNER]
    xbc = proj[..., SSD_D_INNER:SSD_D_INNER + SSD_CONV_DIM]
    dt_raw = proj[..., SSD_D_INNER + SSD_CONV_DIM:]
    xbc_c, conv_new = causal_dwconv(xbc, conv_buf.astype(xbc.dtype), conv_w, conv_b)
    xbc_c = jax.nn.silu(xbc_c)
    xh = xbc_c[..., :SSD_D_INNER].reshape(bsz, L, G, HG, P)
    bm = xbc_c[..., SSD_D_INNER:SSD_D_INNER + G * N].reshape(bsz, L, G, N)
    cm = xbc_c[..., SSD_D_INNER + G * N:].reshape(bsz, L, G, N)
    dt = jax.nn.softplus(dt_raw.astype(jnp.float32) + dt_bias.astype(jnp.float32)).reshape(bsz, L, G, HG)
    a = -jnp.exp(a_log.astype(jnp.float32)).reshape(G, HG)
    chunk = SSD_CHUNK if L % SSD_CHUNK == 0 else L
    y, h_last = ssd_chunked(xh, dt, a, bm, cm, ssm0.reshape(bsz, G, HG, P, N), chunk)
    y = y + xh * d_skip.reshape(G, HG)[..., None]
    y = (y.reshape(bsz, L, SSD_D_INNER) * jax.nn.silu(z)).reshape(bsz, L, G, SSD_D_INNER // G)
    y = rmsnorm(y, norm_g.reshape(G, SSD_D_INNER // G)).reshape(bsz, L, SSD_D_INNER)
    return y @ w_out, conv_new, h_last.reshape(bsz, SSD_HEADS, P, N)


def pool_mixer(h, buf, start, w_grp, scale):
    bsz, L, D = h.shape
    cat = jnp.concatenate([buf.astype(h.dtype), h], axis=1)
    cs = jnp.pad(jnp.cumsum(cat.astype(jnp.float32), axis=1), ((0, 0), (1, 0), (0, 0)))
    pos = start + jnp.arange(L, dtype=jnp.int32)
    parts = []
    for gi, w in enumerate(POOL_WINDOWS):
        lo, hi = gi * POOL_GC, (gi + 1) * POOL_GC
        tot = cs[:, POOL_BUF + 1:POOL_BUF + 1 + L, lo:hi] - cs[:, POOL_BUF + 1 - w:POOL_BUF + 1 - w + L, lo:hi]
        cnt = jnp.minimum(pos + 1, w).astype(jnp.float32)[None, :, None]
        parts.append(tot / cnt)
    pooled = jnp.concatenate(parts, axis=-1).astype(h.dtype)
    diff = (pooled - h).reshape(bsz, L, POOL_GROUPS, POOL_GC)
    y = jnp.einsum('blgc,gcd->blgd', diff, w_grp).reshape(bsz, L, D) * scale
    return y, cat[:, -POOL_BUF:]


def compress_blocks(kf, pe, w1, w2):
    bsz, lp = kf.shape[:2]
    sub = kf.reshape(bsz, lp // CMP_STRIDE, CMP_STRIDE, ATT_KV, ATT_HEAD_DIM)
    w1r = w1.reshape(CMP_BLOCK // CMP_STRIDE, CMP_STRIDE, ATT_HEAD_DIM, CMP_HIDDEN)
    pre = jnp.einsum('bnskd,jsde->bnjke', sub, w1r)
    hid = pre[:, :-1, 0] + pre[:, 1:, 1] + pe.reshape(-1) @ w1
    return jax.nn.silu(hid) @ w2


def nsa_mixer(h, kv_past, win_buf, start, n_keep, w_in, cmp_pe, cmp_w1, cmp_w2, w_out, rel_bias):
    bsz, L, _ = h.shape
    proj = h @ w_in
    q = proj[..., :Q_DIM].reshape(bsz, L, ATT_KV, ATT_HG, ATT_HEAD_DIM) * (ATT_HEAD_DIM ** -0.5)
    kv_new = proj[..., Q_DIM:Q_DIM + KV_DIM].reshape(bsz, L, 6, ATT_KV, ATT_HEAD_DIM)
    gates = jax.nn.sigmoid(proj[..., Q_DIM + KV_DIM:].astype(jnp.float32)).reshape(bsz, L, ATT_KV, ATT_HG, 3)
    paged_new = kv_new[:, :, :4]
    win_new = kv_new[:, :, 4:]
    full = jnp.concatenate([kv_past.astype(h.dtype), paged_new], axis=1)
    n_keys = full.shape[1]
    n_pad = -(-n_keys // SEL_BLOCK) * SEL_BLOCK
    full = jnp.pad(full, ((0, 0), (0, n_pad - n_keys), (0, 0), (0, 0), (0, 0)))
    n_blk = n_pad // SEL_BLOCK
    n_sel = min(N_SEL, n_blk)
    kc = compress_blocks(full[:, :, 0], cmp_pe[0], cmp_w1[0], cmp_w2[0])
    vc = compress_blocks(full[:, :, 1], cmp_pe[1], cmp_w1[1], cmp_w2[1])
    n_cmp = kc.shape[1]
    cmp_end = jnp.arange(n_cmp, dtype=jnp.int32) * CMP_STRIDE + (CMP_BLOCK - 1)
    to_blocks = lambda t: jnp.moveaxis(t.reshape(bsz, n_blk, SEL_BLOCK, ATT_KV, ATT_HEAD_DIM), 3, 1)
    ks_blk = to_blocks(full[:, :, 2])
    vs_blk = to_blocks(full[:, :, 3])
    win_all = jnp.concatenate([win_buf.astype(h.dtype), win_new], axis=1)
    n_buf = win_buf.shape[1]
    win_pos = start - n_buf + jnp.arange(n_buf + L, dtype=jnp.int32)
    kw_all, vw_all = win_all[:, :, 0], win_all[:, :, 1]
    head_id = jnp.arange(ATT_HEADS, dtype=jnp.int32).reshape(ATT_KV, ATT_HG)
    bias_flat = rel_bias.T.reshape(-1)
    take = jax.vmap(jax.vmap(lambda blk, ix: blk[ix]))

    def token_bias(dist):
        b = rel_bias[rel_bucket(dist)]
        return jnp.transpose(b.reshape(dist.shape + (ATT_KV, ATT_HG)), (2, 3, 0, 1))

    def attend_block(qb, gb, qpos, kw, vw, kwpos):
        nq = qb.shape[1]
        d_c = qpos[:, None] - cmp_end[None, :]
        s_c = jnp.einsum('bqghd,bngd->bghqn', qb, kc) + token_bias(d_c)
        p_c = masked_softmax(s_c, d_c >= 0)
        o_c = jnp.einsum('bghqn,bngd->bqghd', p_c.astype(vc.dtype), vc)
        imp = jnp.pad(p_c.sum(axis=2), ((0, 0), (0, 0), (0, 0), (1, 1)))
        imp = (imp[..., 1:] + imp[..., :-1]).reshape(bsz, ATT_KV, nq, n_blk, SEL_BLOCK // CMP_STRIDE).sum(-1)
        qblk = qpos // SEL_BLOCK
        jb = jnp.arange(n_blk, dtype=jnp.int32)
        lag = qblk[:, None] - jb[None, :]
        allowed = lag >= 0
        forced = (jb[None, :] == 0) | (allowed & (lag < N_LOCAL))
        score = jnp.where(allowed, imp + jnp.where(forced, FORCE_BONUS, 0.0), -1.0)
        _, idx = lax.top_k(score, n_sel)
        k_s = take(ks_blk, idx)
        v_s = take(vs_blk, idx)
        kpos = idx[..., None] * SEL_BLOCK + jnp.arange(SEL_BLOCK, dtype=jnp.int32)
        d_s = qpos[None, None, :, None, None] - kpos
        bias_s = bias_flat[head_id[None, :, :, None, None, None] * N_BUCKETS + rel_bucket(d_s)[:, :, None]]
        s_s = jnp.einsum('bqghd,bgqnkd->bghqnk', qb, k_s) + bias_s
        shp = s_s.shape[:4] + (n_sel * SEL_BLOCK,)
        ok_s = jnp.broadcast_to((d_s >= 0)[:, :, None], s_s.shape).reshape(shp)
        p_s = masked_softmax(s_s.reshape(shp), ok_s)
        o_s = jnp.einsum('bghqm,bgqmd->bqghd', p_s.astype(v_s.dtype),
                         v_s.reshape(bsz, ATT_KV, nq, n_sel * SEL_BLOCK, ATT_HEAD_DIM))
        d_w = qpos[:, None] - kwpos[None, :]
        ok_w = (d_w >= 0) & (d_w <= WINDOW) & (kwpos[None, :] >= 0)
        s_w = jnp.einsum('bqghd,bsgd->bghqs', qb, kw) + token_bias(d_w)
        p_w = masked_softmax(s_w, ok_w)
        o_w = jnp.einsum('bghqs,bsgd->bqghd', p_w.astype(vw.dtype), vw)
        return gb[..., 0:1] * o_c + gb[..., 1:2] * o_s + gb[..., 2:3] * o_w

    qpos_all = start + jnp.arange(L, dtype=jnp.int32)
    if L <= Q_BLOCK:
        o = attend_block(q, gates, qpos_all, kw_all, vw_all, win_pos)
    else:
        span = WINDOW + Q_BLOCK
        off = n_buf - WINDOW

        def one(i):
            q0 = i * Q_BLOCK
            sl = lambda t: lax.dynamic_slice_in_dim(t, q0, Q_BLOCK, axis=1)
            wl = lambda t: lax.dynamic_slice_in_dim(t, q0 + off, span, axis=1)
            return attend_block(sl(q), sl(gates), lax.dynamic_slice_in_dim(qpos_all, q0, Q_BLOCK),
                                wl(kw_all), wl(vw_all), lax.dynamic_slice_in_dim(win_pos, q0 + off, span))

        o = lax.map(one, jnp.arange(L // Q_BLOCK, dtype=jnp.int32))
        o = jnp.moveaxis(o, 0, 1).reshape(bsz, L, ATT_KV, ATT_HG, ATT_HEAD_DIM)
    y = o.reshape(bsz, L, Q_DIM).astype(h.dtype) @ w_out
    return y, paged_new, win_all[:, -n_keep:]


def setup_inputs(seed: int = 0) -> dict:
    key = jax.random.key(seed)
    keys = iter(jax.random.split(key, 40))

    def nrm(shape, scale=1.0):
        return jax.random.normal(next(keys), shape, jnp.float32) * scale

    def gain(shape):
        return 1.0 + 0.05 * jax.random.normal(next(keys), shape, jnp.float32)

    n_pages = PAST_LEN // PAGE_SIZE
    n_used = DEC_BATCH * n_pages
    n_phys = n_used + max(1, n_used // 4)
    win_rows = min(WINDOW, PAST_LEN)
    x_prompt = nrm((BATCH, SEQ, D_MODEL))
    x_sample = nrm((DEC_BATCH, DEC_SEQ, D_MODEL))
    cache_nsa_kv = nrm((N_NSA, n_phys, PAGE_SIZE, 4, ATT_KV, ATT_HEAD_DIM))
    state_nsa_win = nrm((N_NSA, DEC_BATCH, win_rows, 2, ATT_KV, ATT_HEAD_DIM))
    state_ssm = nrm((N_SSD, DEC_BATCH, SSD_HEADS, SSD_HEAD_DIM, SSD_STATE), 0.5)
    state_conv = nrm((N_SSD, DEC_BATCH, SSD_CONV - 1, SSD_CONV_DIM))
    state_pool = nrm((N_POOL, DEC_BATCH, POOL_BUF, D_MODEL))
    page_table = jax.random.permutation(next(keys), n_phys)[:n_used].reshape(DEC_BATCH, n_pages).astype(jnp.int32)
    dt = jnp.exp(jax.random.uniform(next(keys), (N_SSD, SSD_HEADS), jnp.float32,
                                    math.log(DT_MIN), math.log(DT_MAX)))
    ssd_dt_bias = dt + jnp.log(-jnp.expm1(-dt))
    ssd_a_log = jnp.log(jax.random.uniform(next(keys), (N_SSD, SSD_HEADS), jnp.float32, 1.0, 16.0))
    return {
        'x_prompt': x_prompt,
        'x_sample': x_sample,
        'cache_nsa_kv': cache_nsa_kv,
        'state_nsa_win': state_nsa_win,
        'state_ssm': state_ssm,
        'state_conv': state_conv,
        'state_pool': state_pool,
        'page_table': page_table,
        'rel_bias': nrm((N_BUCKETS, ATT_HEADS), 0.5),
        'norm_mix': gain((DEPTH, D_MODEL)),
        'norm_ffn': gain((DEPTH, D_MODEL)),
        'norm_out': gain((D_MODEL,)),
        'ffn_w_up': nrm((DEPTH, D_MODEL, D_FF), D_MODEL ** -0.5),
        'ffn_w_down': nrm((DEPTH, D_FF, D_MODEL), 0.5 * D_FF ** -0.5),
        'ssd_w_in': nrm((N_SSD, D_MODEL, SSD_IN_DIM), D_MODEL ** -0.5),
        'ssd_conv_w': nrm((N_SSD, SSD_CONV, SSD_CONV_DIM), SSD_CONV ** -0.5),
        'ssd_conv_b': nrm((N_SSD, SSD_CONV_DIM), 0.01),
        'ssd_dt_bias': ssd_dt_bias,
        'ssd_a_log': ssd_a_log,
        'ssd_d': 1.0 + 0.1 * nrm((N_SSD, SSD_HEADS)),
        'ssd_norm': gain((N_SSD, SSD_D_INNER)),
        'ssd_w_out': nrm((N_SSD, SSD_D_INNER, D_MODEL), SSD_D_INNER ** -0.5),
        'pool_w': nrm((N_POOL, POOL_GROUPS, POOL_GC, POOL_GC), POOL_GC ** -0.5),
        'pool_scale': 1.0 + 0.1 * nrm((N_POOL, D_MODEL)),
        'nsa_w_in': nrm((N_NSA, D_MODEL, NSA_IN_DIM), D_MODEL ** -0.5),
        'nsa_cmp_pe': nrm((N_NSA, 2, CMP_BLOCK, ATT_HEAD_DIM), 0.5),
        'nsa_cmp_w1': nrm((N_NSA, 2, CMP_BLOCK * ATT_HEAD_DIM, CMP_HIDDEN), (CMP_BLOCK * ATT_HEAD_DIM) ** -0.5),
        'nsa_cmp_w2': nrm((N_NSA, 2, CMP_HIDDEN, ATT_HEAD_DIM), CMP_HIDDEN ** -0.5),
        'nsa_w_out': nrm((N_NSA, Q_DIM, D_MODEL), Q_DIM ** -0.5),
    }


def reference(x_prompt, x_sample, cache_nsa_kv, state_nsa_win, state_ssm, state_conv, state_pool, page_table,
              rel_bias, norm_mix, norm_ffn, norm_out, ffn_w_up, ffn_w_down,
              ssd_w_in, ssd_conv_w, ssd_conv_b, ssd_dt_bias, ssd_a_log, ssd_d, ssd_norm, ssd_w_out,
              pool_w, pool_scale, nsa_w_in, nsa_cmp_pe, nsa_cmp_w1, nsa_cmp_w2, nsa_w_out):
    bp, lp, _ = x_prompt.shape
    bs = x_sample.shape[0]
    past_len = page_table.shape[1] * PAGE_SIZE
    xp, xs = x_prompt, x_sample
    kv_p, kv_s, win_p, win_s = [], [], [], []
    ssm_p, ssm_s, conv_p, conv_s, pool_p, pool_s = [], [], [], [], [], []
    for i in range(DEPTH):
        kind, li = i % N_MIXERS, i // N_MIXERS
        hp = rmsnorm(xp, norm_mix[i])
        hs = rmsnorm(xs, norm_mix[i])
        if kind == 0:
            w = (ssd_w_in[li], ssd_conv_w[li], ssd_conv_b[li], ssd_dt_bias[li], ssd_a_log[li],
                 ssd_d[li], ssd_norm[li], ssd_w_out[li])
            conv0 = jnp.zeros((bp, SSD_CONV - 1, SSD_CONV_DIM), hp.dtype)
            ssm0 = jnp.zeros((bp, SSD_HEADS, SSD_HEAD_DIM, SSD_STATE), jnp.float32)
            yp, c_new_p, s_new_p = ssd_mixer(hp, conv0, ssm0, *w)
            ys, c_new_s, s_new_s = ssd_mixer(hs, state_conv[li], state_ssm[li], *w)
            conv_p.append(c_new_p)
            conv_s.append(c_new_s)
            ssm_p.append(s_new_p)
            ssm_s.append(s_new_s)
        elif kind == 1:
            buf0 = jnp.zeros((bp, POOL_BUF, D_MODEL), hp.dtype)
            yp, b_new_p = pool_mixer(hp, buf0, 0, pool_w[li], pool_scale[li])
            ys, b_new_s = pool_mixer(hs, state_pool[li], past_len, pool_w[li], pool_scale[li])
            pool_p.append(b_new_p)
            pool_s.append(b_new_s)
        else:
            w = (nsa_w_in[li], nsa_cmp_pe[li], nsa_cmp_w1[li], nsa_cmp_w2[li], nsa_w_out[li], rel_bias)
            kv0 = jnp.zeros((bp, 0, 4, ATT_KV, ATT_HEAD_DIM), hp.dtype)
            win0 = jnp.zeros((bp, WINDOW, 2, ATT_KV, ATT_HEAD_DIM), hp.dtype)
            yp, kvn_p, wn_p = nsa_mixer(hp, kv0, win0, 0, min(WINDOW, lp), *w)
            past = cache_nsa_kv[li][page_table].reshape(bs, past_len, 4, ATT_KV, ATT_HEAD_DIM)
            ys, kvn_s, wn_s = nsa_mixer(hs, past, state_nsa_win[li], past_len, state_nsa_win.shape[2], *w)
            kv_p.append(kvn_p)
            kv_s.append(kvn_s)
            win_p.append(wn_p)
            win_s.append(wn_s)
        xp = xp + yp
        xs = xs + ys
        xp = xp + sqrelu_mlp(rmsnorm(xp, norm_ffn[i]), ffn_w_up[i], ffn_w_down[i])
        xs = xs + sqrelu_mlp(rmsnorm(xs, norm_ffn[i]), ffn_w_up[i], ffn_w_down[i])
    y_prompt = rmsnorm(xp, norm_out)
    y_sample = rmsnorm(xs, norm_out)
    nsa_kv_prompt = jnp.stack(kv_p)
    nsa_kv_sample = jnp.stack(kv_s)
    nsa_win_prompt = jnp.stack(win_p)
    nsa_win_sample = jnp.stack(win_s)
    ssm_prompt = jnp.stack(ssm_p)
    ssm_sample = jnp.stack(ssm_s)
    conv_prompt = jnp.stack(conv_p)
    conv_sample = jnp.stack(conv_s)
    pool_prompt = jnp.stack(pool_p)
    pool_sample = jnp.stack(pool_s)
    return (y_prompt, y_sample, nsa_kv_prompt, nsa_kv_sample, nsa_win_prompt, nsa_win_sample,
            ssm_prompt, ssm_sample, conv_prompt, conv_sample, pool_prompt, pool_sample)
```

```python
import functools
import math

import numpy as np
import jax
import jax.numpy as jnp
from jax import lax
from jax.experimental import pallas as pl
from jax.experimental.pallas import tpu as pltpu

F32 = jnp.float32
BF16 = jnp.bfloat16
HIGHEST = lax.Precision.HIGHEST
EPS = 1e-6
NEG_INF = float("-inf")

V7X_VMEM_LIMIT_BYTES = 56 * 1024 * 1024
LANES = 128
SUBLANES = 8

D_MODEL = 1024
SSD_HEAD_DIM = 64
SSD_GROUPS = 4
SSD_STATE = 128
SSD_CONV = 4
SSD_CHUNK = 128
POOL_WINDOWS = (2, 4, 8, 16)
POOL_HALO = 16
ATT_HEADS = 16
ATT_HEAD_DIM = 64
ATT_KV = 4
ATT_HG = 4
CMP_BLOCK = 32
CMP_STRIDE = 16
CMP_HIDDEN = 128
SEL_BLOCK = 64
N_SEL = 8
N_LOCAL = 2
FORCE_BONUS = 1000.0
WINDOW = 512
Q_TILE = 128
PAGE_SIZE = 128
N_BUCKETS = 32
MAX_DISTANCE = 128
CMP_FRONT = 112
CMP_BACK = 16


def _cparams(*sem):
    return pltpu.CompilerParams(dimension_semantics=sem, vmem_limit_bytes=V7X_VMEM_LIMIT_BYTES)


def _bucket_thresholds():
    d = np.arange(0, MAX_DISTANCE + 1)
    max_exact = N_BUCKETS // 2
    nf = np.maximum(d, 1).astype(np.float32)
    large = max_exact + (np.log(nf / np.float32(max_exact)) / np.float32(math.log(MAX_DISTANCE / max_exact))
                         * np.float32(N_BUCKETS - max_exact)).astype(np.int32)
    large = np.minimum(large, N_BUCKETS - 1)
    b = np.where(d < max_exact, d, large)
    return [int(np.argmax(b >= k)) for k in range(N_BUCKETS)]


BUCKET_THR = _bucket_thresholds()


def _rms(x, g):
    return x * lax.rsqrt(jnp.mean(x * x, axis=-1, keepdims=True) + EPS) * g


def _silu(x):
    return x * jax.nn.sigmoid(x)


def _softplus(x):
    return jnp.maximum(x, 0.0) + jnp.log1p(jnp.exp(-jnp.abs(x)))


def _dot(a, b):
    return jnp.dot(a, b, preferred_element_type=F32)


def _dot_nt(a, b):
    return lax.dot_general(a, b, (((1,), (1,)), ((), ())), preferred_element_type=F32)


def _dot_exact(a, b):
    return jnp.dot(a, b, precision=HIGHEST, preferred_element_type=F32)


def _norm_mm_kernel(x_ref, g_ref, w_ref, o_ref, h_scr):
    @pl.when(pl.program_id(1) == 0)
    def _():
        h_scr[...] = _rms(x_ref[...], g_ref[...]).astype(BF16)

    o_ref[...] = _dot(h_scr[...], w_ref[...])


def norm_matmul(x, g, w, tm, tn):
    T, D = x.shape
    N = w.shape[1]
    return pl.pallas_call(
        _norm_mm_kernel,
        grid=(T // tm, N // tn),
        in_specs=[pl.BlockSpec((tm, D), lambda i, j: (i, 0)),
                  pl.BlockSpec((1, D), lambda i, j: (0, 0)),
                  pl.BlockSpec((D, tn), lambda i, j: (0, j))],
        out_specs=pl.BlockSpec((tm, tn), lambda i, j: (i, j)),
        out_shape=jax.ShapeDtypeStruct((T, N), F32),
        scratch_shapes=[pltpu.VMEM((tm, D), BF16)],
        compiler_params=_cparams("parallel", "arbitrary"),
        name="norm_matmul",
    )(x, g, w)


def _mlp_kernel(x_ref, g_ref, wu_ref, wd_ref, go_ref, o_ref, h_scr, acc, *, final_norm):
    j = pl.program_id(1)

    @pl.when(j == 0)
    def _():
        h_scr[...] = _rms(x_ref[...], g_ref[...]).astype(BF16)
        acc[...] = jnp.zeros_like(acc)

    a = jnp.maximum(_dot(h_scr[...], wu_ref[...]), 0.0)
    acc[...] += _dot((a * a).astype(BF16), wd_ref[...])

    @pl.when(j == pl.num_programs(1) - 1)
    def _():
        y = x_ref[...] + acc[...]
        o_ref[...] = _rms(y, go_ref[...]) if final_norm else y


def sqrelu_mlp_residual(x, g, w_up, w_down, g_out, final_norm, tm, tf):
    T, D = x.shape
    F = w_up.shape[1]
    return pl.pallas_call(
        functools.partial(_mlp_kernel, final_norm=final_norm),
        grid=(T // tm, F // tf),
        in_specs=[pl.BlockSpec((tm, D), lambda i, j: (i, 0)),
                  pl.BlockSpec((1, D), lambda i, j: (0, 0)),
                  pl.BlockSpec((D, tf), lambda i, j: (0, j)),
                  pl.BlockSpec((tf, D), lambda i, j: (j, 0)),
                  pl.BlockSpec((1, D), lambda i, j: (0, 0))],
        out_specs=pl.BlockSpec((tm, D), lambda i, j: (i, 0)),
        out_shape=jax.ShapeDtypeStruct((T, D), F32),
        scratch_shapes=[pltpu.VMEM((tm, D), BF16), pltpu.VMEM((tm, D), F32)],
        compiler_params=_cparams("parallel", "arbitrary"),
        name="sqrelu_mlp",
    )(x, g, w_up, w_down, g_out)


def _mm_res_kernel(a_ref, w_ref, r_ref, o_ref):
    o_ref[...] = r_ref[...] + _dot(a_ref[...].astype(BF16), w_ref[...])


def matmul_residual(a, w, res, tm):
    T, K = a.shape
    D = w.shape[1]
    return pl.pallas_call(
        _mm_res_kernel,
        grid=(T // tm,),
        in_specs=[pl.BlockSpec((tm, K), lambda i: (i, 0)),
                  pl.BlockSpec((K, D), lambda i: (0, 0)),
                  pl.BlockSpec((tm, D), lambda i: (i, 0))],
        out_specs=pl.BlockSpec((tm, D), lambda i: (i, 0)),
        out_shape=jax.ShapeDtypeStruct((T, D), F32),
        compiler_params=_cparams("parallel"),
        name="matmul_residual",
    )(a, w, res)


def _pool_kernel(x_ref, halo_ref, g_ref, w_ref, sc_ref, o_ref, tail_ref, buf, *, start, tm):
    l = pl.program_id(1)

    @pl.when(l == 0)
    def _():
        buf[0:POOL_HALO] = halo_ref[0]

    x = x_ref[0]
    h = _rms(x, g_ref[...])
    buf[POOL_HALO:POOL_HALO + tm] = h
    pos = start + l * tm + lax.broadcasted_iota(jnp.int32, (tm, 1), 0)
    gc = x.shape[1] // len(POOL_WINDOWS)
    parts = []
    for gi, w in enumerate(POOL_WINDOWS):
        lo, hi = gi * gc, (gi + 1) * gc
        tot = buf[POOL_HALO:POOL_HALO + tm, lo:hi]
        for k in range(1, w):
            tot = tot + buf[POOL_HALO - k:POOL_HALO - k + tm, lo:hi]
        cnt = jnp.minimum(pos + 1, w).astype(F32)
        diff = tot / cnt - h[:, lo:hi]
        parts.append(_dot(diff.astype(BF16), w_ref[gi]))
    y = jnp.concatenate(parts, axis=1) * sc_ref[...]
    o_ref[0] = x + y
    t = buf[tm:tm + POOL_HALO]
    tail_ref[0] = t
    buf[0:POOL_HALO] = t


def pool_mixer_residual(x, halo, g, w_grp, scale, start, tm):
    B, L, D = x.shape
    return pl.pallas_call(
        functools.partial(_pool_kernel, start=start, tm=tm),
        grid=(B, L // tm),
        in_specs=[pl.BlockSpec((1, tm, D), lambda b, l: (b, l, 0)),
                  pl.BlockSpec((1, POOL_HALO, D), lambda b, l: (b, 0, 0)),
                  pl.BlockSpec((1, D), lambda b, l: (0, 0)),
                  pl.BlockSpec(w_grp.shape, lambda b, l: (0, 0, 0)),
                  pl.BlockSpec((1, D), lambda b, l: (0, 0))],
        out_specs=[pl.BlockSpec((1, tm, D), lambda b, l: (b, l, 0)),
                   pl.BlockSpec((1, POOL_HALO, D), lambda b, l: (b, 0, 0))],
        out_shape=[jax.ShapeDtypeStruct((B, L, D), F32), jax.ShapeDtypeStruct((B, POOL_HALO, D), F32)],
        scratch_shapes=[pltpu.VMEM((POOL_HALO + tm, D), F32)],
        compiler_params=_cparams("parallel", "arbitrary"),
        name="pool_mixer",
    )(x, halo, g, w_grp, scale)


def _ssd_pre_kernel(xbc_ref, dtr_ref, c0_ref, cw_ref, cb_ref, dtb_ref, xc_ref, dt_ref, tail_ref, ext, *, tm):
    @pl.when(pl.program_id(1) == 0)
    def _():
        ext[0:SUBLANES] = c0_ref[0]

    ext[SUBLANES:SUBLANES + tm] = xbc_ref[0]
    u = cb_ref[...]
    for k in range(SSD_CONV):
        off = SUBLANES - (SSD_CONV - 1) + k
        u = u + ext[off:off + tm] * cw_ref[k:k + 1]
    xc_ref[0] = _silu(u)
    t = ext[tm:tm + SUBLANES]
    tail_ref[0] = t
    ext[0:SUBLANES] = t
    dt_ref[0] = _softplus(dtr_ref[0] + dtb_ref[...])


def ssd_pre(proj, conv0, conv_w, conv_b, dt_bias, tm):
    B, L, _ = proj.shape
    C = conv_w.shape[1]
    NDT = dt_bias.shape[1]
    return pl.pallas_call(
        functools.partial(_ssd_pre_kernel, tm=tm),
        grid=(B, L // tm),
        in_specs=[pl.BlockSpec((1, tm, C), lambda b, l: (b, l, 1)),
                  pl.BlockSpec((1, tm, NDT), lambda b, l: (b, l, 4)),
                  pl.BlockSpec((1, SUBLANES, C), lambda b, l: (b, 0, 0)),
                  pl.BlockSpec((SSD_CONV, C), lambda b, l: (0, 0)),
                  pl.BlockSpec((1, C), lambda b, l: (0, 0)),
                  pl.BlockSpec((1, NDT), lambda b, l: (0, 0))],
        out_specs=[pl.BlockSpec((1, tm, C), lambda b, l: (b, l, 0)),
                   pl.BlockSpec((1, tm, NDT), lambda b, l: (b, l, 0)),
                   pl.BlockSpec((1, SUBLANES, C), lambda b, l: (b, 0, 0))],
        out_shape=[jax.ShapeDtypeStruct((B, L, C), F32), jax.ShapeDtypeStruct((B, L, NDT), F32),
                   jax.ShapeDtypeStruct((B, SUBLANES, C), F32)],
        scratch_shapes=[pltpu.VMEM((SUBLANES + tm, C), F32)],
        compiler_params=_cparams("parallel", "arbitrary"),
        name="ssd_pre",
    )(proj, proj, conv0, conv_w, conv_b, dt_bias)


def _pair_expand(v, q):
    lane = lax.broadcasted_iota(jnp.int32, (q, LANES), 1)
    tiles = []
    for p in range(4):
        lo = jnp.broadcast_to(v[:, 2 * p:2 * p + 1], (q, LANES))
        hi = jnp.broadcast_to(v[:, 2 * p + 1:2 * p + 2], (q, LANES))
        tiles.append(jnp.where(lane < SSD_HEAD_DIM, lo, hi))
    return jnp.concatenate(tiles, axis=1)


def _ssd_scan_kernel(xh_ref, b_ref, c_ref, dt_ref, alog_ref, tri_ref, sin_ref, y_ref, sout_ref, *, seg):
    Q = SSD_CHUNK
    nseg = Q // seg
    hpg = 8

    @pl.when(pl.program_id(2) == 0)
    def _():
        sout_ref[...] = sin_ref[...]

    xh = xh_ref[0]
    bg = b_ref[0].astype(BF16)
    cg = c_ref[0]
    dt = dt_ref[0]
    a = -jnp.exp(alog_ref[0])
    tri = tri_ref[...]
    acum = _dot_exact(tri, dt * a)
    acum_t = acum.T
    dt_t = dt.T
    causal = tri > 0.5
    cb = _dot_nt(cg.astype(BF16), bg)
    lane = lax.broadcasted_iota(jnp.int32, (Q, LANES), 1)
    low = lane < SSD_HEAD_DIM
    ys = []
    for p in range(hpg // 2):
        xpair = xh[:, LANES * p:LANES * (p + 1)]
        acc = None
        for which in range(2):
            h = 2 * p + which
            sg = acum[:, h:h + 1] - acum_t[h:h + 1, :]
            dec = jnp.exp(jnp.where(causal, sg, NEG_INF))
            wts = cb * dec * dt_t[h:h + 1, :]
            xm = jnp.where(low if which == 0 else jnp.logical_not(low), xpair, 0.0)
            term = _dot(wts.astype(BF16), xm.astype(BF16))
            acc = term if acc is None else acc + term
        ys.append(acc)
    y_intra = jnp.concatenate(ys, axis=1)

    if nseg == 1:
        alast = jnp.broadcast_to(acum[Q - 1:Q, :], (Q, LANES))
    else:
        r = lax.broadcasted_iota(jnp.int32, (Q, Q), 0)
        s = lax.broadcasted_iota(jnp.int32, (Q, Q), 1)
        lastsel = (s == r - (r & (seg - 1)) + (seg - 1)).astype(F32)
        alast = _dot_exact(lastsel, acum)
    wcol = jnp.exp(alast - acum) * dt
    xw_t = (xh * _pair_expand(wcol, Q)).T
    col = lax.broadcasted_iota(jnp.int32, (hpg * SSD_HEAD_DIM, Q), 1)
    y_parts = []
    for si in range(nseg):
        r0 = si * seg
        h0 = sout_ref[0, si].reshape(hpg * SSD_HEAD_DIM, SSD_STATE)
        y_parts.append(_dot_nt(cg[r0:r0 + seg], h0))
        xm = xw_t if nseg == 1 else jnp.where((col >= r0) & (col < r0 + seg), xw_t, 0.0)
        s_new = _dot(xm.astype(BF16), bg)
        for h in range(hpg):
            cd = jnp.exp(alast[r0:r0 + 1, h:h + 1])
            rows = slice(SSD_HEAD_DIM * h, SSD_HEAD_DIM * (h + 1))
            sout_ref[0, si, h] = h0[rows] * cd + s_new[rows]
    y_inter = y_parts[0] if nseg == 1 else jnp.concatenate(y_parts, axis=0)
    y_ref[0] = y_intra + y_inter * _pair_expand(jnp.exp(acum), Q)


def ssd_scan(xc, dt, a_log, tri, state0, seg):
    NB, R, _ = xc.shape
    Q = SSD_CHUNK
    nseg = Q // seg
    NC = R // Q
    G = SSD_GROUPS
    HD = 8 * SSD_HEAD_DIM
    return pl.pallas_call(
        functools.partial(_ssd_scan_kernel, seg=seg),
        grid=(NB, G, NC),
        in_specs=[pl.BlockSpec((1, Q, HD), lambda b, g, c: (b, c, g)),
                  pl.BlockSpec((1, Q, SSD_STATE), lambda b, g, c: (b, c, 16 + g)),
                  pl.BlockSpec((1, Q, SSD_STATE), lambda b, g, c: (b, c, 20 + g)),
                  pl.BlockSpec((1, Q, LANES), lambda b, g, c: (b, c, g)),
                  pl.BlockSpec((1, 1, LANES), lambda b, g, c: (g, 0, 0)),
                  pl.BlockSpec((Q, Q), lambda b, g, c: (0, 0)),
                  pl.BlockSpec((1, nseg, 8, SSD_HEAD_DIM, SSD_STATE), lambda b, g, c: (b, 0, g, 0, 0))],
        out_specs=[pl.BlockSpec((1, Q, HD), lambda b, g, c: (b, c, g)),
                   pl.BlockSpec((1, nseg, 8, SSD_HEAD_DIM, SSD_STATE), lambda b, g, c: (b, 0, g, 0, 0))],
        out_shape=[jax.ShapeDtypeStruct((NB, R, G * HD), F32), jax.ShapeDtypeStruct(state0.shape, F32)],
        compiler_params=_cparams("parallel", "parallel", "arbitrary"),
        name="ssd_scan",
    )(xc, xc, xc, dt, a_log, tri, state0)


def _ssd_post_kernel(y_ref, xh_ref, z_ref, d_ref, g_ref, w_ref, r_ref, o_ref):
    y = (y_ref[...] + xh_ref[...] * d_ref[...]) * _silu(z_ref[...])
    gw = y.shape[1] // SSD_GROUPS
    parts = []
    for gi in range(SSD_GROUPS):
        parts.append(_rms(y[:, gi * gw:(gi + 1) * gw], g_ref[:, gi * gw:(gi + 1) * gw]).astype(BF16))
    o_ref[...] = r_ref[...] + _dot(jnp.concatenate(parts, axis=1), w_ref[...])


def ssd_post(y, xc, proj, d_exp, norm_g, w_out, res, tm):
    T, DI = y.shape
    D = w_out.shape[1]
    return pl.pallas_call(
        _ssd_post_kernel,
        grid=(T // tm,),
        in_specs=[pl.BlockSpec((tm, DI), lambda i: (i, 0)),
                  pl.BlockSpec((tm, DI), lambda i: (i, 0)),
                  pl.BlockSpec((tm, DI), lambda i: (i, 0)),
                  pl.BlockSpec((1, DI), lambda i: (0, 0)),
                  pl.BlockSpec((1, DI), lambda i: (0, 0)),
                  pl.BlockSpec((DI, D), lambda i: (0, 0)),
                  pl.BlockSpec((tm, D), lambda i: (i, 0))],
        out_specs=pl.BlockSpec((tm, D), lambda i: (i, 0)),
        out_shape=jax.ShapeDtypeStruct((T, D), F32),
        compiler_params=_cparams("parallel"),
        name="ssd_post",
    )(y, xc, proj, d_exp, norm_g, w_out, res)


def _segment_tri(seg):
    r = np.arange(SSD_CHUNK)
    return jnp.asarray(((r[:, None] // seg == r[None, :] // seg) & (r[None, :] <= r[:, None])).astype(np.float32))


def ssd_layer(x, nb, conv_state, ssm_state, seg, w, tm_mm):
    T, D = x.shape
    L = T // nb
    proj = norm_matmul(x, w["norm"], w["w_in"], tm_mm, 1024)
    conv0 = jnp.pad(conv_state, ((0, 0), (SUBLANES - (SSD_CONV - 1), 0), (0, 0)))
    xc, dt, tail = ssd_pre(proj.reshape(nb, L, -1), conv0, w["conv_w"], w["conv_b"], w["dt_bias"],
                           min(L, SSD_CHUNK))
    rows = SSD_CHUNK if seg < SSD_CHUNK else L
    ngrp = T // rows
    nseg = SSD_CHUNK // seg
    st0 = ssm_state.reshape((ngrp, nseg) + ssm_state.shape[1:])
    y, st = ssd_scan(xc.reshape(ngrp, rows, -1), dt.reshape(ngrp, rows, -1), w["a_log"], _segment_tri(seg), st0, seg)
    x_new = ssd_post(y.reshape(T, -1), xc.reshape(T, -1), proj, w["d_exp"], w["norm_g"], w["w_out"], x, tm_mm)
    return x_new, tail[:, SUBLANES - (SSD_CONV - 1):], st.reshape(ssm_state.shape)


def _bias_of(d, rb_ref, h):
    val = jnp.full(d.shape, rb_ref[0, h], F32)
    for k in range(1, N_BUCKETS):
        val = jnp.where(d >= BUCKET_THR[k], rb_ref[k, h], val)
    return jnp.where(d >= 0, val - rb_ref[N_BUCKETS - 1, h], 0.0)


def _bias_tables_kernel(rb_ref, tp_ref, ts_ref, *, past_len):
    r = lax.broadcasted_iota(jnp.int32, (Q_TILE, Q_TILE), 0)
    c = lax.broadcasted_iota(jnp.int32, (Q_TILE, Q_TILE), 1)
    t = lax.broadcasted_iota(jnp.int32, (SUBLANES, 1024), 0)
    j = lax.broadcasted_iota(jnp.int32, (SUBLANES, 1024), 1)
    win0 = past_len - WINDOW
    d_s = jnp.where(j < 256, past_len + t - (CMP_STRIDE * j + CMP_BLOCK - 1),
                    jnp.where(j < 384, past_len + t - (past_len - PAGE_SIZE + (j - 256)),
                              jnp.where(j < 512, t - (j - 384), past_len + t - (win0 + (j - 512)))))

    def body(h, carry):
        tp_ref[h, 0] = _bias_of(r - c, rb_ref, h)
        tp_ref[h, 1] = _bias_of(Q_TILE + r - c, rb_ref, h)
        tp_ref[h, 2] = _bias_of(r - CMP_STRIDE * (c - CMP_FRONT) - (CMP_BLOCK - 1), rb_ref, h)
        ts_ref[h] = _bias_of(d_s, rb_ref, h)
        return carry

    lax.fori_loop(0, ATT_HEADS, body, 0)


def bias_tables(rel_bias, past_len):
    return pl.pallas_call(
        functools.partial(_bias_tables_kernel, past_len=past_len),
        in_specs=[pl.BlockSpec(memory_space=pltpu.SMEM)],
        out_specs=[pl.BlockSpec(memory_space=pltpu.VMEM), pl.BlockSpec(memory_space=pltpu.VMEM)],
        out_shape=[jax.ShapeDtypeStruct((ATT_HEADS, 3, Q_TILE, Q_TILE), F32),
                   jax.ShapeDtypeStruct((ATT_HEADS, SUBLANES, 1024), F32)],
        name="bias_tables",
    )(rel_bias)


def _cmp_pre_kernel(*refs, nsrc, nsub, n_prefetch):
    refs = refs[n_prefetch:]
    srcs = refs[:nsrc]
    w_ref, o_ref, stage = refs[nsrc], refs[nsrc + 1], refs[nsrc + 2]
    rows = nsub * CMP_STRIDE
    n = nsrc * nsub
    for f in range(2):
        xs = []
        for pair in range(2):
            l0 = f * 256 + pair * LANES
            for si, src in enumerate(srcs):
                stage[si * rows:(si + 1) * rows] = src[0, :, l0:l0 + LANES]
            cols = [stage[pl.ds(s, n, stride=CMP_STRIDE)] for s in range(CMP_STRIDE)]
            xs.append(jnp.concatenate(cols, axis=1))
        x2 = jnp.concatenate(xs, axis=0).astype(BF16)
        pre = _dot(x2, w_ref[f])
        for pair in range(2):
            o_ref[0, f, 2 * pair] = pre[pair * n:(pair + 1) * n, 0:256]
            o_ref[0, f, 2 * pair + 1] = pre[pair * n:(pair + 1) * n, 256:512]


def _cmp_fin_kernel(pre_ref, pe_ref, w1_ref, w2_ref, o_ref, scr, *, n):
    pre = pre_ref[0, 0, 0]
    scr[0:n] = pre[:, CMP_HIDDEN:2 * CMP_HIDDEN]
    scr[n:n + SUBLANES] = jnp.zeros((SUBLANES, CMP_HIDDEN), F32)
    cvec = _dot(pe_ref[0].astype(BF16), w1_ref[0])[0:1]
    hid = pre[:, 0:CMP_HIDDEN] + scr[pl.ds(1, n)] + cvec
    o_ref[0, 0, 0] = _dot(_silu(hid).astype(BF16), w2_ref[0])


def cmp_finish(pre, pe8, w1, w2d):
    B, _, KV, n, _ = pre.shape
    return pl.pallas_call(
        functools.partial(_cmp_fin_kernel, n=n),
        grid=(B, 2, KV),
        in_specs=[pl.BlockSpec((1, 1, 1, n, 256), lambda b, f, k: (b, f, k, 0, 0)),
                  pl.BlockSpec((1, SUBLANES, w1.shape[1]), lambda b, f, k: (f, 0, 0)),
                  pl.BlockSpec((1,) + w1.shape[1:], lambda b, f, k: (f, 0, 0)),
                  pl.BlockSpec((1,) + w2d.shape[1:], lambda b, f, k: (f, 0, 0))],
        out_specs=pl.BlockSpec((1, 1, 1, n, LANES), lambda b, f, k: (b, f, k, 0, 0)),
        out_shape=jax.ShapeDtypeStruct((B, 2, KV, n, LANES), F32),
        scratch_shapes=[pltpu.VMEM((n + SUBLANES, CMP_HIDDEN), F32)],
        compiler_params=_cparams("parallel", "parallel", "parallel"),
        name="cmp_finish",
    )(pre, pe8, w1, w2d)


def cmp_pre_prompt(proj, w_pair, rows):
    B, L, _ = proj.shape
    nsub = rows // CMP_STRIDE
    return pl.pallas_call(
        functools.partial(_cmp_pre_kernel, nsrc=1, nsub=nsub, n_prefetch=0),
        grid=(B, L // rows),
        in_specs=[pl.BlockSpec((1, rows, 512), lambda b, t: (b, t, 2)),
                  pl.BlockSpec(w_pair.shape, lambda b, t: (0, 0, 0))],
        out_specs=pl.BlockSpec((1, 2, ATT_KV, nsub, 256), lambda b, t: (b, 0, 0, t, 0)),
        out_shape=jax.ShapeDtypeStruct((B, 2, ATT_KV, L // CMP_STRIDE, 256), F32),
        scratch_shapes=[pltpu.VMEM((rows, LANES), F32)],
        compiler_params=_cparams("parallel", "parallel"),
        name="cmp_pre_prompt",
    )(proj, w_pair)


def cmp_pre_sample(page_table, cache, new_page, w_pair):
    B, n_pages = page_table.shape
    nsub = PAGE_SIZE // CMP_STRIDE
    nsrc = n_pages + 1

    def page_spec(p):
        return pl.BlockSpec((1, PAGE_SIZE, 512), lambda b, pt: (pt[b, p], 0, 0))

    grid_spec = pltpu.PrefetchScalarGridSpec(
        num_scalar_prefetch=1,
        grid=(B,),
        in_specs=[page_spec(p) for p in range(n_pages)]
        + [pl.BlockSpec((1, PAGE_SIZE, 512), lambda b, pt: (b, 0, 0)),
           pl.BlockSpec(w_pair.shape, lambda b, pt: (0, 0, 0))],
        out_specs=pl.BlockSpec((1, 2, ATT_KV, nsrc * nsub, 256), lambda b, pt: (b, 0, 0, 0, 0)),
        scratch_shapes=[pltpu.VMEM((nsrc * PAGE_SIZE, LANES), F32)],
    )
    return pl.pallas_call(
        functools.partial(_cmp_pre_kernel, nsrc=nsrc, nsub=nsub, n_prefetch=1),
        grid_spec=grid_spec,
        out_shape=jax.ShapeDtypeStruct((B, 2, ATT_KV, nsrc * nsub, 256), F32),
        compiler_params=_cparams("parallel"),
        name="cmp_pre_sample",
    )(page_table, *([cache] * n_pages), new_page, w_pair)


def _importance_matrix(n_rows, front):
    m = np.arange(n_rows)[:, None] - front
    j = np.arange(LANES)[None, :]
    ratio = SEL_BLOCK // CMP_STRIDE
    a = ((m >= ratio * j) & (m <= ratio * j + ratio - 1)).astype(np.float32) \
        + ((m >= ratio * j - 1) & (m <= ratio * j + ratio - 2)).astype(np.float32)
    return jnp.asarray(a)


def _softmax_rows(s):
    m = jnp.max(s, axis=-1, keepdims=True)
    m = jnp.where(m > NEG_INF, m, 0.0)
    e = jnp.exp(s - m)
    return e / jnp.maximum(jnp.sum(e, axis=-1, keepdims=True), 1e-30)


def _top_blocks(score, n_sel):
    jb = lax.broadcasted_iota(jnp.int32, score.shape, 1).astype(F32)
    sel = jnp.zeros(score.shape, F32)
    sc = score
    for _ in range(n_sel):
        mx = jnp.max(sc, axis=-1, keepdims=True)
        idx = jnp.min(jnp.where(sc == mx, jb, 1e9), axis=-1, keepdims=True)
        pick = jb == idx
        sel = jnp.where(pick, 1.0, sel)
        sc = jnp.where(pick, NEG_INF, sc)
    return sel


def _flash_step(carry, qs, kt, vt, bias, mask):
    m, l, acc = carry
    s = _dot_nt(qs, kt)
    if bias is not None:
        s = s + bias
    s = jnp.where(mask, s, NEG_INF)
    m_new = jnp.maximum(m, jnp.max(s, axis=-1, keepdims=True))
    m_safe = jnp.where(m_new > NEG_INF, m_new, 0.0)
    alpha = jnp.exp(m - m_safe)
    p = jnp.exp(s - m_safe)
    l = alpha * l + jnp.sum(p, axis=-1, keepdims=True)
    acc = alpha * acc + _dot(p.astype(BF16), vt)
    return m_new, l, acc


def _tile4(x):
    return jnp.concatenate([x, x, x, x], axis=0)


def _attn_prompt_kernel(q_ref, gate_ref, kc_ref, vc_ref, ks_ref, vs_ref, kw_ref, vw_ref, tp_ref, amat_ref, o_ref,
                        *, nsub):
    i = pl.program_id(2)
    QT = Q_TILE
    R4 = ATT_HG * QT
    lane = lax.broadcasted_iota(jnp.int32, (QT, LANES), 1)
    low = lane < ATT_HEAD_DIM
    q4 = q_ref[0]
    qrows = []
    for p in range(2):
        qp = q4[:, LANES * p:LANES * (p + 1)]
        qrows.append(jnp.where(low, qp, 0.0))
        qrows.append(jnp.where(low, 0.0, qp))
    qs = jnp.concatenate(qrows, axis=0).astype(BF16)
    r1 = lax.broadcasted_iota(jnp.int32, (QT, 1), 0)
    r4 = lax.broadcasted_iota(jnp.int32, (R4, 1), 0) & (QT - 1)
    c1 = lax.broadcasted_iota(jnp.int32, (1, LANES), 1)
    t_diag = tp_ref[:, 0].reshape(R4, QT)
    t_sub = tp_ref[:, 1].reshape(R4, QT)
    t_cmp = tp_ref[:, 2].reshape(R4, QT)

    st = pl.multiple_of(i * SUBLANES, SUBLANES)
    kc_far = kc_ref[0, 0, 0, CMP_FRONT:CMP_FRONT + nsub, :].astype(BF16)
    vc_far = vc_ref[0, 0, 0, CMP_FRONT:CMP_FRONT + nsub, :].astype(BF16)
    kc_d = kc_ref[0, 0, 0, pl.ds(st, QT), :].astype(BF16)
    vc_d = vc_ref[0, 0, 0, pl.ds(st, QT), :].astype(BF16)
    n_far = lax.broadcasted_iota(jnp.int32, (1, nsub), 1)
    s_far = jnp.where(n_far < i * SUBLANES - CMP_FRONT, _dot_nt(qs, kc_far), NEG_INF)
    d_cmp = r4 - CMP_STRIDE * (c1 - CMP_FRONT) - (CMP_BLOCK - 1)
    ok_d = (d_cmp >= 0) & (c1 >= CMP_FRONT - i * SUBLANES)
    s_d = jnp.where(ok_d, _dot_nt(qs, kc_d) + t_cmp, NEG_INF)
    p_c = _softmax_rows(jnp.concatenate([s_far, s_d], axis=1))
    o_c = _dot(p_c[:, :nsub].astype(BF16), vc_far) + _dot(p_c[:, nsub:].astype(BF16), vc_d)
    p_sum = p_c[0:QT] + p_c[QT:2 * QT] + p_c[2 * QT:3 * QT] + p_c[3 * QT:4 * QT]
    imp = _dot_exact(p_sum[:, :nsub], amat_ref[CMP_FRONT:CMP_FRONT + nsub, :]) \
        + _dot_exact(p_sum[:, nsub:], amat_ref[pl.ds(st, QT), :])

    qblk = 2 * i + (r1 >= SEL_BLOCK).astype(jnp.int32)
    lag = qblk - c1
    allowed = lag >= 0
    forced = (c1 == 0) | (allowed & (lag < N_LOCAL))
    score = jnp.where(allowed, imp + jnp.where(forced, FORCE_BONUS, 0.0), -1.0)
    sel = _top_blocks(score, N_SEL).astype(BF16)

    jb = lax.broadcasted_iota(jnp.int32, (LANES, LANES), 0)
    half = (lax.broadcasted_iota(jnp.int32, (LANES, LANES), 1) >= SEL_BLOCK).astype(jnp.int32)

    def sel_mask(t):
        e = (jb == 2 * t + half).astype(BF16)
        return _tile4(_dot(sel, e)) > 0.5

    def k_tile(ref, t):
        return ref[0, 0, 0, pl.ds(pl.multiple_of(t * QT, QT), QT), :]

    init = (jnp.full((R4, 1), NEG_INF, F32), jnp.zeros((R4, 1), F32), jnp.zeros((R4, LANES), F32))
    lower = r4 >= c1

    def far_body(t, carry):
        return _flash_step(carry, qs, k_tile(ks_ref, t), k_tile(vs_ref, t), None, sel_mask(t))

    carry = lax.fori_loop(0, jnp.maximum(i - 1, 0), far_body, init)
    tm1 = jnp.maximum(i - 1, 0)
    carry = _flash_step(carry, qs, k_tile(ks_ref, tm1), k_tile(vs_ref, tm1), t_sub, sel_mask(tm1) & (jnp.full((R4, LANES), i, jnp.int32) >= 1))
    m_s, l_s, acc_s = _flash_step(carry, qs, k_tile(ks_ref, i), k_tile(vs_ref, i), t_diag, sel_mask(i) & lower)
    o_s = acc_s / jnp.maximum(l_s, 1e-30)

    carry = init
    upper = r4 <= c1
    for back in range(WINDOW // QT, 0, -1):
        t = i - back
        tc = jnp.maximum(t, 0)
        mask = jnp.full((R4, LANES), t, jnp.int32) >= 0
        if back == WINDOW // QT:
            mask = mask & upper
        carry = _flash_step(carry, qs, k_tile(kw_ref, tc), k_tile(vw_ref, tc), t_sub if back == 1 else None, mask)
    m_w, l_w, acc_w = _flash_step(carry, qs, k_tile(kw_ref, i), k_tile(vw_ref, i), t_diag,
                                  jnp.broadcast_to(lower, (R4, LANES)))
    o_w = acc_w / jnp.maximum(l_w, 1e-30)

    g = jax.nn.sigmoid(gate_ref[0])
    outs = []
    for p in range(2):
        halves = []
        for which in range(2):
            h = 2 * p + which
            rows = slice(h * QT, (h + 1) * QT)
            halves.append(g[:, h:h + 1] * o_c[rows] + g[:, 4 + h:5 + h] * o_s[rows] + g[:, 8 + h:9 + h] * o_w[rows])
        outs.append(jnp.where(low, halves[0], halves[1]))
    o_ref[0] = jnp.concatenate(outs, axis=1)


def attn_prompt(proj, cmp_kv, kv4, tp, amat):
    B, L, _ = proj.shape
    nsub = L // CMP_STRIDE
    ncp = cmp_kv.shape[3]

    def kv_spec(f):
        return pl.BlockSpec((1, 1, 1, L, LANES), lambda b, k, i: (b, f, k, 0, 0))

    def cmp_spec(f):
        return pl.BlockSpec((1, 1, 1, ncp, LANES), lambda b, k, i: (b, f, k, 0, 0))

    return pl.pallas_call(
        functools.partial(_attn_prompt_kernel, nsub=nsub),
        grid=(B, ATT_KV, L // Q_TILE),
        in_specs=[pl.BlockSpec((1, Q_TILE, 256), lambda b, k, i: (b, i, k)),
                  pl.BlockSpec((1, Q_TILE, LANES), lambda b, k, i: (b, i, 20 + k)),
                  cmp_spec(0), cmp_spec(1), kv_spec(0), kv_spec(1), kv_spec(2), kv_spec(3),
                  pl.BlockSpec((ATT_HG, 3, Q_TILE, Q_TILE), lambda b, k, i: (k, 0, 0, 0)),
                  pl.BlockSpec(amat.shape, lambda b, k, i: (0, 0))],
        out_specs=pl.BlockSpec((1, Q_TILE, 256), lambda b, k, i: (b, i, k)),
        out_shape=jax.ShapeDtypeStruct((B, L, ATT_HEADS * ATT_HEAD_DIM), F32),
        compiler_params=_cparams("parallel", "parallel", "arbitrary"),
        name="attn_prompt",
    )(proj, proj, cmp_kv, cmp_kv, kv4, kv4, kv4, kv4, tp, amat)


def _attn_sample_kernel(*refs, n_pages, past_len):
    pages = refs[1:1 + n_pages]
    q_ref, gate_ref, new_s_ref, new_w_ref, win_ref, cmp_ref, ts_ref, amat_ref, o_ref = refs[1 + n_pages:]
    T = SUBLANES
    R = ATT_HG * T
    n_cmp_rows = cmp_ref.shape[3]
    lane_t = lax.broadcasted_iota(jnp.int32, (T, LANES), 1)
    low_t = lane_t < ATT_HEAD_DIM
    t_r = lax.broadcasted_iota(jnp.int32, (R, 1), 0) & (T - 1)
    c1 = lax.broadcasted_iota(jnp.int32, (1, LANES), 1)
    g = jax.nn.sigmoid(gate_ref[0])
    zeros_new = jnp.zeros((LANES - T, LANES), F32)
    out_tiles = []
    for kv in range(ATT_KV):
        par = kv % 2
        pl0 = LANES * (kv // 2)
        keep = (lane_t >= ATT_HEAD_DIM) if par else low_t
        qrows = []
        for hg in range(ATT_HG):
            h = kv * ATT_HG + hg
            q2 = q_ref[0, :, LANES * (h // 2):LANES * (h // 2 + 1)]
            if h % 2 != par:
                q2 = pltpu.roll(q2, ATT_HEAD_DIM, axis=1)
            qrows.append(jnp.where(keep, q2, 0.0))
        qs = jnp.concatenate(qrows, axis=0).astype(BF16)
        tab = ts_ref[kv * ATT_HG:(kv + 1) * ATT_HG].reshape(R, 1024)

        kc = jnp.concatenate([cmp_ref[0, 0, kv], jnp.zeros((256 - n_cmp_rows, LANES), F32)], axis=0)
        vc = jnp.concatenate([cmp_ref[0, 1, kv], jnp.zeros((256 - n_cmp_rows, LANES), F32)], axis=0)
        n_c = lax.broadcasted_iota(jnp.int32, (1, 256), 1)
        d_c = past_len + t_r - (CMP_STRIDE * n_c + CMP_BLOCK - 1)
        s_c = _dot_nt(qs, kc.astype(BF16)) + tab[:, 0:256]
        p_c = _softmax_rows(jnp.where(d_c >= 0, s_c, NEG_INF))
        o_c = _dot(p_c.astype(BF16), vc.astype(BF16))
        p_sum = p_c[0:T] + p_c[T:2 * T] + p_c[2 * T:3 * T] + p_c[3 * T:4 * T]
        imp = _dot_exact(p_sum, amat_ref[...])

        qblk = jnp.right_shift(past_len + lax.broadcasted_iota(jnp.int32, (T, 1), 0), SEL_BLOCK.bit_length() - 1)
        lag = qblk - c1
        allowed = lag >= 0
        forced = (c1 == 0) | (allowed & (lag < N_LOCAL))
        score = jnp.where(allowed, imp + jnp.where(forced, FORCE_BONUS, 0.0), -1.0)
        sel = _top_blocks(score, N_SEL)

        s_parts, m_parts, v_parts = [], [], []
        for p in range(n_pages):
            kp = pages[p][0, :, pl0:pl0 + LANES].astype(BF16)
            s = _dot_nt(qs, kp)
            if p == n_pages - 1:
                s = s + tab[:, 256:384]
            s_parts.append(s)
            selp = jnp.where(c1 < SEL_BLOCK, sel[:, 2 * p:2 * p + 1], sel[:, 2 * p + 1:2 * p + 2])
            m_parts.append(_tile4(selp) > 0.5)
            v_parts.append(pages[p][0, :, 256 + pl0:256 + pl0 + LANES].astype(BF16))
        k_new = jnp.concatenate([new_s_ref[0, :, pl0:pl0 + LANES], zeros_new], axis=0).astype(BF16)
        v_new = jnp.concatenate([new_s_ref[0, :, 256 + pl0:256 + pl0 + LANES], zeros_new], axis=0).astype(BF16)
        nb = 2 * n_pages
        new_ok = (c1 <= t_r) & (c1 < T)
        s_parts.append(_dot_nt(qs, k_new) + tab[:, 384:512])
        m_parts.append((_tile4(jnp.broadcast_to(sel[:, nb:nb + 1], (T, LANES))) > 0.5) & new_ok)
        v_parts.append(v_new)
        s_all = jnp.where(jnp.concatenate(m_parts, axis=1), jnp.concatenate(s_parts, axis=1), NEG_INF)
        p_s = _softmax_rows(s_all).astype(BF16)
        o_s = _dot(p_s[:, 0:LANES], v_parts[0])
        for p in range(1, n_pages + 1):
            o_s = o_s + _dot(p_s[:, LANES * p:LANES * (p + 1)], v_parts[p])

        kw = win_ref[0, :, pl0:pl0 + LANES].astype(BF16)
        vw = win_ref[0, :, 256 + pl0:256 + pl0 + LANES].astype(BF16)
        kw_new = jnp.concatenate([new_w_ref[0, :, pl0:pl0 + LANES], zeros_new], axis=0).astype(BF16)
        vw_new = jnp.concatenate([new_w_ref[0, :, 256 + pl0:256 + pl0 + LANES], zeros_new], axis=0).astype(BF16)
        c_w = lax.broadcasted_iota(jnp.int32, (1, WINDOW), 1)
        s_w = jnp.concatenate([jnp.where(c_w >= t_r, _dot_nt(qs, kw) + tab[:, 512:1024], NEG_INF),
                               jnp.where(new_ok, _dot_nt(qs, kw_new) + tab[:, 384:512], NEG_INF)], axis=1)
        p_w = _softmax_rows(s_w).astype(BF16)
        o_w = _dot(p_w[:, 0:WINDOW], vw) + _dot(p_w[:, WINDOW:], vw_new)

        gk = g[:, LANES * kv:LANES * (kv + 1)]
        for pair in range(2):
            halves = []
            for which in range(2):
                hg = 2 * pair + which
                rows = slice(hg * T, (hg + 1) * T)
                o = gk[:, hg:hg + 1] * o_c[rows] + gk[:, 4 + hg:5 + hg] * o_s[rows] + gk[:, 8 + hg:9 + hg] * o_w[rows]
                if which != par:
                    o = pltpu.roll(o, ATT_HEAD_DIM, axis=1)
                halves.append(o)
            out_tiles.append(jnp.where(low_t, halves[0], halves[1]))
    o_ref[0] = jnp.concatenate(out_tiles, axis=1)


def attn_sample(page_table, cache, proj, win, cmp_kv, ts, amat, past_len):
    B, n_pages = page_table.shape
    T = proj.shape[1]

    def page_spec(p):
        return pl.BlockSpec((1, PAGE_SIZE, 512), lambda b, pt: (pt[b, p], 0, 1))

    grid_spec = pltpu.PrefetchScalarGridSpec(
        num_scalar_prefetch=1,
        grid=(B,),
        in_specs=[page_spec(p) for p in range(n_pages)]
        + [pl.BlockSpec((1, T, 1024), lambda b, pt: (b, 0, 0)),
           pl.BlockSpec((1, T, 512), lambda b, pt: (b, 0, 5)),
           pl.BlockSpec((1, T, 512), lambda b, pt: (b, 0, 3)),
           pl.BlockSpec((1, T, 512), lambda b, pt: (b, 0, 4)),
           pl.BlockSpec((1,) + win.shape[1:], lambda b, pt: (b, 0, 0)),
           pl.BlockSpec((1,) + cmp_kv.shape[1:], lambda b, pt: (b, 0, 0, 0, 0)),
           pl.BlockSpec(ts.shape, lambda b, pt: (0, 0, 0)),
           pl.BlockSpec(amat.shape, lambda b, pt: (0, 0))],
        out_specs=pl.BlockSpec((1, T, 1024), lambda b, pt: (b, 0, 0)),
    )
    return pl.pallas_call(
        functools.partial(_attn_sample_kernel, n_pages=n_pages, past_len=past_len),
        grid_spec=grid_spec,
        out_shape=jax.ShapeDtypeStruct((B, T, ATT_HEADS * ATT_HEAD_DIM), F32),
        compiler_params=_cparams("parallel"),
        name="attn_sample",
    )(page_table, *([cache] * n_pages), proj, proj, proj, proj, win, cmp_kv, ts, amat)


def _ssd_weights(i, li, norm_mix, ssd_w_in, ssd_conv_w, ssd_conv_b, ssd_dt_bias, ssd_a_log, ssd_d, ssd_norm, ssd_w_out):
    d_inner = ssd_w_out.shape[1]
    conv_dim = ssd_conv_w.shape[2]
    heads = ssd_dt_bias.shape[1]
    hpg = heads // SSD_GROUPS
    w = ssd_w_in[li]

    def per_group(v):
        v = v.reshape(v.shape[:-1] + (SSD_GROUPS, hpg))
        return jnp.pad(v, [(0, 0)] * (v.ndim - 1) + [(0, LANES - hpg)]).reshape(v.shape[:-2] + (SSD_GROUPS * LANES,))

    w_dt = per_group(w[:, d_inner + conv_dim:])
    w_in = jnp.concatenate([w[:, :d_inner], w_dt, jnp.zeros_like(w_dt), w[:, d_inner:d_inner + conv_dim]], axis=1)
    return dict(
        norm=norm_mix[i][None],
        w_in=w_in.astype(BF16),
        conv_w=ssd_conv_w[li], conv_b=ssd_conv_b[li][None],
        dt_bias=per_group(ssd_dt_bias[li])[None],
        a_log=per_group(ssd_a_log[li]).reshape(SSD_GROUPS, 1, LANES),
        d_exp=jnp.repeat(ssd_d[li], SSD_HEAD_DIM)[None],
        norm_g=ssd_norm[li][None],
        w_out=ssd_w_out[li].astype(BF16),
    )


def _nsa_in_weight(w):
    q_dim = ATT_HEADS * ATT_HEAD_DIM
    kv_dim = 6 * ATT_KV * ATT_HEAD_DIM
    idx = np.zeros((ATT_KV, LANES), np.int32)
    ok = np.zeros((ATT_KV, LANES), bool)
    for kv in range(ATT_KV):
        for br in range(3):
            for hg in range(ATT_HG):
                idx[kv, br * ATT_HG + hg] = q_dim + kv_dim + (kv * ATT_HG + hg) * 3 + br
                ok[kv, br * ATT_HG + hg] = True
    w_g = jnp.where(jnp.asarray(ok.reshape(-1))[None, :], w[:, idx.reshape(-1)], 0.0)
    return jnp.concatenate([w[:, :q_dim] * (ATT_HEAD_DIM ** -0.5), w[:, q_dim:q_dim + kv_dim], w_g], axis=1).astype(BF16)


def _cmp_pair_weight(w1):
    w1r = w1.reshape(2, 2, CMP_STRIDE, ATT_HEAD_DIM, CMP_HIDDEN)
    eye = jnp.eye(2, dtype=w1.dtype)
    wp = jnp.einsum("fjsde,wv->fswdvje", w1r, eye)
    return wp.reshape(2, CMP_STRIDE * 2 * ATT_HEAD_DIM, 2 * 2 * CMP_HIDDEN).astype(BF16)


def kernel(x_prompt, x_sample, cache_nsa_kv, state_nsa_win, state_ssm, state_conv, state_pool, page_table, rel_bias,
           norm_mix, norm_ffn, norm_out, ffn_w_up, ffn_w_down, ssd_w_in, ssd_conv_w, ssd_conv_b, ssd_dt_bias,
           ssd_a_log, ssd_d, ssd_norm, ssd_w_out, pool_w, pool_scale, nsa_w_in, nsa_cmp_pe, nsa_cmp_w1, nsa_cmp_w2,
           nsa_w_out):
    bp, lp, d_model = x_prompt.shape
    bs, ls, _ = x_sample.shape
    depth = norm_mix.shape[0]
    n_pages = page_table.shape[1]
    past_len = n_pages * PAGE_SIZE
    assert ls == SUBLANES and lp % (16 * Q_TILE) == 0 and past_len >= WINDOW and state_nsa_win.shape[2] == WINDOW
    xp = x_prompt.reshape(bp * lp, d_model)
    xs = x_sample.reshape(bs * ls, d_model)
    tm_p, tm_s = 512, 512
    outs = {k: [] for k in ("kv_p", "kv_s", "win_p", "win_s", "ssm_p", "ssm_s", "conv_p", "conv_s", "pool_p", "pool_s")}
    for i in range(depth):
        kind, li = i % 3, i // 3
        if kind == 0:
            w = _ssd_weights(i, li, norm_mix, ssd_w_in, ssd_conv_w, ssd_conv_b, ssd_dt_bias, ssd_a_log, ssd_d,
                             ssd_norm, ssd_w_out)
            conv_dim = ssd_conv_w.shape[2]
            xp, c_p, s_p = ssd_layer(xp, bp, jnp.zeros((bp, SSD_CONV - 1, conv_dim), F32),
                                     jnp.zeros((bp,) + state_ssm.shape[2:], F32), SSD_CHUNK, w, tm_p)
            xs, c_s, s_s = ssd_layer(xs, bs, state_conv[li], state_ssm[li], ls, w, tm_s)
            outs["conv_p"].append(c_p)
            outs["conv_s"].append(c_s)
            outs["ssm_p"].append(s_p)
            outs["ssm_s"].append(s_s)
        elif kind == 1:
            g = norm_mix[i][None]
            pw = pool_w[li].astype(BF16)
            sc = pool_scale[li][None]
            xp3, tail_p = pool_mixer_residual(xp.reshape(bp, lp, d_model), jnp.zeros((bp, POOL_HALO, d_model), F32),
                                              g, pw, sc, 0, 512)
            halo_s = jnp.pad(state_pool[li], ((0, 0), (1, 0), (0, 0)))
            xs3, tail_s = pool_mixer_residual(xs.reshape(bs, ls, d_model), halo_s, g, pw, sc, past_len, ls)
            xp, xs = xp3.reshape(bp * lp, d_model), xs3.reshape(bs * ls, d_model)
            outs["pool_p"].append(tail_p[:, 1:])
            outs["pool_s"].append(tail_s[:, 1:])
        else:
            g = norm_mix[i][None]
            w_in = _nsa_in_weight(nsa_w_in[li])
            w_pair = _cmp_pair_weight(nsa_cmp_w1[li])
            pe8 = jnp.broadcast_to(nsa_cmp_pe[li].reshape(2, 1, -1), (2, SUBLANES, CMP_BLOCK * ATT_HEAD_DIM))
            w1 = nsa_cmp_w1[li].astype(BF16)
            w2d = jnp.concatenate([nsa_cmp_w2[li], nsa_cmp_w2[li]], axis=-1).astype(BF16)
            tp, ts = bias_tables(rel_bias, past_len)
            w_out = nsa_w_out[li].astype(BF16)
            kvw = ATT_KV * ATT_HEAD_DIM
            proj_p = norm_matmul(xp, g, w_in, tm_p, 1024).reshape(bp, lp, -1)
            pre_p = cmp_pre_prompt(proj_p, w_pair, 2048)
            cmp_p = jnp.pad(cmp_finish(pre_p, pe8, w1, w2d), ((0, 0), (0, 0), (0, 0), (CMP_FRONT, CMP_BACK), (0, 0)))
            kv6 = proj_p[:, :, 1024:2560].reshape(bp, lp, 6, ATT_KV, ATT_HEAD_DIM)
            kv4 = jnp.transpose(kv6[:, :, 2:6], (0, 2, 3, 1, 4)).astype(BF16)
            kv4 = jnp.concatenate([kv4, kv4], axis=-1)
            amat_p = _importance_matrix(lp // CMP_STRIDE + CMP_FRONT + CMP_BACK, CMP_FRONT)
            o_p = attn_prompt(proj_p, cmp_p, kv4, tp, amat_p)
            xp = matmul_residual(o_p.reshape(bp * lp, -1), w_out, xp, tm_p)
            outs["kv_p"].append(kv6[:, :, 0:4])
            outs["win_p"].append(kv6[:, lp - WINDOW:, 4:6])
            proj_s = norm_matmul(xs, g, w_in, tm_s, 1024).reshape(bs, ls, -1)
            cache = cache_nsa_kv[li].reshape(cache_nsa_kv.shape[1], PAGE_SIZE, 4 * kvw)
            new_page = jnp.pad(proj_s[:, :, 1024:1024 + 2 * kvw], ((0, 0), (0, PAGE_SIZE - ls), (0, 0)))
            pre_s = cmp_pre_sample(page_table, cache, new_page, w_pair)
            cmp_s = cmp_finish(pre_s, pe8, w1, w2d)
            win = state_nsa_win[li].reshape(bs, WINDOW, 2 * kvw)
            amat_s = _importance_matrix(256, 0)
            o_s = attn_sample(page_table, cache, proj_s, win, cmp_s, ts, amat_s, past_len)
            xs = matmul_residual(o_s.reshape(bs * ls, -1), w_out, xs, tm_s)
            kv6s = proj_s[:, :, 1024:2560].reshape(bs, ls, 6, ATT_KV, ATT_HEAD_DIM)
            outs["kv_s"].append(kv6s[:, :, 0:4])
            outs["win_s"].append(jnp.concatenate([state_nsa_win[li][:, ls:], kv6s[:, :, 4:6]], axis=1))
        last = i == depth - 1
        xp = sqrelu_mlp_residual(xp, norm_ffn[i][None], ffn_w_up[i].astype(BF16), ffn_w_down[i].astype(BF16),
                                 norm_out[None], last, tm_p, 512)
        xs = sqrelu_mlp_residual(xs, norm_ffn[i][None], ffn_w_up[i].astype(BF16), ffn_w_down[i].astype(BF16),
                                 norm_out[None], last, tm_s, 512)
    st = lambda k: jnp.stack(outs[k])
    return (xp.reshape(bp, lp, d_model), xs.reshape(bs, ls, d_model), st("kv_p"), st("kv_s"), st("win_p"),
            st("win_s"), st("ssm_p"), st("ssm_s"), st("conv_p"), st("conv_s"), st("pool_p"), st("pool_s"))
```

```python
import functools
import math

import numpy as np
import jax
import jax.numpy as jnp
from jax import lax
from jax.experimental import pallas as pl
from jax.experimental.pallas import tpu as pltpu

F32 = jnp.float32
BF16 = jnp.bfloat16
HIGHEST = lax.Precision.HIGHEST
EPS = 1e-6
NEG_INF = float("-inf")

V7X_VMEM_LIMIT_BYTES = 56 * 1024 * 1024
LANES = 128
SUBLANES = 8

D_MODEL = 1024
SSD_HEAD_DIM = 64
SSD_GROUPS = 4
SSD_STATE = 128
SSD_CONV = 4
SSD_CHUNK = 128
POOL_WINDOWS = (2, 4, 8, 16)
POOL_HALO = 16
ATT_HEADS = 16
ATT_HEAD_DIM = 64
ATT_KV = 4
ATT_HG = 4
CMP_BLOCK = 32
CMP_STRIDE = 16
CMP_HIDDEN = 128
SEL_BLOCK = 64
N_SEL = 8
N_LOCAL = 2
FORCE_BONUS = 1000.0
WINDOW = 512
Q_TILE = 128
PAGE_SIZE = 128
N_BUCKETS = 32
MAX_DISTANCE = 128
CMP_FRONT = 112
CMP_BACK = 16
FAR_KEYS = 1024


def _cparams(*sem):
    return pltpu.CompilerParams(dimension_semantics=sem, vmem_limit_bytes=V7X_VMEM_LIMIT_BYTES)


def _bucket_thresholds():
    d = np.arange(0, MAX_DISTANCE + 1)
    max_exact = N_BUCKETS // 2
    nf = np.maximum(d, 1).astype(np.float32)
    large = max_exact + (np.log(nf / np.float32(max_exact)) / np.float32(math.log(MAX_DISTANCE / max_exact))
                         * np.float32(N_BUCKETS - max_exact)).astype(np.int32)
    large = np.minimum(large, N_BUCKETS - 1)
    b = np.where(d < max_exact, d, large)
    return [int(np.argmax(b >= k)) for k in range(N_BUCKETS)]


BUCKET_THR = _bucket_thresholds()


def _rms(x, g):
    return x * lax.rsqrt(jnp.mean(x * x, axis=-1, keepdims=True) + EPS) * g


def _silu(x):
    return x * jax.nn.sigmoid(x)


def _softplus(x):
    return jnp.maximum(x, 0.0) + jnp.log1p(jnp.exp(-jnp.abs(x)))


def _dot(a, b):
    return jnp.dot(a, b, preferred_element_type=F32)


def _dot_nt(a, b):
    return lax.dot_general(a, b, (((1,), (1,)), ((), ())), preferred_element_type=F32)


def _dot_exact(a, b):
    return jnp.dot(a, b, precision=HIGHEST, preferred_element_type=F32)


def _norm_mm_kernel(x_ref, g_ref, w_ref, o_ref, h_scr):
    @pl.when(pl.program_id(1) == 0)
    def _():
        h_scr[...] = _rms(x_ref[...], g_ref[...]).astype(BF16)

    o_ref[...] = _dot(h_scr[...], w_ref[...])


def norm_matmul(x, g, w, tm, tn):
    T, D = x.shape
    N = w.shape[1]
    return pl.pallas_call(
        _norm_mm_kernel,
        grid=(T // tm, N // tn),
        in_specs=[pl.BlockSpec((tm, D), lambda i, j: (i, 0)),
                  pl.BlockSpec((1, D), lambda i, j: (0, 0)),
                  pl.BlockSpec((D, tn), lambda i, j: (0, j))],
        out_specs=pl.BlockSpec((tm, tn), lambda i, j: (i, j)),
        out_shape=jax.ShapeDtypeStruct((T, N), F32),
        scratch_shapes=[pltpu.VMEM((tm, D), BF16)],
        compiler_params=_cparams("parallel", "arbitrary"),
        name="norm_matmul",
    )(x, g, w)


def _mlp_kernel(x_ref, g_ref, wu_ref, wd_ref, go_ref, o_ref, h_scr, acc, *, final_norm):
    j = pl.program_id(1)

    @pl.when(j == 0)
    def _():
        h_scr[...] = _rms(x_ref[...], g_ref[...]).astype(BF16)
        acc[...] = jnp.zeros_like(acc)

    a = jnp.maximum(_dot(h_scr[...], wu_ref[...]), 0.0)
    acc[...] += _dot((a * a).astype(BF16), wd_ref[...])

    @pl.when(j == pl.num_programs(1) - 1)
    def _():
        y = x_ref[...] + acc[...]
        o_ref[...] = _rms(y, go_ref[...]) if final_norm else y


def sqrelu_mlp_residual(x, g, w_up, w_down, g_out, final_norm, tm, tf):
    T, D = x.shape
    F = w_up.shape[1]
    return pl.pallas_call(
        functools.partial(_mlp_kernel, final_norm=final_norm),
        grid=(T // tm, F // tf),
        in_specs=[pl.BlockSpec((tm, D), lambda i, j: (i, 0)),
                  pl.BlockSpec((1, D), lambda i, j: (0, 0)),
                  pl.BlockSpec((D, tf), lambda i, j: (0, j)),
                  pl.BlockSpec((tf, D), lambda i, j: (j, 0)),
                  pl.BlockSpec((1, D), lambda i, j: (0, 0))],
        out_specs=pl.BlockSpec((tm, D), lambda i, j: (i, 0)),
        out_shape=jax.ShapeDtypeStruct((T, D), F32),
        scratch_shapes=[pltpu.VMEM((tm, D), BF16), pltpu.VMEM((tm, D), F32)],
        compiler_params=_cparams("parallel", "arbitrary"),
        name="sqrelu_mlp",
    )(x, g, w_up, w_down, g_out)


def _mm_res_kernel(a_ref, w_ref, r_ref, o_ref):
    o_ref[...] = r_ref[...] + _dot(a_ref[...].astype(BF16), w_ref[...])


def matmul_residual(a, w, res, tm):
    T, K = a.shape
    D = w.shape[1]
    return pl.pallas_call(
        _mm_res_kernel,
        grid=(T // tm,),
        in_specs=[pl.BlockSpec((tm, K), lambda i: (i, 0)),
                  pl.BlockSpec((K, D), lambda i: (0, 0)),
                  pl.BlockSpec((tm, D), lambda i: (i, 0))],
        out_specs=pl.BlockSpec((tm, D), lambda i: (i, 0)),
        out_shape=jax.ShapeDtypeStruct((T, D), F32),
        compiler_params=_cparams("parallel"),
        name="matmul_residual",
    )(a, w, res)


def _pool_kernel(x_ref, halo_ref, g_ref, w_ref, sc_ref, o_ref, tail_ref, buf, *, start, tm):
    l = pl.program_id(1)

    @pl.when(l == 0)
    def _():
        buf[0:POOL_HALO] = halo_ref[0]

    x = x_ref[0]
    h = _rms(x, g_ref[...])
    buf[POOL_HALO:POOL_HALO + tm] = h
    pos = start + l * tm + lax.broadcasted_iota(jnp.int32, (tm, 1), 0)
    gc = x.shape[1] // len(POOL_WINDOWS)
    parts = []
    for gi, w in enumerate(POOL_WINDOWS):
        lo, hi = gi * gc, (gi + 1) * gc
        tot = buf[POOL_HALO:POOL_HALO + tm, lo:hi]
        for k in range(1, w):
            tot = tot + buf[POOL_HALO - k:POOL_HALO - k + tm, lo:hi]
        cnt = jnp.minimum(pos + 1, w).astype(F32)
        diff = tot / cnt - h[:, lo:hi]
        parts.append(_dot(diff.astype(BF16), w_ref[gi]))
    y = jnp.concatenate(parts, axis=1) * sc_ref[...]
    o_ref[0] = x + y
    t = buf[tm:tm + POOL_HALO]
    tail_ref[0] = t
    buf[0:POOL_HALO] = t


def pool_mixer_residual(x, halo, g, w_grp, scale, start, tm):
    B, L, D = x.shape
    return pl.pallas_call(
        functools.partial(_pool_kernel, start=start, tm=tm),
        grid=(B, L // tm),
        in_specs=[pl.BlockSpec((1, tm, D), lambda b, l: (b, l, 0)),
                  pl.BlockSpec((1, POOL_HALO, D), lambda b, l: (b, 0, 0)),
                  pl.BlockSpec((1, D), lambda b, l: (0, 0)),
                  pl.BlockSpec(w_grp.shape, lambda b, l: (0, 0, 0)),
                  pl.BlockSpec((1, D), lambda b, l: (0, 0))],
        out_specs=[pl.BlockSpec((1, tm, D), lambda b, l: (b, l, 0)),
                   pl.BlockSpec((1, POOL_HALO, D), lambda b, l: (b, 0, 0))],
        out_shape=[jax.ShapeDtypeStruct((B, L, D), F32), jax.ShapeDtypeStruct((B, POOL_HALO, D), F32)],
        scratch_shapes=[pltpu.VMEM((POOL_HALO + tm, D), F32)],
        compiler_params=_cparams("parallel", "arbitrary"),
        name="pool_mixer",
    )(x, halo, g, w_grp, scale)


def _ssd_pre_kernel(xbc_ref, dtr_ref, c0_ref, cw_ref, cb_ref, dtb_ref, xc_ref, dt_ref, tail_ref, ext, *, tm):
    @pl.when(pl.program_id(1) == 0)
    def _():
        ext[0:SUBLANES] = c0_ref[0]

    ext[SUBLANES:SUBLANES + tm] = xbc_ref[0]
    u = cb_ref[...]
    for k in range(SSD_CONV):
        off = SUBLANES - (SSD_CONV - 1) + k
        u = u + ext[off:off + tm] * cw_ref[k:k + 1]
    xc_ref[0] = _silu(u)
    t = ext[tm:tm + SUBLANES]
    tail_ref[0] = t
    ext[0:SUBLANES] = t
    dt_ref[0] = _softplus(dtr_ref[0] + dtb_ref[...])


def ssd_pre(proj, conv0, conv_w, conv_b, dt_bias, tm):
    B, L, _ = proj.shape
    C = conv_w.shape[1]
    NDT = dt_bias.shape[1]
    return pl.pallas_call(
        functools.partial(_ssd_pre_kernel, tm=tm),
        grid=(B, L // tm),
        in_specs=[pl.BlockSpec((1, tm, C), lambda b, l: (b, l, 1)),
                  pl.BlockSpec((1, tm, NDT), lambda b, l: (b, l, 4)),
                  pl.BlockSpec((1, SUBLANES, C), lambda b, l: (b, 0, 0)),
                  pl.BlockSpec((SSD_CONV, C), lambda b, l: (0, 0)),
                  pl.BlockSpec((1, C), lambda b, l: (0, 0)),
                  pl.BlockSpec((1, NDT), lambda b, l: (0, 0))],
        out_specs=[pl.BlockSpec((1, tm, C), lambda b, l: (b, l, 0)),
                   pl.BlockSpec((1, tm, NDT), lambda b, l: (b, l, 0)),
                   pl.BlockSpec((1, SUBLANES, C), lambda b, l: (b, 0, 0))],
        out_shape=[jax.ShapeDtypeStruct((B, L, C), F32), jax.ShapeDtypeStruct((B, L, NDT), F32),
                   jax.ShapeDtypeStruct((B, SUBLANES, C), F32)],
        scratch_shapes=[pltpu.VMEM((SUBLANES + tm, C), F32)],
        compiler_params=_cparams("parallel", "arbitrary"),
        name="ssd_pre",
    )(proj, proj, conv0, conv_w, conv_b, dt_bias)


def _pair_expand(v, q):
    lane = lax.broadcasted_iota(jnp.int32, (q, LANES), 1)
    tiles = []
    for p in range(4):
        lo = jnp.broadcast_to(v[:, 2 * p:2 * p + 1], (q, LANES))
        hi = jnp.broadcast_to(v[:, 2 * p + 1:2 * p + 2], (q, LANES))
        tiles.append(jnp.where(lane < SSD_HEAD_DIM, lo, hi))
    return jnp.concatenate(tiles, axis=1)


def _ssd_scan_kernel(xh_ref, b_ref, c_ref, dt_ref, alog_ref, tri_ref, sin_ref, y_ref, sout_ref, *, seg):
    Q = SSD_CHUNK
    nseg = Q // seg
    hpg = 8

    @pl.when(pl.program_id(2) == 0)
    def _():
        sout_ref[...] = sin_ref[...]

    xh = xh_ref[0]
    bg = b_ref[0].astype(BF16)
    cg = c_ref[0]
    dt = dt_ref[0]
    a = -jnp.exp(alog_ref[0])
    tri = tri_ref[...]
    acum = _dot_exact(tri, dt * a)
    acum_t = acum.T
    dt_t = dt.T
    causal = tri > 0.5
    cb = _dot_nt(cg.astype(BF16), bg)
    lane = lax.broadcasted_iota(jnp.int32, (Q, LANES), 1)
    low = lane < SSD_HEAD_DIM
    ys = []
    for p in range(hpg // 2):
        xpair = xh[:, LANES * p:LANES * (p + 1)]
        acc = None
        for which in range(2):
            h = 2 * p + which
            sg = acum[:, h:h + 1] - acum_t[h:h + 1, :]
            dec = jnp.exp(jnp.where(causal, sg, NEG_INF))
            wts = cb * dec * dt_t[h:h + 1, :]
            xm = jnp.where(low if which == 0 else jnp.logical_not(low), xpair, 0.0)
            term = _dot(wts.astype(BF16), xm.astype(BF16))
            acc = term if acc is None else acc + term
        ys.append(acc)
    y_intra = jnp.concatenate(ys, axis=1)

    if nseg == 1:
        alast = jnp.broadcast_to(acum[Q - 1:Q, :], (Q, LANES))
    else:
        r = lax.broadcasted_iota(jnp.int32, (Q, Q), 0)
        s = lax.broadcasted_iota(jnp.int32, (Q, Q), 1)
        lastsel = (s == r - (r & (seg - 1)) + (seg - 1)).astype(F32)
        alast = _dot_exact(lastsel, acum)
    wcol = jnp.exp(alast - acum) * dt
    xw_t = (xh * _pair_expand(wcol, Q)).T
    col = lax.broadcasted_iota(jnp.int32, (hpg * SSD_HEAD_DIM, Q), 1)
    y_parts = []
    for si in range(nseg):
        r0 = si * seg
        h0 = sout_ref[0, si].reshape(hpg * SSD_HEAD_DIM, SSD_STATE)
        y_parts.append(_dot_nt(cg[r0:r0 + seg], h0))
        xm = xw_t if nseg == 1 else jnp.where((col >= r0) & (col < r0 + seg), xw_t, 0.0)
        s_new = _dot(xm.astype(BF16), bg)
        for h in range(hpg):
            cd = jnp.exp(alast[r0:r0 + 1, h:h + 1])
            rows = slice(SSD_HEAD_DIM * h, SSD_HEAD_DIM * (h + 1))
            sout_ref[0, si, h] = h0[rows] * cd + s_new[rows]
    y_inter = y_parts[0] if nseg == 1 else jnp.concatenate(y_parts, axis=0)
    y_ref[0] = y_intra + y_inter * _pair_expand(jnp.exp(acum), Q)


def ssd_scan(xc, dt, a_log, tri, state0, seg):
    NB, R, _ = xc.shape
    Q = SSD_CHUNK
    nseg = Q // seg
    NC = R // Q
    G = SSD_GROUPS
    HD = 8 * SSD_HEAD_DIM
    return pl.pallas_call(
        functools.partial(_ssd_scan_kernel, seg=seg),
        grid=(NB, G, NC),
        in_specs=[pl.BlockSpec((1, Q, HD), lambda b, g, c: (b, c, g)),
                  pl.BlockSpec((1, Q, SSD_STATE), lambda b, g, c: (b, c, 16 + g)),
                  pl.BlockSpec((1, Q, SSD_STATE), lambda b, g, c: (b, c, 20 + g)),
                  pl.BlockSpec((1, Q, LANES), lambda b, g, c: (b, c, g)),
                  pl.BlockSpec((1, 1, LANES), lambda b, g, c: (g, 0, 0)),
                  pl.BlockSpec((Q, Q), lambda b, g, c: (0, 0)),
                  pl.BlockSpec((1, nseg, 8, SSD_HEAD_DIM, SSD_STATE), lambda b, g, c: (b, 0, g, 0, 0))],
        out_specs=[pl.BlockSpec((1, Q, HD), lambda b, g, c: (b, c, g)),
                   pl.BlockSpec((1, nseg, 8, SSD_HEAD_DIM, SSD_STATE), lambda b, g, c: (b, 0, g, 0, 0))],
        out_shape=[jax.ShapeDtypeStruct((NB, R, G * HD), F32), jax.ShapeDtypeStruct(state0.shape, F32)],
        compiler_params=_cparams("parallel", "parallel", "arbitrary"),
        name="ssd_scan",
    )(xc, xc, xc, dt, a_log, tri, state0)


def _ssd_post_kernel(y_ref, xh_ref, z_ref, d_ref, g_ref, w_ref, r_ref, o_ref):
    y = (y_ref[...] + xh_ref[...] * d_ref[...]) * _silu(z_ref[...])
    gw = y.shape[1] // SSD_GROUPS
    parts = []
    for gi in range(SSD_GROUPS):
        parts.append(_rms(y[:, gi * gw:(gi + 1) * gw], g_ref[:, gi * gw:(gi + 1) * gw]).astype(BF16))
    o_ref[...] = r_ref[...] + _dot(jnp.concatenate(parts, axis=1), w_ref[...])


def ssd_post(y, xc, proj, d_exp, norm_g, w_out, res, tm):
    T, DI = y.shape
    D = w_out.shape[1]
    return pl.pallas_call(
        _ssd_post_kernel,
        grid=(T // tm,),
        in_specs=[pl.BlockSpec((tm, DI), lambda i: (i, 0)),
                  pl.BlockSpec((tm, DI), lambda i: (i, 0)),
                  pl.BlockSpec((tm, DI), lambda i: (i, 0)),
                  pl.BlockSpec((1, DI), lambda i: (0, 0)),
                  pl.BlockSpec((1, DI), lambda i: (0, 0)),
                  pl.BlockSpec((DI, D), lambda i: (0, 0)),
                  pl.BlockSpec((tm, D), lambda i: (i, 0))],
        out_specs=pl.BlockSpec((tm, D), lambda i: (i, 0)),
        out_shape=jax.ShapeDtypeStruct((T, D), F32),
        compiler_params=_cparams("parallel"),
        name="ssd_post",
    )(y, xc, proj, d_exp, norm_g, w_out, res)


def _segment_tri(seg):
    r = np.arange(SSD_CHUNK)
    return jnp.asarray(((r[:, None] // seg == r[None, :] // seg) & (r[None, :] <= r[:, None])).astype(np.float32))


def ssd_layer(x, nb, conv_state, ssm_state, seg, w, tm_mm):
    T, D = x.shape
    L = T // nb
    proj = norm_matmul(x, w["norm"], w["w_in"], tm_mm, 1024)
    conv0 = jnp.pad(conv_state, ((0, 0), (SUBLANES - (SSD_CONV - 1), 0), (0, 0)))
    xc, dt, tail = ssd_pre(proj.reshape(nb, L, -1), conv0, w["conv_w"], w["conv_b"], w["dt_bias"],
                           min(L, SSD_CHUNK))
    rows = SSD_CHUNK if seg < SSD_CHUNK else L
    ngrp = T // rows
    nseg = SSD_CHUNK // seg
    st0 = ssm_state.reshape((ngrp, nseg) + ssm_state.shape[1:])
    y, st = ssd_scan(xc.reshape(ngrp, rows, -1), dt.reshape(ngrp, rows, -1), w["a_log"], _segment_tri(seg), st0, seg)
    x_new = ssd_post(y.reshape(T, -1), xc.reshape(T, -1), proj, w["d_exp"], w["norm_g"], w["w_out"], x, tm_mm)
    return x_new, tail[:, SUBLANES - (SSD_CONV - 1):], st.reshape(ssm_state.shape)


def _bias_of(d, rb_ref, h):
    val = jnp.full(d.shape, rb_ref[0, h], F32)
    for k in range(1, N_BUCKETS):
        val = jnp.where(d >= BUCKET_THR[k], rb_ref[k, h], val)
    return jnp.where(d >= 0, val - rb_ref[N_BUCKETS - 1, h], 0.0)


def _bias_tables_kernel(rb_ref, tp_ref, ts_ref, *, past_len):
    r = lax.broadcasted_iota(jnp.int32, (Q_TILE, Q_TILE), 0)
    c = lax.broadcasted_iota(jnp.int32, (Q_TILE, Q_TILE), 1)
    t = lax.broadcasted_iota(jnp.int32, (SUBLANES, 1024), 0)
    j = lax.broadcasted_iota(jnp.int32, (SUBLANES, 1024), 1)
    win0 = past_len - WINDOW
    d_s = jnp.where(j < 256, past_len + t - (CMP_STRIDE * j + CMP_BLOCK - 1),
                    jnp.where(j < 384, past_len + t - (past_len - PAGE_SIZE + (j - 256)),
                              jnp.where(j < 512, t - (j - 384), past_len + t - (win0 + (j - 512)))))

    def body(h, carry):
        tp_ref[h, 0] = _bias_of(r - c, rb_ref, h)
        tp_ref[h, 1] = _bias_of(Q_TILE + r - c, rb_ref, h)
        tp_ref[h, 2] = _bias_of(r - CMP_STRIDE * (c - CMP_FRONT) - (CMP_BLOCK - 1), rb_ref, h)
        ts_ref[h] = _bias_of(d_s, rb_ref, h)
        return carry

    lax.fori_loop(0, ATT_HEADS, body, 0)


def bias_tables(rel_bias, past_len):
    return pl.pallas_call(
        functools.partial(_bias_tables_kernel, past_len=past_len),
        in_specs=[pl.BlockSpec(memory_space=pltpu.SMEM)],
        out_specs=[pl.BlockSpec(memory_space=pltpu.VMEM), pl.BlockSpec(memory_space=pltpu.VMEM)],
        out_shape=[jax.ShapeDtypeStruct((ATT_HEADS, 3, Q_TILE, Q_TILE), F32),
                   jax.ShapeDtypeStruct((ATT_HEADS, SUBLANES, 1024), F32)],
        name="bias_tables",
    )(rel_bias)


def _cmp_pre_kernel(*refs, nsrc, nsub, n_prefetch):
    refs = refs[n_prefetch:]
    srcs = refs[:nsrc]
    w_ref, o_ref, stage = refs[nsrc], refs[nsrc + 1], refs[nsrc + 2]
    rows = nsub * CMP_STRIDE
    n = nsrc * nsub
    for f in range(2):
        xs = []
        for pair in range(2):
            l0 = f * 256 + pair * LANES
            for si, src in enumerate(srcs):
                stage[si * rows:(si + 1) * rows] = src[0, :, l0:l0 + LANES]
            cols = [stage[pl.ds(s, n, stride=CMP_STRIDE)] for s in range(CMP_STRIDE)]
            xs.append(jnp.concatenate(cols, axis=1))
        x2 = jnp.concatenate(xs, axis=0).astype(BF16)
        pre = _dot(x2, w_ref[f])
        for pair in range(2):
            o_ref[0, f, 2 * pair] = pre[pair * n:(pair + 1) * n, 0:256]
            o_ref[0, f, 2 * pair + 1] = pre[pair * n:(pair + 1) * n, 256:512]


def _cmp_fin_kernel(pre_ref, pe_ref, w1_ref, w2_ref, o_ref, scr, *, n):
    pre = pre_ref[0, 0, 0]
    scr[0:n] = pre[:, CMP_HIDDEN:2 * CMP_HIDDEN]
    scr[n:n + SUBLANES] = jnp.zeros((SUBLANES, CMP_HIDDEN), F32)
    cvec = _dot(pe_ref[0].astype(BF16), w1_ref[0])[0:1]
    hid = pre[:, 0:CMP_HIDDEN] + scr[pl.ds(1, n)] + cvec
    o_ref[0, 0, 0] = _dot(_silu(hid).astype(BF16), w2_ref[0])


def cmp_finish(pre, pe8, w1, w2d):
    B, _, KV, n, _ = pre.shape
    return pl.pallas_call(
        functools.partial(_cmp_fin_kernel, n=n),
        grid=(B, 2, KV),
        in_specs=[pl.BlockSpec((1, 1, 1, n, 256), lambda b, f, k: (b, f, k, 0, 0)),
                  pl.BlockSpec((1, SUBLANES, w1.shape[1]), lambda b, f, k: (f, 0, 0)),
                  pl.BlockSpec((1,) + w1.shape[1:], lambda b, f, k: (f, 0, 0)),
                  pl.BlockSpec((1,) + w2d.shape[1:], lambda b, f, k: (f, 0, 0))],
        out_specs=pl.BlockSpec((1, 1, 1, n, LANES), lambda b, f, k: (b, f, k, 0, 0)),
        out_shape=jax.ShapeDtypeStruct((B, 2, KV, n, LANES), F32),
        scratch_shapes=[pltpu.VMEM((n + SUBLANES, CMP_HIDDEN), F32)],
        compiler_params=_cparams("parallel", "parallel", "parallel"),
        name="cmp_finish",
    )(pre, pe8, w1, w2d)


def cmp_pre_prompt(proj, w_pair, rows):
    B, L, _ = proj.shape
    nsub = rows // CMP_STRIDE
    return pl.pallas_call(
        functools.partial(_cmp_pre_kernel, nsrc=1, nsub=nsub, n_prefetch=0),
        grid=(B, L // rows),
        in_specs=[pl.BlockSpec((1, rows, 512), lambda b, t: (b, t, 2)),
                  pl.BlockSpec(w_pair.shape, lambda b, t: (0, 0, 0))],
        out_specs=pl.BlockSpec((1, 2, ATT_KV, nsub, 256), lambda b, t: (b, 0, 0, t, 0)),
        out_shape=jax.ShapeDtypeStruct((B, 2, ATT_KV, L // CMP_STRIDE, 256), F32),
        scratch_shapes=[pltpu.VMEM((rows, LANES), F32)],
        compiler_params=_cparams("parallel", "parallel"),
        name="cmp_pre_prompt",
    )(proj, w_pair)


def cmp_pre_sample(page_table, cache, new_page, w_pair):
    B, n_pages = page_table.shape
    nsub = PAGE_SIZE // CMP_STRIDE
    nsrc = n_pages + 1

    def page_spec(p):
        return pl.BlockSpec((1, PAGE_SIZE, 512), lambda b, pt: (pt[b, p], 0, 0))

    grid_spec = pltpu.PrefetchScalarGridSpec(
        num_scalar_prefetch=1,
        grid=(B,),
        in_specs=[page_spec(p) for p in range(n_pages)]
        + [pl.BlockSpec((1, PAGE_SIZE, 512), lambda b, pt: (b, 0, 0)),
           pl.BlockSpec(w_pair.shape, lambda b, pt: (0, 0, 0))],
        out_specs=pl.BlockSpec((1, 2, ATT_KV, nsrc * nsub, 256), lambda b, pt: (b, 0, 0, 0, 0)),
        scratch_shapes=[pltpu.VMEM((nsrc * PAGE_SIZE, LANES), F32)],
    )
    return pl.pallas_call(
        functools.partial(_cmp_pre_kernel, nsrc=nsrc, nsub=nsub, n_prefetch=1),
        grid_spec=grid_spec,
        out_shape=jax.ShapeDtypeStruct((B, 2, ATT_KV, nsrc * nsub, 256), F32),
        compiler_params=_cparams("parallel"),
        name="cmp_pre_sample",
    )(page_table, *([cache] * n_pages), new_page, w_pair)


def _importance_matrix(n_rows, front):
    m = np.arange(n_rows)[:, None] - front
    j = np.arange(LANES)[None, :]
    ratio = SEL_BLOCK // CMP_STRIDE
    a = ((m >= ratio * j) & (m <= ratio * j + ratio - 1)).astype(np.float32) \
        + ((m >= ratio * j - 1) & (m <= ratio * j + ratio - 2)).astype(np.float32)
    return jnp.asarray(a)


def _softmax_rows(s):
    m = jnp.max(s, axis=-1, keepdims=True)
    m = jnp.where(m > NEG_INF, m, 0.0)
    e = jnp.exp(s - m)
    return e / jnp.maximum(jnp.sum(e, axis=-1, keepdims=True), 1e-30)


def _top_blocks(score, n_sel, axis=1):
    jb = lax.broadcasted_iota(jnp.int32, score.shape, axis).astype(F32)
    sel = jnp.zeros(score.shape, F32)
    sc = score
    for _ in range(n_sel):
        mx = jnp.max(sc, axis=axis, keepdims=True)
        idx = jnp.min(jnp.where(sc == mx, jb, 1e9), axis=axis, keepdims=True)
        pick = jb == idx
        sel = jnp.where(pick, 1.0, sel)
        sc = jnp.where(pick, NEG_INF, sc)
    return sel


def _tile4(x):
    return jnp.concatenate([x, x, x, x], axis=0)


def _attn_prompt_kernel(q_ref, gate_ref, kc_ref, vc_ref, ks_ref, vs_ref, kw_ref, vw_ref, tp_ref, amat_ref, o_ref,
                        mask_scr, mx_scr, acc_scr, *, nsub):
    i = pl.program_id(2)
    QT = Q_TILE
    R2 = 2 * QT
    low = lax.broadcasted_iota(jnp.int32, (QT, LANES), 1) < ATT_HEAD_DIM
    qpp = jnp.concatenate([q_ref[0, :, 0:LANES], q_ref[0, :, LANES:2 * LANES]], axis=0).astype(BF16)
    r1 = lax.broadcasted_iota(jnp.int32, (QT, 1), 0)
    c1 = lax.broadcasted_iota(jnp.int32, (1, LANES), 1)
    lower = r1 >= c1
    upper = r1 <= c1
    i_vec = jnp.full((QT, LANES), i, jnp.int32)

    def half_masks(n):
        lo = lax.broadcasted_iota(jnp.int32, (n, LANES), 1) < ATT_HEAD_DIM
        return jnp.where(lo, 1.0, 0.0).astype(BF16), jnp.where(lo, 0.0, 1.0).astype(BF16)

    def bd(x2):
        m_lo, m_hi = half_masks(x2.shape[0])
        return jnp.concatenate([x2 * m_lo, x2 * m_hi], axis=0)

    def spread(madd):
        m2 = jnp.concatenate([madd, madd], axis=1)
        return jnp.concatenate([m2, m2], axis=0)

    def bias_full(kind):
        return jnp.concatenate([jnp.concatenate([tp_ref[2 * p, kind], tp_ref[2 * p + 1, kind]], axis=1)
                                for p in range(2)], axis=0)

    def row_max2(x):
        parts = []
        for w in range(2):
            m = jnp.max(x[:, LANES * w:LANES * (w + 1)], axis=-1, keepdims=True)
            parts.append(jnp.broadcast_to(jnp.where(m > NEG_INF, m, 0.0), (R2, LANES)))
        return jnp.concatenate(parts, axis=1)

    def widen(m2, reps):
        if reps == 1:
            return m2
        return jnp.concatenate([m2[:, :LANES]] * reps + [m2[:, LANES:]] * reps, axis=1)

    def kv_rows(ref, start, n):
        return ref[0, 0, 0, pl.ds(pl.multiple_of(start, n), n), :]

    def key_cols(ref, t0, ntiles):
        tiles = [ref[0, 0, 0, t0 + c] for c in range(ntiles)]
        return tiles[0] if ntiles == 1 else jnp.concatenate(tiles, axis=1)

    def scores(kt2, add):
        n = kt2.shape[1]
        top = lax.broadcasted_iota(jnp.int32, (LANES, n), 0) < ATT_HEAD_DIM
        kbd_t = jnp.concatenate([kt2 * jnp.where(top, 1.0, 0.0).astype(BF16),
                                 kt2 * jnp.where(top, 0.0, 1.0).astype(BF16)], axis=1)
        return _dot(qpp, kbd_t) + add

    def scores_nt(k2, add):
        return _dot_nt(qpp, bd(k2)) + add

    def weighted(s, m_wide, v2):
        ones = jnp.concatenate(half_masks(v2.shape[0]), axis=0)
        rhs = jnp.concatenate([bd(v2), ones], axis=1)
        return _dot(jnp.exp(s - m_wide).astype(BF16), rhs)

    def accumulate(s, m_wide, v2):
        acc_scr[...] += weighted(s, m_wide, v2)

    def normalized(acc):
        return acc[:, :LANES] / jnp.maximum(acc[:, LANES:], 1e-30)

    nback = WINDOW // QT
    ws, wt = [], []
    for back in range(nback, -1, -1):
        tc = jnp.maximum(i - back, 0)
        ok = i_vec >= back
        if back == nback:
            ok = ok & upper
        if back == 0:
            ok = lower
        madd = spread(jnp.where(ok, 0.0, NEG_INF))
        ws.append(scores(key_cols(kw_ref, tc, 1), madd + bias_full(back) if back <= 1 else madd))
        wt.append(tc)
    mel = ws[0]
    for s in ws[1:]:
        mel = jnp.maximum(mel, s)
    m2_w = row_max2(mel)
    acc_w = None
    for s, tc in zip(ws, wt):
        term = weighted(s, m2_w, kv_rows(vw_ref, tc * QT, QT))
        acc_w = term if acc_w is None else acc_w + term
    o_w = normalized(acc_w)

    st = pl.multiple_of(i * SUBLANES, SUBLANES)
    d_cmp = r1 - CMP_STRIDE * (c1 - CMP_FRONT) - (CMP_BLOCK - 1)
    ctiles = []
    for j in range(nsub // QT):
        rows = slice(CMP_FRONT + j * QT, CMP_FRONT + (j + 1) * QT)
        ok = jnp.broadcast_to(j * QT + c1 < i * SUBLANES - CMP_FRONT, (QT, LANES))
        ctiles.append((kc_ref[0, 0, 0, rows, :], vc_ref[0, 0, 0, rows, :], amat_ref[rows, :], ok, False))
    ctiles.append((kc_ref[0, 0, 0, pl.ds(st, QT), :], vc_ref[0, 0, 0, pl.ds(st, QT), :], amat_ref[pl.ds(st, QT), :],
                   (d_cmp >= 0) & (c1 >= CMP_FRONT - i * SUBLANES), True))
    ss = []
    for kt, _, _, ok, diag in ctiles:
        add = spread(jnp.where(ok, 0.0, NEG_INF))
        ss.append(scores_nt(kt.astype(BF16), add + bias_full(2) if diag else add))
    mel = ss[0]
    for s in ss[1:]:
        mel = jnp.maximum(mel, s)
    m2 = row_max2(mel)
    es = [jnp.exp(s - m2) for s in ss]
    lel = es[0]
    for e in es[1:]:
        lel = lel + e
    inv = []
    for w in range(2):
        l = jnp.sum(lel[:, LANES * w:LANES * (w + 1)], axis=-1, keepdims=True)
        inv.append(jnp.broadcast_to(1.0 / jnp.maximum(l, 1e-30), (R2, LANES)))
    inv2 = jnp.concatenate(inv, axis=1)
    o_c = None
    phs = []
    for e, (_, vt, _, _, _) in zip(es, ctiles):
        pn = e * inv2
        term = _dot(pn.astype(BF16), bd(vt.astype(BF16)))
        o_c = term if o_c is None else o_c + term
        phs.append(pn[:QT, :LANES] + pn[:QT, LANES:] + pn[QT:, :LANES] + pn[QT:, LANES:])
    imp = _dot_exact(jnp.concatenate(phs, axis=1),
                     jnp.concatenate([t[2] for t in ctiles], axis=0))

    qblk = 2 * i + (c1 >= SEL_BLOCK).astype(jnp.int32)
    lag = qblk - r1
    allowed = lag >= 0
    forced = (r1 == 0) | (allowed & (lag < N_LOCAL))
    score_t = jnp.where(allowed, imp.T + jnp.where(forced, FORCE_BONUS, 0.0), -1.0)
    sel = _top_blocks(score_t, N_SEL, axis=0).T.astype(BF16)

    FK = FAR_KEYS
    tm1 = jnp.maximum(i - 1, 0)
    limit = tm1 * QT
    n_grp = jnp.right_shift(tm1 + FK // QT - 1, (FK // QT).bit_length() - 1)
    shift = SEL_BLOCK.bit_length() - 1

    def key_mask(key0, n):
        jb = lax.broadcasted_iota(jnp.int32, (LANES, n), 0)
        key = key0 + lax.broadcasted_iota(jnp.int32, (LANES, n), 1)
        hit = jb == jnp.right_shift(key, shift)
        if n == FK:
            hit = hit & (key < limit)
        return jnp.where(_dot(sel, jnp.where(hit, 1.0, 0.0).astype(BF16)) > 0.5, 0.0, NEG_INF)

    mk_d = spread(jnp.where(lower, key_mask(i * QT, QT), NEG_INF))
    mk_s = spread(jnp.where(i_vec >= 1, key_mask(tm1 * QT, QT), NEG_INF))

    def edge_scores():
        return (scores(key_cols(ks_ref, i, 1), mk_d + bias_full(0)),
                scores(key_cols(ks_ref, tm1, 1), mk_s + bias_full(1)))

    s_d, s_s = edge_scores()
    mx_scr[...] = jnp.maximum(s_d, s_s)

    def max_body(gi, c):
        mk = key_mask(gi * FK, FK)
        mask_scr[gi] = mk
        s = scores(key_cols(ks_ref, gi * (FK // QT), FK // QT), spread(mk))
        halves = []
        for w in range(2):
            m = s[:, w * FK:w * FK + LANES]
            for c4 in range(1, FK // LANES):
                m = jnp.maximum(m, s[:, w * FK + c4 * LANES:w * FK + (c4 + 1) * LANES])
            halves.append(m)
        mx_scr[...] = jnp.maximum(mx_scr[...], jnp.concatenate(halves, axis=1))
        return c

    lax.fori_loop(0, n_grp, max_body, 0)
    m2_s = row_max2(mx_scr[...])
    acc_scr[...] = jnp.zeros_like(acc_scr)
    s_d, s_s = edge_scores()
    accumulate(s_d, m2_s, kv_rows(vs_ref, i * QT, QT))
    accumulate(s_s, m2_s, kv_rows(vs_ref, tm1 * QT, QT))
    m2_wide = widen(m2_s, FK // LANES)

    def sum_body(gi, c):
        s = scores(key_cols(ks_ref, gi * (FK // QT), FK // QT), spread(mask_scr[gi]))
        accumulate(s, m2_wide, kv_rows(vs_ref, gi * FK, FK))
        return c

    lax.fori_loop(0, n_grp, sum_body, 0)
    o_s = normalized(acc_scr[...])

    g = jax.nn.sigmoid(gate_ref[0])

    def gate(br):
        tiles = []
        for p in range(2):
            c0 = br * ATT_HG + 2 * p
            tiles.append(jnp.where(low, jnp.broadcast_to(g[:, c0:c0 + 1], (QT, LANES)),
                                   jnp.broadcast_to(g[:, c0 + 1:c0 + 2], (QT, LANES))))
        return jnp.concatenate(tiles, axis=0)

    o = gate(0) * o_c + gate(1) * o_s + gate(2) * o_w
    o_ref[0] = jnp.concatenate([o[:QT], o[QT:]], axis=1)


def attn_prompt(proj, cmp_kv, kt, vv, tp, amat):
    B, L, _ = proj.shape
    nsub = L // CMP_STRIDE
    ncp = cmp_kv.shape[3]

    def k_spec(f):
        return pl.BlockSpec((1, 1, 1, L // Q_TILE, LANES, Q_TILE), lambda b, k, i: (b, f, k, 0, 0, 0))

    def v_spec(f):
        return pl.BlockSpec((1, 1, 1, L, LANES), lambda b, k, i: (b, f, k, 0, 0))

    def cmp_spec(f):
        return pl.BlockSpec((1, 1, 1, ncp, LANES), lambda b, k, i: (b, f, k, 0, 0))

    return pl.pallas_call(
        functools.partial(_attn_prompt_kernel, nsub=nsub),
        grid=(B, ATT_KV, L // Q_TILE),
        in_specs=[pl.BlockSpec((1, Q_TILE, 256), lambda b, k, i: (b, i, k)),
                  pl.BlockSpec((1, Q_TILE, LANES), lambda b, k, i: (b, i, 20 + k)),
                  cmp_spec(0), cmp_spec(1), k_spec(0), v_spec(0), k_spec(1), v_spec(1),
                  pl.BlockSpec((ATT_HG, 3, Q_TILE, Q_TILE), lambda b, k, i: (k, 0, 0, 0)),
                  pl.BlockSpec(amat.shape, lambda b, k, i: (0, 0))],
        out_specs=pl.BlockSpec((1, Q_TILE, 256), lambda b, k, i: (b, i, k)),
        out_shape=jax.ShapeDtypeStruct((B, L, ATT_HEADS * ATT_HEAD_DIM), F32),
        scratch_shapes=[pltpu.VMEM((L // FAR_KEYS, Q_TILE, FAR_KEYS), F32),
                        pltpu.VMEM((2 * Q_TILE, 2 * LANES), F32),
                        pltpu.VMEM((2 * Q_TILE, 2 * LANES), F32)],
        compiler_params=_cparams("parallel", "parallel", "arbitrary"),
        name="attn_prompt",
    )(proj, proj, cmp_kv, cmp_kv, kt, vv, kt, vv, tp, amat)


def _attn_sample_kernel(*refs, n_pages, past_len):
    pages = refs[1:1 + n_pages]
    q_ref, gate_ref, new_s_ref, new_w_ref, win_ref, cmp_ref, ts_ref, amat_ref, o_ref = refs[1 + n_pages:]
    T = SUBLANES
    R = ATT_HG * T
    n_cmp_rows = cmp_ref.shape[3]
    lane_t = lax.broadcasted_iota(jnp.int32, (T, LANES), 1)
    low_t = lane_t < ATT_HEAD_DIM
    t_r = lax.broadcasted_iota(jnp.int32, (R, 1), 0) & (T - 1)
    c1 = lax.broadcasted_iota(jnp.int32, (1, LANES), 1)
    g = jax.nn.sigmoid(gate_ref[0])
    zeros_new = jnp.zeros((LANES - T, LANES), F32)
    out_tiles = []
    for kv in range(ATT_KV):
        par = kv % 2
        pl0 = LANES * (kv // 2)
        keep = (lane_t >= ATT_HEAD_DIM) if par else low_t
        qrows = []
        for hg in range(ATT_HG):
            h = kv * ATT_HG + hg
            q2 = q_ref[0, :, LANES * (h // 2):LANES * (h // 2 + 1)]
            if h % 2 != par:
                q2 = pltpu.roll(q2, ATT_HEAD_DIM, axis=1)
            qrows.append(jnp.where(keep, q2, 0.0))
        qs = jnp.concatenate(qrows, axis=0).astype(BF16)
        tab = ts_ref[kv * ATT_HG:(kv + 1) * ATT_HG].reshape(R, 1024)

        kc = jnp.concatenate([cmp_ref[0, 0, kv], jnp.zeros((256 - n_cmp_rows, LANES), F32)], axis=0)
        vc = jnp.concatenate([cmp_ref[0, 1, kv], jnp.zeros((256 - n_cmp_rows, LANES), F32)], axis=0)
        n_c = lax.broadcasted_iota(jnp.int32, (1, 256), 1)
        d_c = past_len + t_r - (CMP_STRIDE * n_c + CMP_BLOCK - 1)
        s_c = _dot_nt(qs, kc.astype(BF16)) + tab[:, 0:256]
        p_c = _softmax_rows(jnp.where(d_c >= 0, s_c, NEG_INF))
        o_c = _dot(p_c.astype(BF16), vc.astype(BF16))
        p_sum = p_c[0:T] + p_c[T:2 * T] + p_c[2 * T:3 * T] + p_c[3 * T:4 * T]
        imp = _dot_exact(p_sum, amat_ref[...])

        qblk = jnp.right_shift(past_len + lax.broadcasted_iota(jnp.int32, (T, 1), 0), SEL_BLOCK.bit_length() - 1)
        lag = qblk - c1
        allowed = lag >= 0
        forced = (c1 == 0) | (allowed & (lag < N_LOCAL))
        score = jnp.where(allowed, imp + jnp.where(forced, FORCE_BONUS, 0.0), -1.0)
        sel = _top_blocks(score, N_SEL)

        s_parts, m_parts, v_parts = [], [], []
        for p in range(n_pages):
            kp = pages[p][0, :, pl0:pl0 + LANES].astype(BF16)
            s = _dot_nt(qs, kp)
            if p == n_pages - 1:
                s = s + tab[:, 256:384]
            s_parts.append(s)
            selp = jnp.where(c1 < SEL_BLOCK, sel[:, 2 * p:2 * p + 1], sel[:, 2 * p + 1:2 * p + 2])
            m_parts.append(_tile4(selp) > 0.5)
            v_parts.append(pages[p][0, :, 256 + pl0:256 + pl0 + LANES].astype(BF16))
        k_new = jnp.concatenate([new_s_ref[0, :, pl0:pl0 + LANES], zeros_new], axis=0).astype(BF16)
        v_new = jnp.concatenate([new_s_ref[0, :, 256 + pl0:256 + pl0 + LANES], zeros_new], axis=0).astype(BF16)
        nb = 2 * n_pages
        new_ok = (c1 <= t_r) & (c1 < T)
        s_parts.append(_dot_nt(qs, k_new) + tab[:, 384:512])
        m_parts.append((_tile4(jnp.broadcast_to(sel[:, nb:nb + 1], (T, LANES))) > 0.5) & new_ok)
        v_parts.append(v_new)
        s_all = jnp.where(jnp.concatenate(m_parts, axis=1), jnp.concatenate(s_parts, axis=1), NEG_INF)
        p_s = _softmax_rows(s_all).astype(BF16)
        o_s = _dot(p_s[:, 0:LANES], v_parts[0])
        for p in range(1, n_pages + 1):
            o_s = o_s + _dot(p_s[:, LANES * p:LANES * (p + 1)], v_parts[p])

        kw = win_ref[0, :, pl0:pl0 + LANES].astype(BF16)
        vw = win_ref[0, :, 256 + pl0:256 + pl0 + LANES].astype(BF16)
        kw_new = jnp.concatenate([new_w_ref[0, :, pl0:pl0 + LANES], zeros_new], axis=0).astype(BF16)
        vw_new = jnp.concatenate([new_w_ref[0, :, 256 + pl0:256 + pl0 + LANES], zeros_new], axis=0).astype(BF16)
        c_w = lax.broadcasted_iota(jnp.int32, (1, WINDOW), 1)
        s_w = jnp.concatenate([jnp.where(c_w >= t_r, _dot_nt(qs, kw) + tab[:, 512:1024], NEG_INF),
                               jnp.where(new_ok, _dot_nt(qs, kw_new) + tab[:, 384:512], NEG_INF)], axis=1)
        p_w = _softmax_rows(s_w).astype(BF16)
        o_w = _dot(p_w[:, 0:WINDOW], vw) + _dot(p_w[:, WINDOW:], vw_new)

        gk = g[:, LANES * kv:LANES * (kv + 1)]
        for pair in range(2):
            halves = []
            for which in range(2):
                hg = 2 * pair + which
                rows = slice(hg * T, (hg + 1) * T)
                o = gk[:, hg:hg + 1] * o_c[rows] + gk[:, 4 + hg:5 + hg] * o_s[rows] + gk[:, 8 + hg:9 + hg] * o_w[rows]
                if which != par:
                    o = pltpu.roll(o, ATT_HEAD_DIM, axis=1)
                halves.append(o)
            out_tiles.append(jnp.where(low_t, halves[0], halves[1]))
    o_ref[0] = jnp.concatenate(out_tiles, axis=1)


def attn_sample(page_table, cache, proj, win, cmp_kv, ts, amat, past_len):
    B, n_pages = page_table.shape
    T = proj.shape[1]

    def page_spec(p):
        return pl.BlockSpec((1, PAGE_SIZE, 512), lambda b, pt: (pt[b, p], 0, 1))

    grid_spec = pltpu.PrefetchScalarGridSpec(
        num_scalar_prefetch=1,
        grid=(B,),
        in_specs=[page_spec(p) for p in range(n_pages)]
        + [pl.BlockSpec((1, T, 1024), lambda b, pt: (b, 0, 0)),
           pl.BlockSpec((1, T, 512), lambda b, pt: (b, 0, 5)),
           pl.BlockSpec((1, T, 512), lambda b, pt: (b, 0, 3)),
           pl.BlockSpec((1, T, 512), lambda b, pt: (b, 0, 4)),
           pl.BlockSpec((1,) + win.shape[1:], lambda b, pt: (b, 0, 0)),
           pl.BlockSpec((1,) + cmp_kv.shape[1:], lambda b, pt: (b, 0, 0, 0, 0)),
           pl.BlockSpec(ts.shape, lambda b, pt: (0, 0, 0)),
           pl.BlockSpec(amat.shape, lambda b, pt: (0, 0))],
        out_specs=pl.BlockSpec((1, T, 1024), lambda b, pt: (b, 0, 0)),
    )
    return pl.pallas_call(
        functools.partial(_attn_sample_kernel, n_pages=n_pages, past_len=past_len),
        grid_spec=grid_spec,
        out_shape=jax.ShapeDtypeStruct((B, T, ATT_HEADS * ATT_HEAD_DIM), F32),
        compiler_params=_cparams("parallel"),
        name="attn_sample",
    )(page_table, *([cache] * n_pages), proj, proj, proj, proj, win, cmp_kv, ts, amat)


def _ssd_weights(i, li, norm_mix, ssd_w_in, ssd_conv_w, ssd_conv_b, ssd_dt_bias, ssd_a_log, ssd_d, ssd_norm, ssd_w_out):
    d_inner = ssd_w_out.shape[1]
    conv_dim = ssd_conv_w.shape[2]
    heads = ssd_dt_bias.shape[1]
    hpg = heads // SSD_GROUPS
    w = ssd_w_in[li]

    def per_group(v):
        v = v.reshape(v.shape[:-1] + (SSD_GROUPS, hpg))
        return jnp.pad(v, [(0, 0)] * (v.ndim - 1) + [(0, LANES - hpg)]).reshape(v.shape[:-2] + (SSD_GROUPS * LANES,))

    w_dt = per_group(w[:, d_inner + conv_dim:])
    w_in = jnp.concatenate([w[:, :d_inner], w_dt, jnp.zeros_like(w_dt), w[:, d_inner:d_inner + conv_dim]], axis=1)
    return dict(
        norm=norm_mix[i][None],
        w_in=w_in.astype(BF16),
        conv_w=ssd_conv_w[li], conv_b=ssd_conv_b[li][None],
        dt_bias=per_group(ssd_dt_bias[li])[None],
        a_log=per_group(ssd_a_log[li]).reshape(SSD_GROUPS, 1, LANES),
        d_exp=jnp.repeat(ssd_d[li], SSD_HEAD_DIM)[None],
        norm_g=ssd_norm[li][None],
        w_out=ssd_w_out[li].astype(BF16),
    )


def _nsa_in_weight(w):
    q_dim = ATT_HEADS * ATT_HEAD_DIM
    kv_dim = 6 * ATT_KV * ATT_HEAD_DIM
    idx = np.zeros((ATT_KV, LANES), np.int32)
    ok = np.zeros((ATT_KV, LANES), bool)
    for kv in range(ATT_KV):
        for br in range(3):
            for hg in range(ATT_HG):
                idx[kv, br * ATT_HG + hg] = q_dim + kv_dim + (kv * ATT_HG + hg) * 3 + br
                ok[kv, br * ATT_HG + hg] = True
    w_g = jnp.where(jnp.asarray(ok.reshape(-1))[None, :], w[:, idx.reshape(-1)], 0.0)
    return jnp.concatenate([w[:, :q_dim] * (ATT_HEAD_DIM ** -0.5), w[:, q_dim:q_dim + kv_dim], w_g], axis=1).astype(BF16)


def _prompt_kv_layouts(kv6):
    B, L = kv6.shape[:2]
    k = jnp.transpose(kv6[:, :, 2::2].astype(BF16), (0, 2, 3, 4, 1))
    k = jnp.concatenate([k, k], axis=3).reshape(B, 2, ATT_KV, LANES, L // Q_TILE, Q_TILE)
    v = jnp.transpose(kv6[:, :, 3::2].astype(BF16), (0, 2, 3, 1, 4))
    return jnp.transpose(k, (0, 1, 2, 4, 3, 5)), jnp.concatenate([v, v], axis=-1)


def _cmp_pair_weight(w1):
    w1r = w1.reshape(2, 2, CMP_STRIDE, ATT_HEAD_DIM, CMP_HIDDEN)
    eye = jnp.eye(2, dtype=w1.dtype)
    wp = jnp.einsum("fjsde,wv->fswdvje", w1r, eye)
    return wp.reshape(2, CMP_STRIDE * 2 * ATT_HEAD_DIM, 2 * 2 * CMP_HIDDEN).astype(BF16)


def kernel(x_prompt, x_sample, cache_nsa_kv, state_nsa_win, state_ssm, state_conv, state_pool, page_table, rel_bias,
           norm_mix, norm_ffn, norm_out, ffn_w_up, ffn_w_down, ssd_w_in, ssd_conv_w, ssd_conv_b, ssd_dt_bias,
           ssd_a_log, ssd_d, ssd_norm, ssd_w_out, pool_w, pool_scale, nsa_w_in, nsa_cmp_pe, nsa_cmp_w1, nsa_cmp_w2,
           nsa_w_out):
    bp, lp, d_model = x_prompt.shape
    bs, ls, _ = x_sample.shape
    depth = norm_mix.shape[0]
    n_pages = page_table.shape[1]
    past_len = n_pages * PAGE_SIZE
    assert ls == SUBLANES and lp % (16 * Q_TILE) == 0 and past_len >= WINDOW and state_nsa_win.shape[2] == WINDOW
    xp = x_prompt.reshape(bp * lp, d_model)
    xs = x_sample.reshape(bs * ls, d_model)
    tm_p, tm_s = 512, 512
    outs = {k: [] for k in ("kv_p", "kv_s", "win_p", "win_s", "ssm_p", "ssm_s", "conv_p", "conv_s", "pool_p", "pool_s")}
    for i in range(depth):
        kind, li = i % 3, i // 3
        if kind == 0:
            w = _ssd_weights(i, li, norm_mix, ssd_w_in, ssd_conv_w, ssd_conv_b, ssd_dt_bias, ssd_a_log, ssd_d,
                             ssd_norm, ssd_w_out)
            conv_dim = ssd_conv_w.shape[2]
            xp, c_p, s_p = ssd_layer(xp, bp, jnp.zeros((bp, SSD_CONV - 1, conv_dim), F32),
                                     jnp.zeros((bp,) + state_ssm.shape[2:], F32), SSD_CHUNK, w, tm_p)
            xs, c_s, s_s = ssd_layer(xs, bs, state_conv[li], state_ssm[li], ls, w, tm_s)
            outs["conv_p"].append(c_p)
            outs["conv_s"].append(c_s)
            outs["ssm_p"].append(s_p)
            outs["ssm_s"].append(s_s)
        elif kind == 1:
            g = norm_mix[i][None]
            pw = pool_w[li].astype(BF16)
            sc = pool_scale[li][None]
            xp3, tail_p = pool_mixer_residual(xp.reshape(bp, lp, d_model), jnp.zeros((bp, POOL_HALO, d_model), F32),
                                              g, pw, sc, 0, 512)
            halo_s = jnp.pad(state_pool[li], ((0, 0), (1, 0), (0, 0)))
            xs3, tail_s = pool_mixer_residual(xs.reshape(bs, ls, d_model), halo_s, g, pw, sc, past_len, ls)
            xp, xs = xp3.reshape(bp * lp, d_model), xs3.reshape(bs * ls, d_model)
            outs["pool_p"].append(tail_p[:, 1:])
            outs["pool_s"].append(tail_s[:, 1:])
        else:
            g = norm_mix[i][None]
            w_in = _nsa_in_weight(nsa_w_in[li])
            w_pair = _cmp_pair_weight(nsa_cmp_w1[li])
            pe8 = jnp.broadcast_to(nsa_cmp_pe[li].reshape(2, 1, -1), (2, SUBLANES, CMP_BLOCK * ATT_HEAD_DIM))
            w1 = nsa_cmp_w1[li].astype(BF16)
            w2d = jnp.concatenate([nsa_cmp_w2[li], nsa_cmp_w2[li]], axis=-1).astype(BF16)
            tp, ts = bias_tables(rel_bias, past_len)
            w_out = nsa_w_out[li].astype(BF16)
            kvw = ATT_KV * ATT_HEAD_DIM
            proj_p = norm_matmul(xp, g, w_in, tm_p, 1024).reshape(bp, lp, -1)
            pre_p = cmp_pre_prompt(proj_p, w_pair, 2048)
            cmp_p = jnp.pad(cmp_finish(pre_p, pe8, w1, w2d), ((0, 0), (0, 0), (0, 0), (CMP_FRONT, CMP_BACK), (0, 0)))
            kv6 = proj_p[:, :, 1024:2560].reshape(bp, lp, 6, ATT_KV, ATT_HEAD_DIM)
            kt, vv = _prompt_kv_layouts(kv6)
            amat_p = _importance_matrix(lp // CMP_STRIDE + CMP_FRONT + CMP_BACK, CMP_FRONT)
            o_p = attn_prompt(proj_p, cmp_p, kt, vv, tp, amat_p)
            xp = matmul_residual(o_p.reshape(bp * lp, -1), w_out, xp, tm_p)
            outs["kv_p"].append(kv6[:, :, 0:4])
            outs["win_p"].append(kv6[:, lp - WINDOW:, 4:6])
            proj_s = norm_matmul(xs, g, w_in, tm_s, 1024).reshape(bs, ls, -1)
            cache = cache_nsa_kv[li].reshape(cache_nsa_kv.shape[1], PAGE_SIZE, 4 * kvw)
            new_page = jnp.pad(proj_s[:, :, 1024:1024 + 2 * kvw], ((0, 0), (0, PAGE_SIZE - ls), (0, 0)))
            pre_s = cmp_pre_sample(page_table, cache, new_page, w_pair)
            cmp_s = cmp_finish(pre_s, pe8, w1, w2d)
            win = state_nsa_win[li].reshape(bs, WINDOW, 2 * kvw)
            amat_s = _importance_matrix(256, 0)
            o_s = attn_sample(page_table, cache, proj_s, win, cmp_s, ts, amat_s, past_len)
            xs = matmul_residual(o_s.reshape(bs * ls, -1), w_out, xs, tm_s)
            kv6s = proj_s[:, :, 1024:2560].reshape(bs, ls, 6, ATT_KV, ATT_HEAD_DIM)
            outs["kv_s"].append(kv6s[:, :, 0:4])
            outs["win_s"].append(jnp.concatenate([state_nsa_win[li][:, ls:], kv6s[:, :, 4:6]], axis=1))
        last = i == depth - 1
        xp = sqrelu_mlp_residual(xp, norm_ffn[i][None], ffn_w_up[i].astype(BF16), ffn_w_down[i].astype(BF16),
                                 norm_out[None], last, tm_p, 512)
        xs = sqrelu_mlp_residual(xs, norm_ffn[i][None], ffn_w_up[i].astype(BF16), ffn_w_down[i].astype(BF16),
                                 norm_out[None], last, tm_s, 512)
    st = lambda k: jnp.stack(outs[k])
    return (xp.reshape(bp, lp, d_model), xs.reshape(bs, ls, d_model), st("kv_p"), st("kv_s"), st("win_p"),
            st("win_s"), st("ssm_p"), st("ssm_s"), st("conv_p"), st("conv_s"), st("pool_p"), st("pool_s"))
```

```python
import functools
import math

import numpy as np
import jax
import jax.numpy as jnp
from jax import lax
from jax.experimental import pallas as pl
from jax.experimental.pallas import tpu as pltpu

F32 = jnp.float32
BF16 = jnp.bfloat16
HIGHEST = lax.Precision.HIGHEST
EPS = 1e-6
NEG_INF = float("-inf")

V7X_VMEM_LIMIT_BYTES = 56 * 1024 * 1024
LANES = 128
SUBLANES = 8

D_MODEL = 1024
SSD_HEAD_DIM = 64
SSD_GROUPS = 4
SSD_STATE = 128
SSD_CONV = 4
SSD_CHUNK = 128
POOL_WINDOWS = (2, 4, 8, 16)
POOL_HALO = 16
ATT_HEADS = 16
ATT_HEAD_DIM = 64
ATT_KV = 4
ATT_HG = 4
CMP_BLOCK = 32
CMP_STRIDE = 16
CMP_HIDDEN = 128
SEL_BLOCK = 64
N_SEL = 8
N_LOCAL = 2
FORCE_BONUS = 1000.0
WINDOW = 512
Q_TILE = 128
PAGE_SIZE = 128
N_BUCKETS = 32
MAX_DISTANCE = 128
CMP_FRONT = 112
CMP_BACK = 16
FAR_KEYS = 1024


def _cparams(*sem):
    return pltpu.CompilerParams(dimension_semantics=sem, vmem_limit_bytes=V7X_VMEM_LIMIT_BYTES)


def _bucket_thresholds():
    d = np.arange(0, MAX_DISTANCE + 1)
    max_exact = N_BUCKETS // 2
    nf = np.maximum(d, 1).astype(np.float32)
    large = max_exact + (np.log(nf / np.float32(max_exact)) / np.float32(math.log(MAX_DISTANCE / max_exact))
                         * np.float32(N_BUCKETS - max_exact)).astype(np.int32)
    large = np.minimum(large, N_BUCKETS - 1)
    b = np.where(d < max_exact, d, large)
    return [int(np.argmax(b >= k)) for k in range(N_BUCKETS)]


BUCKET_THR = _bucket_thresholds()


def _rms(x, g):
    return x * lax.rsqrt(jnp.mean(x * x, axis=-1, keepdims=True) + EPS) * g


def _sigmoid(x):
    return 0.5 * jnp.tanh(0.5 * x) + 0.5


def _silu(x):
    return x * _sigmoid(x)


def _softplus(x):
    return jnp.maximum(x, 0.0) + jnp.log1p(jnp.exp(-jnp.abs(x)))


def _dot(a, b):
    return jnp.dot(a, b, preferred_element_type=F32)


def _dot_nt(a, b):
    return lax.dot_general(a, b, (((1,), (1,)), ((), ())), preferred_element_type=F32)


def _dot_exact(a, b):
    return jnp.dot(a, b, precision=HIGHEST, preferred_element_type=F32)


def _dot_split3(a, b_bf16):
    hi = a.astype(BF16)
    r1 = a - hi.astype(F32)
    mid = r1.astype(BF16)
    lo = (r1 - mid.astype(F32)).astype(BF16)
    return _dot(hi, b_bf16) + _dot(mid, b_bf16) + _dot(lo, b_bf16)


def _norm_mm_kernel(x_ref, g_ref, w_ref, o_ref, h_scr):
    @pl.when(pl.program_id(1) == 0)
    def _():
        h_scr[...] = _rms(x_ref[...], g_ref[...]).astype(BF16)

    o_ref[...] = _dot(h_scr[...], w_ref[...])


def norm_matmul(x, g, w, tm, tn):
    T, D = x.shape
    N = w.shape[1]
    return pl.pallas_call(
        _norm_mm_kernel,
        grid=(T // tm, N // tn),
        in_specs=[pl.BlockSpec((tm, D), lambda i, j: (i, 0)),
                  pl.BlockSpec((1, D), lambda i, j: (0, 0)),
                  pl.BlockSpec((D, tn), lambda i, j: (0, j))],
        out_specs=pl.BlockSpec((tm, tn), lambda i, j: (i, j)),
        out_shape=jax.ShapeDtypeStruct((T, N), F32),
        scratch_shapes=[pltpu.VMEM((tm, D), BF16)],
        compiler_params=_cparams("parallel", "arbitrary"),
        name="norm_matmul",
    )(x, g, w)


def _mlp_kernel(x_ref, g_ref, wu_ref, wd_ref, go_ref, o_ref, h_scr, acc, *, final_norm):
    j = pl.program_id(1)

    @pl.when(j == 0)
    def _():
        h_scr[...] = _rms(x_ref[...], g_ref[...]).astype(BF16)
        acc[...] = jnp.zeros_like(acc)

    a = jnp.maximum(_dot(h_scr[...], wu_ref[...]), 0.0)
    acc[...] += _dot((a * a).astype(BF16), wd_ref[...])

    @pl.when(j == pl.num_programs(1) - 1)
    def _():
        y = x_ref[...] + acc[...]
        o_ref[...] = _rms(y, go_ref[...]) if final_norm else y


def sqrelu_mlp_residual(x, g, w_up, w_down, g_out, final_norm, tm, tf):
    T, D = x.shape
    F = w_up.shape[1]
    return pl.pallas_call(
        functools.partial(_mlp_kernel, final_norm=final_norm),
        grid=(T // tm, F // tf),
        in_specs=[pl.BlockSpec((tm, D), lambda i, j: (i, 0)),
                  pl.BlockSpec((1, D), lambda i, j: (0, 0)),
                  pl.BlockSpec((D, tf), lambda i, j: (0, j)),
                  pl.BlockSpec((tf, D), lambda i, j: (j, 0)),
                  pl.BlockSpec((1, D), lambda i, j: (0, 0))],
        out_specs=pl.BlockSpec((tm, D), lambda i, j: (i, 0)),
        out_shape=jax.ShapeDtypeStruct((T, D), F32),
        scratch_shapes=[pltpu.VMEM((tm, D), BF16), pltpu.VMEM((tm, D), F32)],
        compiler_params=_cparams("parallel", "arbitrary"),
        name="sqrelu_mlp",
    )(x, g, w_up, w_down, g_out)


def _mm_res_kernel(a_ref, w_ref, r_ref, o_ref):
    o_ref[...] = r_ref[...] + _dot(a_ref[...].astype(BF16), w_ref[...])


def matmul_residual(a, w, res, tm):
    T, K = a.shape
    D = w.shape[1]
    return pl.pallas_call(
        _mm_res_kernel,
        grid=(T // tm,),
        in_specs=[pl.BlockSpec((tm, K), lambda i: (i, 0)),
                  pl.BlockSpec((K, D), lambda i: (0, 0)),
                  pl.BlockSpec((tm, D), lambda i: (i, 0))],
        out_specs=pl.BlockSpec((tm, D), lambda i: (i, 0)),
        out_shape=jax.ShapeDtypeStruct((T, D), F32),
        compiler_params=_cparams("parallel"),
        name="matmul_residual",
    )(a, w, res)


def _pool_kernel(x_ref, halo_ref, g_ref, w_ref, sc_ref, o_ref, tail_ref, buf, *, start, tm):
    l = pl.program_id(1)

    @pl.when(l == 0)
    def _():
        buf[0:POOL_HALO] = halo_ref[0]

    x = x_ref[0]
    h = _rms(x, g_ref[...])
    buf[POOL_HALO:POOL_HALO + tm] = h
    pos = start + l * tm + lax.broadcasted_iota(jnp.int32, (tm, 1), 0)
    gc = x.shape[1] // len(POOL_WINDOWS)
    parts = []
    for gi, w in enumerate(POOL_WINDOWS):
        lo, hi = gi * gc, (gi + 1) * gc
        tot = buf[POOL_HALO:POOL_HALO + tm, lo:hi]
        for k in range(1, w):
            tot = tot + buf[POOL_HALO - k:POOL_HALO - k + tm, lo:hi]
        inv_cnt = 1.0 / jnp.minimum(pos + 1, w).astype(F32)
        diff = tot * inv_cnt - h[:, lo:hi]
        parts.append(_dot(diff.astype(BF16), w_ref[gi]))
    y = jnp.concatenate(parts, axis=1) * sc_ref[...]
    o_ref[0] = x + y
    t = buf[tm:tm + POOL_HALO]
    tail_ref[0] = t
    buf[0:POOL_HALO] = t


def pool_mixer_residual(x, halo, g, w_grp, scale, start, tm):
    B, L, D = x.shape
    return pl.pallas_call(
        functools.partial(_pool_kernel, start=start, tm=tm),
        grid=(B, L // tm),
        in_specs=[pl.BlockSpec((1, tm, D), lambda b, l: (b, l, 0)),
                  pl.BlockSpec((1, POOL_HALO, D), lambda b, l: (b, 0, 0)),
                  pl.BlockSpec((1, D), lambda b, l: (0, 0)),
                  pl.BlockSpec(w_grp.shape, lambda b, l: (0, 0, 0)),
                  pl.BlockSpec((1, D), lambda b, l: (0, 0))],
        out_specs=[pl.BlockSpec((1, tm, D), lambda b, l: (b, l, 0)),
                   pl.BlockSpec((1, POOL_HALO, D), lambda b, l: (b, 0, 0))],
        out_shape=[jax.ShapeDtypeStruct((B, L, D), F32), jax.ShapeDtypeStruct((B, POOL_HALO, D), F32)],
        scratch_shapes=[pltpu.VMEM((POOL_HALO + tm, D), F32)],
        compiler_params=_cparams("parallel", "arbitrary"),
        name="pool_mixer",
    )(x, halo, g, w_grp, scale)


def _ssd_pre_kernel(xbc_ref, dtr_ref, c0_ref, cw_ref, cb_ref, dtb_ref, xc_ref, dt_ref, tail_ref, ext, *, tm):
    @pl.when(pl.program_id(1) == 0)
    def _():
        ext[0:SUBLANES] = c0_ref[0]

    ext[SUBLANES:SUBLANES + tm] = xbc_ref[0]
    u = cb_ref[...]
    for k in range(SSD_CONV):
        off = SUBLANES - (SSD_CONV - 1) + k
        u = u + ext[off:off + tm] * cw_ref[k:k + 1]
    xc_ref[0] = _silu(u)
    t = ext[tm:tm + SUBLANES]
    tail_ref[0] = t
    ext[0:SUBLANES] = t
    dt_ref[0] = _softplus(dtr_ref[0] + dtb_ref[...])


def ssd_pre(proj, conv0, conv_w, conv_b, dt_bias, tm):
    B, L, _ = proj.shape
    C = conv_w.shape[1]
    NDT = dt_bias.shape[1]
    return pl.pallas_call(
        functools.partial(_ssd_pre_kernel, tm=tm),
        grid=(B, L // tm),
        in_specs=[pl.BlockSpec((1, tm, C), lambda b, l: (b, l, 1)),
                  pl.BlockSpec((1, tm, NDT), lambda b, l: (b, l, 4)),
                  pl.BlockSpec((1, SUBLANES, C), lambda b, l: (b, 0, 0)),
                  pl.BlockSpec((SSD_CONV, C), lambda b, l: (0, 0)),
                  pl.BlockSpec((1, C), lambda b, l: (0, 0)),
                  pl.BlockSpec((1, NDT), lambda b, l: (0, 0))],
        out_specs=[pl.BlockSpec((1, tm, C), lambda b, l: (b, l, 0)),
                   pl.BlockSpec((1, tm, NDT), lambda b, l: (b, l, 0)),
                   pl.BlockSpec((1, SUBLANES, C), lambda b, l: (b, 0, 0))],
        out_shape=[jax.ShapeDtypeStruct((B, L, C), F32), jax.ShapeDtypeStruct((B, L, NDT), F32),
                   jax.ShapeDtypeStruct((B, SUBLANES, C), F32)],
        scratch_shapes=[pltpu.VMEM((SUBLANES + tm, C), F32)],
        compiler_params=_cparams("parallel", "arbitrary"),
        name="ssd_pre",
    )(proj, proj, conv0, conv_w, conv_b, dt_bias)


def _pair_expand(v, q):
    lane = lax.broadcasted_iota(jnp.int32, (q, LANES), 1)
    tiles = []
    for p in range(4):
        lo = jnp.broadcast_to(v[:, 2 * p:2 * p + 1], (q, LANES))
        hi = jnp.broadcast_to(v[:, 2 * p + 1:2 * p + 2], (q, LANES))
        tiles.append(jnp.where(lane < SSD_HEAD_DIM, lo, hi))
    return jnp.concatenate(tiles, axis=1)


def _ssd_scan_kernel(xh_ref, b_ref, c_ref, dt_ref, alog_ref, tri_ref, sin_ref, y_ref, sout_ref, *, seg):
    Q = SSD_CHUNK
    nseg = Q // seg
    hpg = 8

    @pl.when(pl.program_id(2) == 0)
    def _():
        sout_ref[...] = sin_ref[...]

    xh = xh_ref[0]
    bg = b_ref[0].astype(BF16)
    cg = c_ref[0]
    dt = dt_ref[0]
    a = -jnp.exp(alog_ref[0])
    tri = tri_ref[...]
    acum = _dot_exact(tri, dt * a)
    acum_t = acum.T
    dt_t = dt.T
    causal = tri > 0.5
    cb = _dot_nt(cg.astype(BF16), bg)
    lane = lax.broadcasted_iota(jnp.int32, (Q, LANES), 1)
    low = lane < SSD_HEAD_DIM
    ys = []
    for p in range(hpg // 2):
        xpair = xh[:, LANES * p:LANES * (p + 1)]
        acc = None
        for which in range(2):
            h = 2 * p + which
            sg = acum[:, h:h + 1] - acum_t[h:h + 1, :]
            dec = jnp.exp(jnp.where(causal, sg, NEG_INF))
            wts = cb * dec * dt_t[h:h + 1, :]
            xm = jnp.where(low if which == 0 else jnp.logical_not(low), xpair, 0.0)
            term = _dot(wts.astype(BF16), xm.astype(BF16))
            acc = term if acc is None else acc + term
        ys.append(acc)
    y_intra = jnp.concatenate(ys, axis=1)

    if nseg == 1:
        alast = jnp.broadcast_to(acum[Q - 1:Q, :], (Q, LANES))
    else:
        r = lax.broadcasted_iota(jnp.int32, (Q, Q), 0)
        s = lax.broadcasted_iota(jnp.int32, (Q, Q), 1)
        lastsel = (s == r - (r & (seg - 1)) + (seg - 1)).astype(F32)
        alast = _dot_exact(lastsel, acum)
    wcol = jnp.exp(alast - acum) * dt
    xw_t = (xh * _pair_expand(wcol, Q)).T
    col = lax.broadcasted_iota(jnp.int32, (hpg * SSD_HEAD_DIM, Q), 1)
    y_parts = []
    for si in range(nseg):
        r0 = si * seg
        h0 = sout_ref[0, si].reshape(hpg * SSD_HEAD_DIM, SSD_STATE)
        y_parts.append(_dot_nt(cg[r0:r0 + seg], h0))
        xm = xw_t if nseg == 1 else jnp.where((col >= r0) & (col < r0 + seg), xw_t, 0.0)
        s_new = _dot(xm.astype(BF16), bg)
        for h in range(hpg):
            cd = jnp.exp(alast[r0:r0 + 1, h:h + 1])
            rows = slice(SSD_HEAD_DIM * h, SSD_HEAD_DIM * (h + 1))
            sout_ref[0, si, h] = h0[rows] * cd + s_new[rows]
    y_inter = y_parts[0] if nseg == 1 else jnp.concatenate(y_parts, axis=0)
    y_ref[0] = y_intra + y_inter * _pair_expand(jnp.exp(acum), Q)


def ssd_scan(xc, dt, a_log, tri, state0, seg):
    NB, R, _ = xc.shape
    Q = SSD_CHUNK
    nseg = Q // seg
    NC = R // Q
    G = SSD_GROUPS
    HD = 8 * SSD_HEAD_DIM
    return pl.pallas_call(
        functools.partial(_ssd_scan_kernel, seg=seg),
        grid=(NB, G, NC),
        in_specs=[pl.BlockSpec((1, Q, HD), lambda b, g, c: (b, c, g)),
                  pl.BlockSpec((1, Q, SSD_STATE), lambda b, g, c: (b, c, 16 + g)),
                  pl.BlockSpec((1, Q, SSD_STATE), lambda b, g, c: (b, c, 20 + g)),
                  pl.BlockSpec((1, Q, LANES), lambda b, g, c: (b, c, g)),
                  pl.BlockSpec((1, 1, LANES), lambda b, g, c: (g, 0, 0)),
                  pl.BlockSpec((Q, Q), lambda b, g, c: (0, 0)),
                  pl.BlockSpec((1, nseg, 8, SSD_HEAD_DIM, SSD_STATE), lambda b, g, c: (b, 0, g, 0, 0))],
        out_specs=[pl.BlockSpec((1, Q, HD), lambda b, g, c: (b, c, g)),
                   pl.BlockSpec((1, nseg, 8, SSD_HEAD_DIM, SSD_STATE), lambda b, g, c: (b, 0, g, 0, 0))],
        out_shape=[jax.ShapeDtypeStruct((NB, R, G * HD), F32), jax.ShapeDtypeStruct(state0.shape, F32)],
        compiler_params=_cparams("parallel", "parallel", "arbitrary"),
        name="ssd_scan",
    )(xc, xc, xc, dt, a_log, tri, state0)


def _ssd_post_kernel(y_ref, xh_ref, z_ref, d_ref, g_ref, w_ref, r_ref, o_ref):
    y = (y_ref[...] + xh_ref[...] * d_ref[...]) * _silu(z_ref[...])
    gw = y.shape[1] // SSD_GROUPS
    parts = []
    for gi in range(SSD_GROUPS):
        parts.append(_rms(y[:, gi * gw:(gi + 1) * gw], g_ref[:, gi * gw:(gi + 1) * gw]).astype(BF16))
    o_ref[...] = r_ref[...] + _dot(jnp.concatenate(parts, axis=1), w_ref[...])


def ssd_post(y, xc, proj, d_exp, norm_g, w_out, res, tm):
    T, DI = y.shape
    D = w_out.shape[1]
    return pl.pallas_call(
        _ssd_post_kernel,
        grid=(T // tm,),
        in_specs=[pl.BlockSpec((tm, DI), lambda i: (i, 0)),
                  pl.BlockSpec((tm, DI), lambda i: (i, 0)),
                  pl.BlockSpec((tm, DI), lambda i: (i, 0)),
                  pl.BlockSpec((1, DI), lambda i: (0, 0)),
                  pl.BlockSpec((1, DI), lambda i: (0, 0)),
                  pl.BlockSpec((DI, D), lambda i: (0, 0)),
                  pl.BlockSpec((tm, D), lambda i: (i, 0))],
        out_specs=pl.BlockSpec((tm, D), lambda i: (i, 0)),
        out_shape=jax.ShapeDtypeStruct((T, D), F32),
        compiler_params=_cparams("parallel"),
        name="ssd_post",
    )(y, xc, proj, d_exp, norm_g, w_out, res)


def _segment_tri(seg):
    r = np.arange(SSD_CHUNK)
    return jnp.asarray(((r[:, None] // seg == r[None, :] // seg) & (r[None, :] <= r[:, None])).astype(np.float32))


def ssd_layer(x, nb, conv_state, ssm_state, seg, w, tm_mm):
    T, D = x.shape
    L = T // nb
    proj = norm_matmul(x, w["norm"], w["w_in"], tm_mm, 1024)
    conv0 = jnp.pad(conv_state, ((0, 0), (SUBLANES - (SSD_CONV - 1), 0), (0, 0)))
    xc, dt, tail = ssd_pre(proj.reshape(nb, L, -1), conv0, w["conv_w"], w["conv_b"], w["dt_bias"],
                           min(L, SSD_CHUNK))
    rows = SSD_CHUNK if seg < SSD_CHUNK else L
    ngrp = T // rows
    nseg = SSD_CHUNK // seg
    st0 = ssm_state.reshape((ngrp, nseg) + ssm_state.shape[1:])
    y, st = ssd_scan(xc.reshape(ngrp, rows, -1), dt.reshape(ngrp, rows, -1), w["a_log"], _segment_tri(seg), st0, seg)
    x_new = ssd_post(y.reshape(T, -1), xc.reshape(T, -1), proj, w["d_exp"], w["norm_g"], w["w_out"], x, tm_mm)
    return x_new, tail[:, SUBLANES - (SSD_CONV - 1):], st.reshape(ssm_state.shape)


def _bias_of(d, rb_ref, h):
    val = jnp.full(d.shape, rb_ref[0, h], F32)
    for k in range(1, N_BUCKETS):
        val = jnp.where(d >= BUCKET_THR[k], rb_ref[k, h], val)
    return jnp.where(d >= 0, val - rb_ref[N_BUCKETS - 1, h], 0.0)


def _bias_tables_kernel(rb_ref, tp_ref, ts_ref, *, past_len):
    r = lax.broadcasted_iota(jnp.int32, (Q_TILE, Q_TILE), 0)
    c = lax.broadcasted_iota(jnp.int32, (Q_TILE, Q_TILE), 1)
    t = lax.broadcasted_iota(jnp.int32, (SUBLANES, 1024), 0)
    j = lax.broadcasted_iota(jnp.int32, (SUBLANES, 1024), 1)
    win0 = past_len - WINDOW
    d_s = jnp.where(j < 256, past_len + t - (CMP_STRIDE * j + CMP_BLOCK - 1),
                    jnp.where(j < 384, past_len + t - (past_len - PAGE_SIZE + (j - 256)),
                              jnp.where(j < 512, t - (j - 384), past_len + t - (win0 + (j - 512)))))

    def body(h, carry):
        tp_ref[h, 0] = _bias_of(r - c, rb_ref, h)
        tp_ref[h, 1] = _bias_of(Q_TILE + r - c, rb_ref, h)
        tp_ref[h, 2] = _bias_of(r - CMP_STRIDE * (c - CMP_FRONT) - (CMP_BLOCK - 1), rb_ref, h)
        ts_ref[h] = _bias_of(d_s, rb_ref, h)
        return carry

    lax.fori_loop(0, ATT_HEADS, body, 0)


def bias_tables(rel_bias, past_len):
    return pl.pallas_call(
        functools.partial(_bias_tables_kernel, past_len=past_len),
        in_specs=[pl.BlockSpec(memory_space=pltpu.SMEM)],
        out_specs=[pl.BlockSpec(memory_space=pltpu.VMEM), pl.BlockSpec(memory_space=pltpu.VMEM)],
        out_shape=[jax.ShapeDtypeStruct((ATT_HEADS, 3, Q_TILE, Q_TILE), F32),
                   jax.ShapeDtypeStruct((ATT_HEADS, SUBLANES, 1024), F32)],
        name="bias_tables",
    )(rel_bias)


def _cmp_pre_kernel(*refs, nsrc, nsub, n_prefetch):
    refs = refs[n_prefetch:]
    srcs = refs[:nsrc]
    w_ref, o_ref, stage = refs[nsrc], refs[nsrc + 1], refs[nsrc + 2]
    rows = nsub * CMP_STRIDE
    n = nsrc * nsub
    for f in range(2):
        xs = []
        for pair in range(2):
            l0 = f * 256 + pair * LANES
            for si, src in enumerate(srcs):
                stage[si * rows:(si + 1) * rows] = src[0, :, l0:l0 + LANES]
            cols = [stage[pl.ds(s, n, stride=CMP_STRIDE)] for s in range(CMP_STRIDE)]
            xs.append(jnp.concatenate(cols, axis=1))
        x2 = jnp.concatenate(xs, axis=0).astype(BF16)
        pre = _dot(x2, w_ref[f])
        for pair in range(2):
            o_ref[0, f, 2 * pair] = pre[pair * n:(pair + 1) * n, 0:256]
            o_ref[0, f, 2 * pair + 1] = pre[pair * n:(pair + 1) * n, 256:512]


def _cmp_fin_kernel(pre_ref, pe_ref, w1_ref, w2_ref, o_ref, scr, *, n):
    for f in range(2):
        cvec = _dot(pe_ref[f].astype(BF16), w1_ref[f])[0:1]
        for kv in range(ATT_KV):
            pre = pre_ref[0, f, kv]
            slot = f * ATT_KV + kv
            scr[slot, 0:n] = pre[:, CMP_HIDDEN:2 * CMP_HIDDEN]
            scr[slot, n:n + SUBLANES] = jnp.zeros((SUBLANES, CMP_HIDDEN), F32)
            hid = pre[:, 0:CMP_HIDDEN] + scr[slot, pl.ds(1, n)] + cvec
            o_ref[0, f, kv] = _dot(_silu(hid).astype(BF16), w2_ref[f])


def cmp_finish(pre, pe8, w1, w2d):
    B, _, KV, n, _ = pre.shape
    return pl.pallas_call(
        functools.partial(_cmp_fin_kernel, n=n),
        grid=(B,),
        in_specs=[pl.BlockSpec((1, 2, KV, n, 256), lambda b: (b, 0, 0, 0, 0)),
                  pl.BlockSpec(pe8.shape, lambda b: (0, 0, 0)),
                  pl.BlockSpec(w1.shape, lambda b: (0, 0, 0)),
                  pl.BlockSpec(w2d.shape, lambda b: (0, 0, 0))],
        out_specs=pl.BlockSpec((1, 2, KV, n, LANES), lambda b: (b, 0, 0, 0, 0)),
        out_shape=jax.ShapeDtypeStruct((B, 2, KV, n, LANES), F32),
        scratch_shapes=[pltpu.VMEM((2 * KV, n + SUBLANES, CMP_HIDDEN), F32)],
        compiler_params=_cparams("parallel"),
        name="cmp_finish",
    )(pre, pe8, w1, w2d)


def cmp_pre_prompt(proj, w_pair, rows):
    B, L, _ = proj.shape
    nsub = rows // CMP_STRIDE
    return pl.pallas_call(
        functools.partial(_cmp_pre_kernel, nsrc=1, nsub=nsub, n_prefetch=0),
        grid=(B, L // rows),
        in_specs=[pl.BlockSpec((1, rows, 512), lambda b, t: (b, t, 2)),
                  pl.BlockSpec(w_pair.shape, lambda b, t: (0, 0, 0))],
        out_specs=pl.BlockSpec((1, 2, ATT_KV, nsub, 256), lambda b, t: (b, 0, 0, t, 0)),
        out_shape=jax.ShapeDtypeStruct((B, 2, ATT_KV, L // CMP_STRIDE, 256), F32),
        scratch_shapes=[pltpu.VMEM((rows, LANES), F32)],
        compiler_params=_cparams("parallel", "parallel"),
        name="cmp_pre_prompt",
    )(proj, w_pair)


def cmp_pre_sample(page_table, cache, new_page, w_pair):
    B, n_pages = page_table.shape
    nsub = PAGE_SIZE // CMP_STRIDE
    nsrc = n_pages + 1

    def page_spec(p):
        return pl.BlockSpec((1, PAGE_SIZE, 512), lambda b, pt: (pt[b, p], 0, 0))

    grid_spec = pltpu.PrefetchScalarGridSpec(
        num_scalar_prefetch=1,
        grid=(B,),
        in_specs=[page_spec(p) for p in range(n_pages)]
        + [pl.BlockSpec((1, PAGE_SIZE, 512), lambda b, pt: (b, 0, 0)),
           pl.BlockSpec(w_pair.shape, lambda b, pt: (0, 0, 0))],
        out_specs=pl.BlockSpec((1, 2, ATT_KV, nsrc * nsub, 256), lambda b, pt: (b, 0, 0, 0, 0)),
        scratch_shapes=[pltpu.VMEM((nsrc * PAGE_SIZE, LANES), F32)],
    )
    return pl.pallas_call(
        functools.partial(_cmp_pre_kernel, nsrc=nsrc, nsub=nsub, n_prefetch=1),
        grid_spec=grid_spec,
        out_shape=jax.ShapeDtypeStruct((B, 2, ATT_KV, nsrc * nsub, 256), F32),
        compiler_params=_cparams("parallel"),
        name="cmp_pre_sample",
    )(page_table, *([cache] * n_pages), new_page, w_pair)


def _importance_matrix(n_rows, front):
    m = np.arange(n_rows)[:, None] - front
    j = np.arange(LANES)[None, :]
    ratio = SEL_BLOCK // CMP_STRIDE
    a = ((m >= ratio * j) & (m <= ratio * j + ratio - 1)).astype(np.float32) \
        + ((m >= ratio * j - 1) & (m <= ratio * j + ratio - 2)).astype(np.float32)
    return jnp.asarray(a)


def _softmax_rows(s):
    m = jnp.max(s, axis=-1, keepdims=True)
    m = jnp.where(m > NEG_INF, m, 0.0)
    e = jnp.exp(s - m)
    return e / jnp.maximum(jnp.sum(e, axis=-1, keepdims=True), 1e-30)


def _top_blocks(score, n_sel, axis=1):
    jb = lax.broadcasted_iota(jnp.int32, score.shape, axis).astype(F32)
    sel = jnp.zeros(score.shape, F32)
    sc = score
    for _ in range(n_sel):
        mx = jnp.max(sc, axis=axis, keepdims=True)
        idx = jnp.min(jnp.where(sc == mx, jb, 1e9), axis=axis, keepdims=True)
        pick = jb == idx
        sel = jnp.where(pick, 1.0, sel)
        sc = jnp.where(pick, NEG_INF, sc)
    return sel


def _tile4(x):
    return jnp.concatenate([x, x, x, x], axis=0)


def _attn_prompt_kernel(q_ref, gate_ref, kc_ref, vc_ref, ks_ref, vs_ref, kw_ref, vw_ref, tp_ref, amat_ref, o_ref,
                        mx_scr, acc_scr, *, nsub):
    i = pl.program_id(2)
    QT = Q_TILE
    R2 = 2 * QT
    low = lax.broadcasted_iota(jnp.int32, (QT, LANES), 1) < ATT_HEAD_DIM
    qpp = jnp.concatenate([q_ref[0, :, 0:LANES], q_ref[0, :, LANES:2 * LANES]], axis=0).astype(BF16)
    r1 = lax.broadcasted_iota(jnp.int32, (QT, 1), 0)
    c1 = lax.broadcasted_iota(jnp.int32, (1, LANES), 1)
    lower = r1 >= c1
    upper = r1 <= c1
    i_vec = jnp.full((QT, LANES), i, jnp.int32)

    def half_masks(n):
        lo = lax.broadcasted_iota(jnp.int32, (n, LANES), 1) < ATT_HEAD_DIM
        return jnp.where(lo, 1.0, 0.0).astype(BF16), jnp.where(lo, 0.0, 1.0).astype(BF16)

    def bd(x2):
        m_lo, m_hi = half_masks(x2.shape[0])
        return jnp.concatenate([x2 * m_lo, x2 * m_hi], axis=0)

    def spread(madd):
        m2 = jnp.concatenate([madd, madd], axis=1)
        return jnp.concatenate([m2, m2], axis=0)

    def bias_full(kind):
        return jnp.concatenate([jnp.concatenate([tp_ref[2 * p, kind], tp_ref[2 * p + 1, kind]], axis=1)
                                for p in range(2)], axis=0)

    def row_max2(x):
        parts = []
        for w in range(2):
            m = jnp.max(x[:, LANES * w:LANES * (w + 1)], axis=-1, keepdims=True)
            parts.append(jnp.broadcast_to(jnp.where(m > NEG_INF, m, 0.0), (R2, LANES)))
        return jnp.concatenate(parts, axis=1)

    def widen(m2, reps):
        if reps == 1:
            return m2
        return jnp.concatenate([m2[:, :LANES]] * reps + [m2[:, LANES:]] * reps, axis=1)

    def kv_rows(ref, start, n):
        return ref[0, 0, 0, pl.ds(pl.multiple_of(start, n), n), :]

    def key_cols(ref, t0, ntiles):
        tiles = [ref[0, 0, 0, t0 + c] for c in range(ntiles)]
        return tiles[0] if ntiles == 1 else jnp.concatenate(tiles, axis=1)

    def scores(kt2, add):
        n = kt2.shape[1]
        top = lax.broadcasted_iota(jnp.int32, (LANES, n), 0) < ATT_HEAD_DIM
        kbd_t = jnp.concatenate([kt2 * jnp.where(top, 1.0, 0.0).astype(BF16),
                                 kt2 * jnp.where(top, 0.0, 1.0).astype(BF16)], axis=1)
        return _dot(qpp, kbd_t) + add

    def scores_nt(k2, add):
        return _dot_nt(qpp, bd(k2)) + add

    def weighted(s, m_wide, v2):
        ones = jnp.concatenate(half_masks(v2.shape[0]), axis=0)
        rhs = jnp.concatenate([bd(v2), ones], axis=1)
        return _dot(jnp.exp(s - m_wide).astype(BF16), rhs)

    def normalized(acc):
        return acc[:, :LANES] / jnp.maximum(acc[:, LANES:], 1e-30)

    nback = WINDOW // QT
    ws, wt = [], []
    for back in range(nback, -1, -1):
        tc = jnp.maximum(i - back, 0)
        ok = i_vec >= back
        if back == nback:
            ok = ok & upper
        if back == 0:
            ok = lower
        madd = spread(jnp.where(ok, 0.0, NEG_INF))
        ws.append(scores(key_cols(kw_ref, tc, 1), madd + bias_full(back) if back <= 1 else madd))
        wt.append(tc)
    mel = ws[0]
    for s in ws[1:]:
        mel = jnp.maximum(mel, s)
    m2_w = row_max2(mel)
    acc_w = None
    for s, tc in zip(ws, wt):
        term = weighted(s, m2_w, kv_rows(vw_ref, tc * QT, QT))
        acc_w = term if acc_w is None else acc_w + term
    o_w = normalized(acc_w)

    st = pl.multiple_of(i * SUBLANES, SUBLANES)
    d_cmp = r1 - CMP_STRIDE * (c1 - CMP_FRONT) - (CMP_BLOCK - 1)
    ctiles = []
    for j in range(nsub // QT):
        rows = slice(CMP_FRONT + j * QT, CMP_FRONT + (j + 1) * QT)
        ok = jnp.broadcast_to(j * QT + c1 < i * SUBLANES - CMP_FRONT, (QT, LANES))
        ctiles.append((kc_ref[0, 0, 0, rows, :], vc_ref[0, 0, 0, rows, :], amat_ref[rows, :], ok, False))
    ctiles.append((kc_ref[0, 0, 0, pl.ds(st, QT), :], vc_ref[0, 0, 0, pl.ds(st, QT), :], amat_ref[pl.ds(st, QT), :],
                   (d_cmp >= 0) & (c1 >= CMP_FRONT - i * SUBLANES), True))
    ss = []
    for kt, _, _, ok, diag in ctiles:
        add = spread(jnp.where(ok, 0.0, NEG_INF))
        ss.append(scores_nt(kt.astype(BF16), add + bias_full(2) if diag else add))
    mel = ss[0]
    for s in ss[1:]:
        mel = jnp.maximum(mel, s)
    m2 = row_max2(mel)
    es = [jnp.exp(s - m2) for s in ss]
    lel = es[0]
    for e in es[1:]:
        lel = lel + e
    inv = []
    for w in range(2):
        l = jnp.sum(lel[:, LANES * w:LANES * (w + 1)], axis=-1, keepdims=True)
        inv.append(jnp.broadcast_to(1.0 / jnp.maximum(l, 1e-30), (R2, LANES)))
    inv2 = jnp.concatenate(inv, axis=1)
    o_c = None
    phs = []
    for e, (_, vt, _, _, _) in zip(es, ctiles):
        pn = e * inv2
        term = _dot(pn.astype(BF16), bd(vt.astype(BF16)))
        o_c = term if o_c is None else o_c + term
        phs.append(pn[:QT, :LANES] + pn[:QT, LANES:] + pn[QT:, :LANES] + pn[QT:, LANES:])
    imp = _dot_split3(jnp.concatenate(phs, axis=1),
                      jnp.concatenate([t[2] for t in ctiles], axis=0).astype(BF16))

    qblk = 2 * i + (c1 >= SEL_BLOCK).astype(jnp.int32)
    lag = qblk - r1
    allowed = lag >= 0
    forced = (r1 == 0) | (allowed & (lag < N_LOCAL))
    score_t = jnp.where(allowed, imp.T + jnp.where(forced, FORCE_BONUS, 0.0), -1.0)
    sel = _top_blocks(score_t, N_SEL, axis=0).T.astype(BF16)

    FK = FAR_KEYS
    tm1 = jnp.maximum(i - 1, 0)
    limit = tm1 * QT
    n_grp = jnp.right_shift(tm1 + FK // QT - 1, (FK // QT).bit_length() - 1)
    shift = SEL_BLOCK.bit_length() - 1

    def key_mask(key0, n):
        jb = lax.broadcasted_iota(jnp.int32, (LANES, n), 0)
        key = key0 + lax.broadcasted_iota(jnp.int32, (LANES, n), 1)
        hit = jb == jnp.right_shift(key, shift)
        if n == FK:
            hit = hit & (key < limit)
        return jnp.where(_dot(sel, jnp.where(hit, 1.0, 0.0).astype(BF16)) > 0.5, 0.0, NEG_INF)

    low2 = lax.broadcasted_iota(jnp.int32, (R2, LANES), 1) < ATT_HEAD_DIM

    def online_step(s, v2):
        n = v2.shape[0]
        reps = n // LANES
        halves = []
        for w in range(2):
            m = s[:, w * n:w * n + LANES]
            for c4 in range(1, reps):
                m = jnp.maximum(m, s[:, w * n + c4 * LANES:w * n + (c4 + 1) * LANES])
            halves.append(jnp.broadcast_to(jnp.max(m, axis=-1, keepdims=True), (R2, LANES)))
        m_old = mx_scr[...]
        m_new = jnp.maximum(m_old, jnp.concatenate(halves, axis=1))
        m_safe = jnp.where(m_new > NEG_INF, m_new, 0.0)
        alpha = jnp.exp(m_old - m_safe)
        a_mix = jnp.where(low2, alpha[:, :LANES], alpha[:, LANES:])
        mx_scr[...] = m_new
        acc_scr[...] = acc_scr[...] * jnp.concatenate([a_mix, a_mix], axis=1) + weighted(s, widen(m_safe, reps), v2)

    mx_scr[...] = jnp.full(mx_scr.shape, NEG_INF, F32)
    acc_scr[...] = jnp.zeros_like(acc_scr)

    def far_body(gi, c):
        s = scores(key_cols(ks_ref, gi * (FK // QT), FK // QT), spread(key_mask(gi * FK, FK)))
        online_step(s, kv_rows(vs_ref, gi * FK, FK))
        return c

    lax.fori_loop(0, n_grp, far_body, 0)
    mk_s = spread(jnp.where(i_vec >= 1, key_mask(tm1 * QT, QT), NEG_INF))
    online_step(scores(key_cols(ks_ref, tm1, 1), mk_s + bias_full(1)), kv_rows(vs_ref, tm1 * QT, QT))
    mk_d = spread(jnp.where(lower, key_mask(i * QT, QT), NEG_INF))
    online_step(scores(key_cols(ks_ref, i, 1), mk_d + bias_full(0)), kv_rows(vs_ref, i * QT, QT))
    o_s = normalized(acc_scr[...])

    g = _sigmoid(gate_ref[0])

    def gate(br):
        tiles = []
        for p in range(2):
            c0 = br * ATT_HG + 2 * p
            tiles.append(jnp.where(low, jnp.broadcast_to(g[:, c0:c0 + 1], (QT, LANES)),
                                   jnp.broadcast_to(g[:, c0 + 1:c0 + 2], (QT, LANES))))
        return jnp.concatenate(tiles, axis=0)

    o = gate(0) * o_c + gate(1) * o_s + gate(2) * o_w
    o_ref[0] = jnp.concatenate([o[:QT], o[QT:]], axis=1)


def attn_prompt(proj, cmp_kv, kt, vv, tp, amat):
    B, L, _ = proj.shape
    nsub = L // CMP_STRIDE
    ncp = cmp_kv.shape[3]

    def k_spec(f):
        return pl.BlockSpec((1, 1, 1, L // Q_TILE, LANES, Q_TILE), lambda b, k, i: (b, f, k, 0, 0, 0))

    def v_spec(f):
        return pl.BlockSpec((1, 1, 1, L, LANES), lambda b, k, i: (b, f, k, 0, 0))

    def cmp_spec(f):
        return pl.BlockSpec((1, 1, 1, ncp, LANES), lambda b, k, i: (b, f, k, 0, 0))

    return pl.pallas_call(
        functools.partial(_attn_prompt_kernel, nsub=nsub),
        grid=(B, ATT_KV, L // Q_TILE),
        in_specs=[pl.BlockSpec((1, Q_TILE, 256), lambda b, k, i: (b, i, k)),
                  pl.BlockSpec((1, Q_TILE, LANES), lambda b, k, i: (b, i, 20 + k)),
                  cmp_spec(0), cmp_spec(1), k_spec(0), v_spec(0), k_spec(1), v_spec(1),
                  pl.BlockSpec((ATT_HG, 3, Q_TILE, Q_TILE), lambda b, k, i: (k, 0, 0, 0)),
                  pl.BlockSpec(amat.shape, lambda b, k, i: (0, 0))],
        out_specs=pl.BlockSpec((1, Q_TILE, 256), lambda b, k, i: (b, i, k)),
        out_shape=jax.ShapeDtypeStruct((B, L, ATT_HEADS * ATT_HEAD_DIM), F32),
        scratch_shapes=[pltpu.VMEM((2 * Q_TILE, 2 * LANES), F32),
                        pltpu.VMEM((2 * Q_TILE, 2 * LANES), F32)],
        compiler_params=_cparams("parallel", "parallel", "arbitrary"),
        name="attn_prompt",
    )(proj, proj, cmp_kv, cmp_kv, kt, vv, kt, vv, tp, amat)


def _attn_sample_kernel(*refs, n_pages, past_len):
    pages = refs[1:1 + n_pages]
    q_ref, gate_ref, new_s_ref, new_w_ref, win_ref, cmp_ref, ts_ref, amat_ref, o_ref = refs[1 + n_pages:]
    T = SUBLANES
    R = ATT_HG * T
    n_cmp_rows = cmp_ref.shape[3]
    lane_t = lax.broadcasted_iota(jnp.int32, (T, LANES), 1)
    low_t = lane_t < ATT_HEAD_DIM
    t_r = lax.broadcasted_iota(jnp.int32, (R, 1), 0) & (T - 1)
    c1 = lax.broadcasted_iota(jnp.int32, (1, LANES), 1)
    g = _sigmoid(gate_ref[0])
    zeros_new = jnp.zeros((LANES - T, LANES), F32)
    new_ok = (c1 <= t_r) & (c1 < T)
    n_c = lax.broadcasted_iota(jnp.int32, (1, 256), 1)
    d_c = past_len + t_r - (CMP_STRIDE * n_c + CMP_BLOCK - 1)
    c_w = lax.broadcasted_iota(jnp.int32, (1, WINDOW), 1)
    cmp_pad = jnp.zeros((256 - n_cmp_rows, LANES), F32)

    def lane_tile(kv):
        return LANES * (kv // 2)

    def queries(kv):
        par = kv % 2
        keep = (lane_t >= ATT_HEAD_DIM) if par else low_t
        qrows = []
        for hg in range(ATT_HG):
            h = kv * ATT_HG + hg
            q2 = q_ref[0, :, LANES * (h // 2):LANES * (h // 2 + 1)]
            if h % 2 != par:
                q2 = pltpu.roll(q2, ATT_HEAD_DIM, axis=1)
            qrows.append(jnp.where(keep, q2, 0.0))
        return jnp.concatenate(qrows, axis=0).astype(BF16)

    def new_rows(ref, l0):
        return jnp.concatenate([ref[0, :, l0:l0 + LANES], zeros_new], axis=0).astype(BF16)

    qs_all = [queries(kv) for kv in range(ATT_KV)]
    tabs = [ts_ref[kv * ATT_HG:(kv + 1) * ATT_HG].reshape(R, 1024) for kv in range(ATT_KV)]

    o_cs, p_sums = [], []
    for kv in range(ATT_KV):
        kc = jnp.concatenate([cmp_ref[0, 0, kv], cmp_pad], axis=0).astype(BF16)
        vc = jnp.concatenate([cmp_ref[0, 1, kv], cmp_pad], axis=0).astype(BF16)
        s_c = _dot_nt(qs_all[kv], kc) + tabs[kv][:, 0:256]
        p_c = _softmax_rows(jnp.where(d_c >= 0, s_c, NEG_INF))
        o_cs.append(_dot(p_c.astype(BF16), vc))
        p_sums.append(p_c[0:T] + p_c[T:2 * T] + p_c[2 * T:3 * T] + p_c[3 * T:4 * T])

    p_all = jnp.concatenate(p_sums + [jnp.zeros((LANES - ATT_KV * T, 256), F32)], axis=0)
    imp_t = _dot_split3(p_all, amat_ref[...].astype(BF16)).T
    rb = lax.broadcasted_iota(jnp.int32, (LANES, 1), 0)
    qblk = jnp.right_shift(past_len + (c1 & (T - 1)), SEL_BLOCK.bit_length() - 1)
    lag = qblk - rb
    allowed = lag >= 0
    forced = (rb == 0) | (allowed & (lag < N_LOCAL))
    score_t = jnp.where(allowed, imp_t + jnp.where(forced, FORCE_BONUS, 0.0), -1.0)
    sel_all = _top_blocks(score_t, N_SEL, axis=0).T

    out_tiles = []
    for kv in range(ATT_KV):
        par = kv % 2
        pl0 = lane_tile(kv)
        qs, tab = qs_all[kv], tabs[kv]
        sel = sel_all[kv * T:(kv + 1) * T]
        o_c = o_cs[kv]

        nb = 2 * n_pages
        k_all = jnp.concatenate([pages[p][0, :, pl0:pl0 + LANES].astype(BF16) for p in range(n_pages)]
                                + [new_rows(new_s_ref, pl0)], axis=0)
        v_all = jnp.concatenate([pages[p][0, :, 256 + pl0:256 + pl0 + LANES].astype(BF16) for p in range(n_pages)]
                                + [new_rows(new_s_ref, 256 + pl0)], axis=0)
        m_parts = [jnp.where(c1 < SEL_BLOCK, sel[:, 2 * p:2 * p + 1], sel[:, 2 * p + 1:2 * p + 2])
                   for p in range(n_pages)]
        m_parts.append(jnp.where(new_ok[0:T], jnp.broadcast_to(sel[:, nb:nb + 1], (T, LANES)), 0.0))
        zero = jnp.zeros((R, LANES), F32)
        bias_s = jnp.concatenate([zero] * (n_pages - 1) + [tab[:, 256:384], tab[:, 384:512]], axis=1)
        s_all = jnp.where(_tile4(jnp.concatenate(m_parts, axis=1)) > 0.5, _dot_nt(qs, k_all) + bias_s, NEG_INF)
        o_s = _dot(_softmax_rows(s_all).astype(BF16), v_all)

        kw = jnp.concatenate([win_ref[0, :, pl0:pl0 + LANES].astype(BF16), new_rows(new_w_ref, pl0)], axis=0)
        vw = jnp.concatenate([win_ref[0, :, 256 + pl0:256 + pl0 + LANES].astype(BF16),
                              new_rows(new_w_ref, 256 + pl0)], axis=0)
        bias_w = jnp.concatenate([tab[:, 512:1024] + jnp.where(c_w >= t_r, 0.0, NEG_INF),
                                  tab[:, 384:512] + jnp.where(new_ok, 0.0, NEG_INF)], axis=1)
        s_w = _dot_nt(qs, kw) + bias_w
        o_w = _dot(_softmax_rows(s_w).astype(BF16), vw)

        gk = g[:, LANES * kv:LANES * (kv + 1)]
        for pair in range(2):
            halves = []
            for which in range(2):
                hg = 2 * pair + which
                rows = slice(hg * T, (hg + 1) * T)
                o = gk[:, hg:hg + 1] * o_c[rows] + gk[:, 4 + hg:5 + hg] * o_s[rows] + gk[:, 8 + hg:9 + hg] * o_w[rows]
                if which != par:
                    o = pltpu.roll(o, ATT_HEAD_DIM, axis=1)
                halves.append(o)
            out_tiles.append(jnp.where(low_t, halves[0], halves[1]))
    o_ref[0] = jnp.concatenate(out_tiles, axis=1)


def attn_sample(page_table, cache, proj, win, cmp_kv, ts, amat, past_len):
    B, n_pages = page_table.shape
    T = proj.shape[1]

    def page_spec(p):
        return pl.BlockSpec((1, PAGE_SIZE, 512), lambda b, pt: (pt[b, p], 0, 1))

    grid_spec = pltpu.PrefetchScalarGridSpec(
        num_scalar_prefetch=1,
        grid=(B,),
        in_specs=[page_spec(p) for p in range(n_pages)]
        + [pl.BlockSpec((1, T, 1024), lambda b, pt: (b, 0, 0)),
           pl.BlockSpec((1, T, 512), lambda b, pt: (b, 0, 5)),
           pl.BlockSpec((1, T, 512), lambda b, pt: (b, 0, 3)),
           pl.BlockSpec((1, T, 512), lambda b, pt: (b, 0, 4)),
           pl.BlockSpec((1,) + win.shape[1:], lambda b, pt: (b, 0, 0)),
           pl.BlockSpec((1,) + cmp_kv.shape[1:], lambda b, pt: (b, 0, 0, 0, 0)),
           pl.BlockSpec(ts.shape, lambda b, pt: (0, 0, 0)),
           pl.BlockSpec(amat.shape, lambda b, pt: (0, 0))],
        out_specs=pl.BlockSpec((1, T, 1024), lambda b, pt: (b, 0, 0)),
    )
    return pl.pallas_call(
        functools.partial(_attn_sample_kernel, n_pages=n_pages, past_len=past_len),
        grid_spec=grid_spec,
        out_shape=jax.ShapeDtypeStruct((B, T, ATT_HEADS * ATT_HEAD_DIM), F32),
        compiler_params=_cparams("parallel"),
        name="attn_sample",
    )(page_table, *([cache] * n_pages), proj, proj, proj, proj, win, cmp_kv, ts, amat)


def _ssd_weights(i, li, norm_mix, ssd_w_in, ssd_conv_w, ssd_conv_b, ssd_dt_bias, ssd_a_log, ssd_d, ssd_norm, ssd_w_out):
    d_inner = ssd_w_out.shape[1]
    conv_dim = ssd_conv_w.shape[2]
    heads = ssd_dt_bias.shape[1]
    hpg = heads // SSD_GROUPS
    w = ssd_w_in[li]

    def per_group(v):
        v = v.reshape(v.shape[:-1] + (SSD_GROUPS, hpg))
        return jnp.pad(v, [(0, 0)] * (v.ndim - 1) + [(0, LANES - hpg)]).reshape(v.shape[:-2] + (SSD_GROUPS * LANES,))

    w_dt = per_group(w[:, d_inner + conv_dim:])
    w_in = jnp.concatenate([w[:, :d_inner], w_dt, jnp.zeros_like(w_dt), w[:, d_inner:d_inner + conv_dim]], axis=1)
    return dict(
        norm=norm_mix[i][None],
        w_in=w_in.astype(BF16),
        conv_w=ssd_conv_w[li], conv_b=ssd_conv_b[li][None],
        dt_bias=per_group(ssd_dt_bias[li])[None],
        a_log=per_group(ssd_a_log[li]).reshape(SSD_GROUPS, 1, LANES),
        d_exp=jnp.repeat(ssd_d[li], SSD_HEAD_DIM)[None],
        norm_g=ssd_norm[li][None],
        w_out=ssd_w_out[li].astype(BF16),
    )


def _nsa_in_weight(w):
    q_dim = ATT_HEADS * ATT_HEAD_DIM
    kv_dim = 6 * ATT_KV * ATT_HEAD_DIM
    idx = np.zeros((ATT_KV, LANES), np.int32)
    ok = np.zeros((ATT_KV, LANES), bool)
    for kv in range(ATT_KV):
        for br in range(3):
            for hg in range(ATT_HG):
                idx[kv, br * ATT_HG + hg] = q_dim + kv_dim + (kv * ATT_HG + hg) * 3 + br
                ok[kv, br * ATT_HG + hg] = True
    w_g = jnp.where(jnp.asarray(ok.reshape(-1))[None, :], w[:, idx.reshape(-1)], 0.0)
    return jnp.concatenate([w[:, :q_dim] * (ATT_HEAD_DIM ** -0.5), w[:, q_dim:q_dim + kv_dim], w_g], axis=1).astype(BF16)


def _prompt_kv_layouts(kv6):
    B, L = kv6.shape[:2]
    k = jnp.transpose(kv6[:, :, 2::2].astype(BF16), (0, 2, 3, 4, 1))
    k = jnp.concatenate([k, k], axis=3).reshape(B, 2, ATT_KV, LANES, L // Q_TILE, Q_TILE)
    v = jnp.transpose(kv6[:, :, 3::2].astype(BF16), (0, 2, 3, 1, 4))
    return jnp.transpose(k, (0, 1, 2, 4, 3, 5)), jnp.concatenate([v, v], axis=-1)


def _cmp_pair_weight(w1):
    w1r = w1.reshape(2, 2, CMP_STRIDE, ATT_HEAD_DIM, CMP_HIDDEN)
    eye = jnp.eye(2, dtype=w1.dtype)
    wp = jnp.einsum("fjsde,wv->fswdvje", w1r, eye)
    return wp.reshape(2, CMP_STRIDE * 2 * ATT_HEAD_DIM, 2 * 2 * CMP_HIDDEN).astype(BF16)


def kernel(x_prompt, x_sample, cache_nsa_kv, state_nsa_win, state_ssm, state_conv, state_pool, page_table, rel_bias,
           norm_mix, norm_ffn, norm_out, ffn_w_up, ffn_w_down, ssd_w_in, ssd_conv_w, ssd_conv_b, ssd_dt_bias,
           ssd_a_log, ssd_d, ssd_norm, ssd_w_out, pool_w, pool_scale, nsa_w_in, nsa_cmp_pe, nsa_cmp_w1, nsa_cmp_w2,
           nsa_w_out):
    bp, lp, d_model = x_prompt.shape
    bs, ls, _ = x_sample.shape
    depth = norm_mix.shape[0]
    n_pages = page_table.shape[1]
    past_len = n_pages * PAGE_SIZE
    assert ls == SUBLANES and lp % (16 * Q_TILE) == 0 and past_len >= WINDOW and state_nsa_win.shape[2] == WINDOW
    xp = x_prompt.reshape(bp * lp, d_model)
    xs = x_sample.reshape(bs * ls, d_model)
    tm_p, tm_s = 512, 512
    outs = {k: [] for k in ("kv_p", "kv_s", "win_p", "win_s", "ssm_p", "ssm_s", "conv_p", "conv_s", "pool_p", "pool_s")}
    for i in range(depth):
        kind, li = i % 3, i // 3
        if kind == 0:
            w = _ssd_weights(i, li, norm_mix, ssd_w_in, ssd_conv_w, ssd_conv_b, ssd_dt_bias, ssd_a_log, ssd_d,
                             ssd_norm, ssd_w_out)
            conv_dim = ssd_conv_w.shape[2]
            xp, c_p, s_p = ssd_layer(xp, bp, jnp.zeros((bp, SSD_CONV - 1, conv_dim), F32),
                                     jnp.zeros((bp,) + state_ssm.shape[2:], F32), SSD_CHUNK, w, tm_p)
            xs, c_s, s_s = ssd_layer(xs, bs, state_conv[li], state_ssm[li], ls, w, tm_s)
            outs["conv_p"].append(c_p)
            outs["conv_s"].append(c_s)
            outs["ssm_p"].append(s_p)
            outs["ssm_s"].append(s_s)
        elif kind == 1:
            g = norm_mix[i][None]
            pw = pool_w[li].astype(BF16)
            sc = pool_scale[li][None]
            xp3, tail_p = pool_mixer_residual(xp.reshape(bp, lp, d_model), jnp.zeros((bp, POOL_HALO, d_model), F32),
                                              g, pw, sc, 0, 512)
            halo_s = jnp.pad(state_pool[li], ((0, 0), (1, 0), (0, 0)))
            xs3, tail_s = pool_mixer_residual(xs.reshape(bs, ls, d_model), halo_s, g, pw, sc, past_len, ls)
            xp, xs = xp3.reshape(bp * lp, d_model), xs3.reshape(bs * ls, d_model)
            outs["pool_p"].append(tail_p[:, 1:])
            outs["pool_s"].append(tail_s[:, 1:])
        else:
            g = norm_mix[i][None]
            w_in = _nsa_in_weight(nsa_w_in[li])
            w_pair = _cmp_pair_weight(nsa_cmp_w1[li])
            pe8 = jnp.broadcast_to(nsa_cmp_pe[li].reshape(2, 1, -1), (2, SUBLANES, CMP_BLOCK * ATT_HEAD_DIM))
            w1 = nsa_cmp_w1[li].astype(BF16)
            w2d = jnp.concatenate([nsa_cmp_w2[li], nsa_cmp_w2[li]], axis=-1).astype(BF16)
            tp, ts = bias_tables(rel_bias, past_len)
            w_out = nsa_w_out[li].astype(BF16)
            kvw = ATT_KV * ATT_HEAD_DIM
            proj_p = norm_matmul(xp, g, w_in, tm_p, 1024).reshape(bp, lp, -1)
            pre_p = cmp_pre_prompt(proj_p, w_pair, 2048)
            cmp_p = jnp.pad(cmp_finish(pre_p, pe8, w1, w2d), ((0, 0), (0, 0), (0, 0), (CMP_FRONT, CMP_BACK), (0, 0)))
            kv6 = proj_p[:, :, 1024:2560].reshape(bp, lp, 6, ATT_KV, ATT_HEAD_DIM)
            kt, vv = _prompt_kv_layouts(kv6)
            amat_p = _importance_matrix(lp // CMP_STRIDE + CMP_FRONT + CMP_BACK, CMP_FRONT)
            o_p = attn_prompt(proj_p, cmp_p, kt, vv, tp, amat_p)
            xp = matmul_residual(o_p.reshape(bp * lp, -1), w_out, xp, tm_p)
            outs["kv_p"].append(kv6[:, :, 0:4])
            outs["win_p"].append(kv6[:, lp - WINDOW:, 4:6])
            proj_s = norm_matmul(xs, g, w_in, tm_s, 1024).reshape(bs, ls, -1)
            cache = cache_nsa_kv[li].reshape(cache_nsa_kv.shape[1], PAGE_SIZE, 4 * kvw)
            new_page = jnp.pad(proj_s[:, :, 1024:1024 + 2 * kvw], ((0, 0), (0, PAGE_SIZE - ls), (0, 0)))
            pre_s = cmp_pre_sample(page_table, cache, new_page, w_pair)
            cmp_s = cmp_finish(pre_s, pe8, w1, w2d)
            win = state_nsa_win[li].reshape(bs, WINDOW, 2 * kvw)
            amat_s = _importance_matrix(256, 0)
            o_s = attn_sample(page_table, cache, proj_s, win, cmp_s, ts, amat_s, past_len)
            xs = matmul_residual(o_s.reshape(bs * ls, -1), w_out, xs, tm_s)
            kv6s = proj_s[:, :, 1024:2560].reshape(bs, ls, 6, ATT_KV, ATT_HEAD_DIM)
            outs["kv_s"].append(kv6s[:, :, 0:4])
            outs["win_s"].append(jnp.concatenate([state_nsa_win[li][:, ls:], kv6s[:, :, 4:6]], axis=1))
        last = i == depth - 1
        xp = sqrelu_mlp_residual(xp, norm_ffn[i][None], ffn_w_up[i].astype(BF16), ffn_w_down[i].astype(BF16),
                                 norm_out[None], last, tm_p, 512)
        xs = sqrelu_mlp_residual(xs, norm_ffn[i][None], ffn_w_up[i].astype(BF16), ffn_w_down[i].astype(BF16),
                                 norm_out[None], last, tm_s, 512)
    st = lambda k: jnp.stack(outs[k])
    return (xp.reshape(bp, lp, d_model), xs.reshape(bs, ls, d_model), st("kv_p"), st("kv_s"), st("win_p"),
            st("win_s"), st("ssm_p"), st("ssm_s"), st("conv_p"), st("conv_s"), st("pool_p"), st("pool_s"))
```

```python
import functools
import math

import numpy as np
import jax
import jax.numpy as jnp
from jax import lax
from jax.experimental import pallas as pl
from jax.experimental.pallas import tpu as pltpu

F32 = jnp.float32
BF16 = jnp.bfloat16
HIGHEST = lax.Precision.HIGHEST
EPS = 1e-6
NEG_INF = float("-inf")

V7X_VMEM_LIMIT_BYTES = 56 * 1024 * 1024
LANES = 128
SUBLANES = 8

D_MODEL = 1024
SSD_HEAD_DIM = 64
SSD_GROUPS = 4
SSD_STATE = 128
SSD_CONV = 4
SSD_CHUNK = 128
POOL_WINDOWS = (2, 4, 8, 16)
POOL_HALO = 16
ATT_HEADS = 16
ATT_HEAD_DIM = 64
ATT_KV = 4
ATT_HG = 4
CMP_BLOCK = 32
CMP_STRIDE = 16
CMP_HIDDEN = 128
SEL_BLOCK = 64
N_SEL = 8
N_LOCAL = 2
FORCE_BONUS = 1000.0
WINDOW = 512
Q_TILE = 128
PAGE_SIZE = 128
N_BUCKETS = 32
MAX_DISTANCE = 128
CMP_FRONT = 112
CMP_BACK = 16
FAR_KEYS = 1024


def _cparams(*sem):
    return pltpu.CompilerParams(dimension_semantics=sem, vmem_limit_bytes=V7X_VMEM_LIMIT_BYTES)


def _bucket_thresholds():
    d = np.arange(0, MAX_DISTANCE + 1)
    max_exact = N_BUCKETS // 2
    nf = np.maximum(d, 1).astype(np.float32)
    large = max_exact + (np.log(nf / np.float32(max_exact)) / np.float32(math.log(MAX_DISTANCE / max_exact))
                         * np.float32(N_BUCKETS - max_exact)).astype(np.int32)
    large = np.minimum(large, N_BUCKETS - 1)
    b = np.where(d < max_exact, d, large)
    return [int(np.argmax(b >= k)) for k in range(N_BUCKETS)]


BUCKET_THR = _bucket_thresholds()


def _rms(x, g):
    return x * lax.rsqrt(jnp.mean(x * x, axis=-1, keepdims=True) + EPS) * g


def _sigmoid(x):
    return 0.5 * jnp.tanh(0.5 * x) + 0.5


def _silu(x):
    return x * _sigmoid(x)


def _softplus(x):
    return jnp.maximum(x, 0.0) + jnp.log1p(jnp.exp(-jnp.abs(x)))


def _dot(a, b):
    return jnp.dot(a, b, preferred_element_type=F32)


def _dot_nt(a, b):
    return lax.dot_general(a, b, (((1,), (1,)), ((), ())), preferred_element_type=F32)


def _dot_exact(a, b):
    return jnp.dot(a, b, precision=HIGHEST, preferred_element_type=F32)


def _dot_split3(a, b_bf16):
    hi = a.astype(BF16)
    r1 = a - hi.astype(F32)
    mid = r1.astype(BF16)
    lo = (r1 - mid.astype(F32)).astype(BF16)
    return _dot(hi, b_bf16) + _dot(mid, b_bf16) + _dot(lo, b_bf16)


def _norm_mm_kernel(x_ref, g_ref, w_ref, o_ref, h_scr):
    @pl.when(pl.program_id(1) == 0)
    def _():
        h_scr[...] = _rms(x_ref[...], g_ref[...]).astype(BF16)

    o_ref[...] = _dot(h_scr[...], w_ref[...])


def norm_matmul(x, g, w, tm, tn):
    T, D = x.shape
    N = w.shape[1]
    return pl.pallas_call(
        _norm_mm_kernel,
        grid=(T // tm, N // tn),
        in_specs=[pl.BlockSpec((tm, D), lambda i, j: (i, 0)),
                  pl.BlockSpec((1, D), lambda i, j: (0, 0)),
                  pl.BlockSpec((D, tn), lambda i, j: (0, j))],
        out_specs=pl.BlockSpec((tm, tn), lambda i, j: (i, j)),
        out_shape=jax.ShapeDtypeStruct((T, N), F32),
        scratch_shapes=[pltpu.VMEM((tm, D), BF16)],
        compiler_params=_cparams("parallel", "arbitrary"),
        name="norm_matmul",
    )(x, g, w)


def _mlp_kernel(x_ref, g_ref, wu_ref, wd_ref, go_ref, o_ref, h_scr, acc, *, final_norm):
    j = pl.program_id(1)

    @pl.when(j == 0)
    def _():
        h_scr[...] = _rms(x_ref[...], g_ref[...]).astype(BF16)
        acc[...] = jnp.zeros_like(acc)

    a = jnp.maximum(_dot(h_scr[...], wu_ref[...]), 0.0)
    acc[...] += _dot((a * a).astype(BF16), wd_ref[...])

    @pl.when(j == pl.num_programs(1) - 1)
    def _():
        y = x_ref[...] + acc[...]
        o_ref[...] = _rms(y, go_ref[...]) if final_norm else y


def sqrelu_mlp_residual(x, g, w_up, w_down, g_out, final_norm, tm, tf):
    T, D = x.shape
    F = w_up.shape[1]
    return pl.pallas_call(
        functools.partial(_mlp_kernel, final_norm=final_norm),
        grid=(T // tm, F // tf),
        in_specs=[pl.BlockSpec((tm, D), lambda i, j: (i, 0)),
                  pl.BlockSpec((1, D), lambda i, j: (0, 0)),
                  pl.BlockSpec((D, tf), lambda i, j: (0, j)),
                  pl.BlockSpec((tf, D), lambda i, j: (j, 0)),
                  pl.BlockSpec((1, D), lambda i, j: (0, 0))],
        out_specs=pl.BlockSpec((tm, D), lambda i, j: (i, 0)),
        out_shape=jax.ShapeDtypeStruct((T, D), F32),
        scratch_shapes=[pltpu.VMEM((tm, D), BF16), pltpu.VMEM((tm, D), F32)],
        compiler_params=_cparams("parallel", "arbitrary"),
        name="sqrelu_mlp",
    )(x, g, w_up, w_down, g_out)


def _mm_res_kernel(a_ref, w_ref, r_ref, o_ref):
    o_ref[...] = r_ref[...] + _dot(a_ref[...].astype(BF16), w_ref[...])


def matmul_residual(a, w, res, tm):
    T, K = a.shape
    D = w.shape[1]
    return pl.pallas_call(
        _mm_res_kernel,
        grid=(T // tm,),
        in_specs=[pl.BlockSpec((tm, K), lambda i: (i, 0)),
                  pl.BlockSpec((K, D), lambda i: (0, 0)),
                  pl.BlockSpec((tm, D), lambda i: (i, 0))],
        out_specs=pl.BlockSpec((tm, D), lambda i: (i, 0)),
        out_shape=jax.ShapeDtypeStruct((T, D), F32),
        compiler_params=_cparams("parallel"),
        name="matmul_residual",
    )(a, w, res)


def _pool_kernel(x_ref, halo_ref, g_ref, w_ref, sc_ref, o_ref, tail_ref, buf, *, start, tm):
    l = pl.program_id(1)

    @pl.when(l == 0)
    def _():
        buf[0:POOL_HALO] = halo_ref[0]

    x = x_ref[0]
    h = _rms(x, g_ref[...])
    buf[POOL_HALO:POOL_HALO + tm] = h
    pos = start + l * tm + lax.broadcasted_iota(jnp.int32, (tm, 1), 0)
    gc = x.shape[1] // len(POOL_WINDOWS)
    parts = []
    for gi, w in enumerate(POOL_WINDOWS):
        lo, hi = gi * gc, (gi + 1) * gc
        tot = buf[POOL_HALO:POOL_HALO + tm, lo:hi]
        for k in range(1, w):
            tot = tot + buf[POOL_HALO - k:POOL_HALO - k + tm, lo:hi]
        inv_cnt = 1.0 / jnp.minimum(pos + 1, w).astype(F32)
        diff = tot * inv_cnt - h[:, lo:hi]
        parts.append(_dot(diff.astype(BF16), w_ref[gi]))
    y = jnp.concatenate(parts, axis=1) * sc_ref[...]
    o_ref[0] = x + y
    t = buf[tm:tm + POOL_HALO]
    tail_ref[0] = t
    buf[0:POOL_HALO] = t


def pool_mixer_residual(x, halo, g, w_grp, scale, start, tm):
    B, L, D = x.shape
    return pl.pallas_call(
        functools.partial(_pool_kernel, start=start, tm=tm),
        grid=(B, L // tm),
        in_specs=[pl.BlockSpec((1, tm, D), lambda b, l: (b, l, 0)),
                  pl.BlockSpec((1, POOL_HALO, D), lambda b, l: (b, 0, 0)),
                  pl.BlockSpec((1, D), lambda b, l: (0, 0)),
                  pl.BlockSpec(w_grp.shape, lambda b, l: (0, 0, 0)),
                  pl.BlockSpec((1, D), lambda b, l: (0, 0))],
        out_specs=[pl.BlockSpec((1, tm, D), lambda b, l: (b, l, 0)),
                   pl.BlockSpec((1, POOL_HALO, D), lambda b, l: (b, 0, 0))],
        out_shape=[jax.ShapeDtypeStruct((B, L, D), F32), jax.ShapeDtypeStruct((B, POOL_HALO, D), F32)],
        scratch_shapes=[pltpu.VMEM((POOL_HALO + tm, D), F32)],
        compiler_params=_cparams("parallel", "arbitrary"),
        name="pool_mixer",
    )(x, halo, g, w_grp, scale)


def _ssd_pre_kernel(xbc_ref, dtr_ref, c0_ref, cw_ref, cb_ref, dtb_ref, xc_ref, dt_ref, tail_ref, ext, *, tm):
    @pl.when(pl.program_id(1) == 0)
    def _():
        ext[0:SUBLANES] = c0_ref[0]

    ext[SUBLANES:SUBLANES + tm] = xbc_ref[0]
    u = cb_ref[...]
    for k in range(SSD_CONV):
        off = SUBLANES - (SSD_CONV - 1) + k
        u = u + ext[off:off + tm] * cw_ref[k:k + 1]
    xc_ref[0] = _silu(u)
    t = ext[tm:tm + SUBLANES]
    tail_ref[0] = t
    ext[0:SUBLANES] = t
    dt_ref[0] = _softplus(dtr_ref[0] + dtb_ref[...])


def ssd_pre(proj, conv0, conv_w, conv_b, dt_bias, tm):
    B, L, _ = proj.shape
    C = conv_w.shape[1]
    NDT = dt_bias.shape[1]
    return pl.pallas_call(
        functools.partial(_ssd_pre_kernel, tm=tm),
        grid=(B, L // tm),
        in_specs=[pl.BlockSpec((1, tm, C), lambda b, l: (b, l, 1)),
                  pl.BlockSpec((1, tm, NDT), lambda b, l: (b, l, 4)),
                  pl.BlockSpec((1, SUBLANES, C), lambda b, l: (b, 0, 0)),
                  pl.BlockSpec((SSD_CONV, C), lambda b, l: (0, 0)),
                  pl.BlockSpec((1, C), lambda b, l: (0, 0)),
                  pl.BlockSpec((1, NDT), lambda b, l: (0, 0))],
        out_specs=[pl.BlockSpec((1, tm, C), lambda b, l: (b, l, 0)),
                   pl.BlockSpec((1, tm, NDT), lambda b, l: (b, l, 0)),
                   pl.BlockSpec((1, SUBLANES, C), lambda b, l: (b, 0, 0))],
        out_shape=[jax.ShapeDtypeStruct((B, L, C), F32), jax.ShapeDtypeStruct((B, L, NDT), F32),
                   jax.ShapeDtypeStruct((B, SUBLANES, C), F32)],
        scratch_shapes=[pltpu.VMEM((SUBLANES + tm, C), F32)],
        compiler_params=_cparams("parallel", "arbitrary"),
        name="ssd_pre",
    )(proj, proj, conv0, conv_w, conv_b, dt_bias)


def _pair_expand(v, q):
    lane = lax.broadcasted_iota(jnp.int32, (q, LANES), 1)
    tiles = []
    for p in range(4):
        lo = jnp.broadcast_to(v[:, 2 * p:2 * p + 1], (q, LANES))
        hi = jnp.broadcast_to(v[:, 2 * p + 1:2 * p + 2], (q, LANES))
        tiles.append(jnp.where(lane < SSD_HEAD_DIM, lo, hi))
    return jnp.concatenate(tiles, axis=1)


def _ssd_scan_kernel(xh_ref, b_ref, c_ref, dt_ref, alog_ref, tri_ref, sin_ref, y_ref, sout_ref, *, seg):
    Q = SSD_CHUNK
    nseg = Q // seg
    hpg = 8

    @pl.when(pl.program_id(2) == 0)
    def _():
        sout_ref[...] = sin_ref[0]

    xh = xh_ref[0]
    bg = b_ref[0].astype(BF16)
    cg = c_ref[0]
    dt = dt_ref[0]
    a = -jnp.exp(alog_ref[0])
    tri = tri_ref[...]
    acum = _dot_exact(tri, dt * a)
    acum_t = acum.T
    dt_t = dt.T
    causal = tri > 0.5
    cb = _dot_nt(cg.astype(BF16), bg)
    lane = lax.broadcasted_iota(jnp.int32, (Q, LANES), 1)
    low = lane < SSD_HEAD_DIM
    ys = []
    for p in range(hpg // 2):
        xpair = xh[:, LANES * p:LANES * (p + 1)]
        acc = None
        for which in range(2):
            h = 2 * p + which
            sg = acum[:, h:h + 1] - acum_t[h:h + 1, :]
            dec = jnp.exp(jnp.where(causal, sg, NEG_INF))
            wts = cb * dec * dt_t[h:h + 1, :]
            xm = jnp.where(low if which == 0 else jnp.logical_not(low), xpair, 0.0)
            term = _dot(wts.astype(BF16), xm.astype(BF16))
            acc = term if acc is None else acc + term
        ys.append(acc)
    y_intra = jnp.concatenate(ys, axis=1)

    if nseg == 1:
        alast = jnp.broadcast_to(acum[Q - 1:Q, :], (Q, LANES))
    else:
        r = lax.broadcasted_iota(jnp.int32, (Q, Q), 0)
        s = lax.broadcasted_iota(jnp.int32, (Q, Q), 1)
        lastsel = (s == r - (r & (seg - 1)) + (seg - 1)).astype(F32)
        alast = _dot_exact(lastsel, acum)
    wcol = jnp.exp(alast - acum) * dt
    xw_t = (xh * _pair_expand(wcol, Q)).T
    col = lax.broadcasted_iota(jnp.int32, (hpg * SSD_HEAD_DIM, Q), 1)
    y_parts = []
    for si in range(nseg):
        r0 = si * seg
        h0 = sout_ref[0, si].reshape(hpg * SSD_HEAD_DIM, SSD_STATE)
        y_parts.append(_dot_nt(cg[r0:r0 + seg], h0))
        xm = xw_t if nseg == 1 else jnp.where((col >= r0) & (col < r0 + seg), xw_t, 0.0)
        s_new = _dot(xm.astype(BF16), bg)
        for h in range(hpg):
            cd = jnp.exp(alast[r0:r0 + 1, h:h + 1])
            rows = slice(SSD_HEAD_DIM * h, SSD_HEAD_DIM * (h + 1))
            sout_ref[0, si, h] = h0[rows] * cd + s_new[rows]
    y_inter = y_parts[0] if nseg == 1 else jnp.concatenate(y_parts, axis=0)
    y_ref[0] = y_intra + y_inter * _pair_expand(jnp.exp(acum), Q)


def ssd_scan(xc, dt, a_log, tri, state0, li, seg):
    NB, R, _ = xc.shape
    Q = SSD_CHUNK
    nseg = Q // seg
    NC = R // Q
    G = SSD_GROUPS
    HD = 8 * SSD_HEAD_DIM
    return pl.pallas_call(
        functools.partial(_ssd_scan_kernel, seg=seg),
        grid=(NB, G, NC),
        in_specs=[pl.BlockSpec((1, Q, HD), lambda b, g, c: (b, c, g)),
                  pl.BlockSpec((1, Q, SSD_STATE), lambda b, g, c: (b, c, 16 + g)),
                  pl.BlockSpec((1, Q, SSD_STATE), lambda b, g, c: (b, c, 20 + g)),
                  pl.BlockSpec((1, Q, LANES), lambda b, g, c: (b, c, g)),
                  pl.BlockSpec((1, 1, LANES), lambda b, g, c: (g, 0, 0)),
                  pl.BlockSpec((Q, Q), lambda b, g, c: (0, 0)),
                  pl.BlockSpec((1, 1, nseg, 8, SSD_HEAD_DIM, SSD_STATE), lambda b, g, c: (li, b, 0, g, 0, 0))],
        out_specs=[pl.BlockSpec((1, Q, HD), lambda b, g, c: (b, c, g)),
                   pl.BlockSpec((1, nseg, 8, SSD_HEAD_DIM, SSD_STATE), lambda b, g, c: (b, 0, g, 0, 0))],
        out_shape=[jax.ShapeDtypeStruct((NB, R, G * HD), F32), jax.ShapeDtypeStruct(state0.shape[1:], F32)],
        compiler_params=_cparams("parallel", "parallel", "arbitrary"),
        name="ssd_scan",
    )(xc, xc, xc, dt, a_log, tri, state0)


def _ssd_post_kernel(y_ref, xh_ref, z_ref, d_ref, g_ref, w_ref, r_ref, o_ref):
    y = (y_ref[...] + xh_ref[...] * d_ref[...]) * _silu(z_ref[...])
    gw = y.shape[1] // SSD_GROUPS
    parts = []
    for gi in range(SSD_GROUPS):
        parts.append(_rms(y[:, gi * gw:(gi + 1) * gw], g_ref[:, gi * gw:(gi + 1) * gw]).astype(BF16))
    o_ref[...] = r_ref[...] + _dot(jnp.concatenate(parts, axis=1), w_ref[...])


def ssd_post(y, xc, proj, d_exp, norm_g, w_out, res, tm):
    T, DI = y.shape
    D = w_out.shape[1]
    return pl.pallas_call(
        _ssd_post_kernel,
        grid=(T // tm,),
        in_specs=[pl.BlockSpec((tm, DI), lambda i: (i, 0)),
                  pl.BlockSpec((tm, DI), lambda i: (i, 0)),
                  pl.BlockSpec((tm, DI), lambda i: (i, 0)),
                  pl.BlockSpec((1, DI), lambda i: (0, 0)),
                  pl.BlockSpec((1, DI), lambda i: (0, 0)),
                  pl.BlockSpec((DI, D), lambda i: (0, 0)),
                  pl.BlockSpec((tm, D), lambda i: (i, 0))],
        out_specs=pl.BlockSpec((tm, D), lambda i: (i, 0)),
        out_shape=jax.ShapeDtypeStruct((T, D), F32),
        compiler_params=_cparams("parallel"),
        name="ssd_post",
    )(y, xc, proj, d_exp, norm_g, w_out, res)


def _segment_tri(seg):
    r = np.arange(SSD_CHUNK)
    return jnp.asarray(((r[:, None] // seg == r[None, :] // seg) & (r[None, :] <= r[:, None])).astype(np.float32))


def ssd_layer(x, nb, conv_state, ssm_states, li, seg, w, tm_mm):
    T, D = x.shape
    L = T // nb
    proj = norm_matmul(x, w["norm"], w["w_in"], tm_mm, 1024)
    conv0 = jnp.pad(conv_state, ((0, 0), (SUBLANES - (SSD_CONV - 1), 0), (0, 0)))
    xc, dt, tail = ssd_pre(proj.reshape(nb, L, -1), conv0, w["conv_w"], w["conv_b"], w["dt_bias"],
                           min(L, SSD_CHUNK))
    rows = SSD_CHUNK if seg < SSD_CHUNK else L
    ngrp = T // rows
    nseg = SSD_CHUNK // seg
    st0 = ssm_states.reshape((ssm_states.shape[0], ngrp, nseg) + ssm_states.shape[2:])
    y, st = ssd_scan(xc.reshape(ngrp, rows, -1), dt.reshape(ngrp, rows, -1), w["a_log"], _segment_tri(seg), st0, li,
                     seg)
    x_new = ssd_post(y.reshape(T, -1), xc.reshape(T, -1), proj, w["d_exp"], w["norm_g"], w["w_out"], x, tm_mm)
    return x_new, tail[:, SUBLANES - (SSD_CONV - 1):], st.reshape(ssm_states.shape[1:])


def _bias_of(d, rb_ref, h):
    val = jnp.full(d.shape, rb_ref[0, h], F32)
    for k in range(1, N_BUCKETS):
        val = jnp.where(d >= BUCKET_THR[k], rb_ref[k, h], val)
    return jnp.where(d >= 0, val - rb_ref[N_BUCKETS - 1, h], 0.0)


def _bias_tables_kernel(rb_ref, tp_ref, ts_ref, *, past_len):
    r = lax.broadcasted_iota(jnp.int32, (Q_TILE, Q_TILE), 0)
    c = lax.broadcasted_iota(jnp.int32, (Q_TILE, Q_TILE), 1)
    t = lax.broadcasted_iota(jnp.int32, (SUBLANES, 1024), 0)
    j = lax.broadcasted_iota(jnp.int32, (SUBLANES, 1024), 1)
    win0 = past_len - WINDOW
    d_s = jnp.where(j < 256, past_len + t - (CMP_STRIDE * j + CMP_BLOCK - 1),
                    jnp.where(j < 384, past_len + t - (past_len - PAGE_SIZE + (j - 256)),
                              jnp.where(j < 512, t - (j - 384), past_len + t - (win0 + (j - 512)))))

    def body(h, carry):
        tp_ref[h, 0] = _bias_of(r - c, rb_ref, h)
        tp_ref[h, 1] = _bias_of(Q_TILE + r - c, rb_ref, h)
        tp_ref[h, 2] = _bias_of(r - CMP_STRIDE * (c - CMP_FRONT) - (CMP_BLOCK - 1), rb_ref, h)
        ts_ref[h] = _bias_of(d_s, rb_ref, h)
        return carry

    lax.fori_loop(0, ATT_HEADS, body, 0)


def bias_tables(rel_bias, past_len):
    return pl.pallas_call(
        functools.partial(_bias_tables_kernel, past_len=past_len),
        in_specs=[pl.BlockSpec(memory_space=pltpu.SMEM)],
        out_specs=[pl.BlockSpec(memory_space=pltpu.VMEM), pl.BlockSpec(memory_space=pltpu.VMEM)],
        out_shape=[jax.ShapeDtypeStruct((ATT_HEADS, 3, Q_TILE, Q_TILE), F32),
                   jax.ShapeDtypeStruct((ATT_HEADS, SUBLANES, 1024), F32)],
        name="bias_tables",
    )(rel_bias)


def _cmp_pre_kernel(*refs, nsrc, nsub, n_prefetch, n_native):
    refs = refs[n_prefetch:]
    srcs = refs[:nsrc]
    w_ref, o_ref, stage = refs[nsrc], refs[nsrc + 1], refs[nsrc + 2]
    rows = nsub * CMP_STRIDE
    n = nsrc * nsub
    for f in range(2):
        xs = []
        for pair in range(2):
            l0 = f * 256 + pair * LANES
            for si, src in enumerate(srcs):
                if si < n_native:
                    stage[si * rows:(si + 1) * rows] = src[0, 0, f, 2 * pair:2 * pair + 2].reshape(LANES, rows).T
                else:
                    stage[si * rows:(si + 1) * rows] = src[0, :, l0:l0 + LANES]
            cols = [stage[pl.ds(s, n, stride=CMP_STRIDE)] for s in range(CMP_STRIDE)]
            xs.append(jnp.concatenate(cols, axis=1))
        x2 = jnp.concatenate(xs, axis=0).astype(BF16)
        pre = _dot(x2, w_ref[f])
        for pair in range(2):
            o_ref[0, f, 2 * pair] = pre[pair * n:(pair + 1) * n, 0:256]
            o_ref[0, f, 2 * pair + 1] = pre[pair * n:(pair + 1) * n, 256:512]


def _cmp_fin_kernel(pre_ref, pe_ref, w1_ref, w2_ref, o_ref, scr, *, n):
    for f in range(2):
        cvec = _dot(pe_ref[f].astype(BF16), w1_ref[f])[0:1]
        for kv in range(ATT_KV):
            pre = pre_ref[0, f, kv]
            slot = f * ATT_KV + kv
            scr[slot, 0:n] = pre[:, CMP_HIDDEN:2 * CMP_HIDDEN]
            scr[slot, n:n + SUBLANES] = jnp.zeros((SUBLANES, CMP_HIDDEN), F32)
            hid = pre[:, 0:CMP_HIDDEN] + scr[slot, pl.ds(1, n)] + cvec
            o_ref[0, f, kv] = _dot(_silu(hid).astype(BF16), w2_ref[f])


def cmp_finish(pre, pe8, w1, w2d):
    B, _, KV, n, _ = pre.shape
    return pl.pallas_call(
        functools.partial(_cmp_fin_kernel, n=n),
        grid=(B,),
        in_specs=[pl.BlockSpec((1, 2, KV, n, 256), lambda b: (b, 0, 0, 0, 0)),
                  pl.BlockSpec(pe8.shape, lambda b: (0, 0, 0)),
                  pl.BlockSpec(w1.shape, lambda b: (0, 0, 0)),
                  pl.BlockSpec(w2d.shape, lambda b: (0, 0, 0))],
        out_specs=pl.BlockSpec((1, 2, KV, n, LANES), lambda b: (b, 0, 0, 0, 0)),
        out_shape=jax.ShapeDtypeStruct((B, 2, KV, n, LANES), F32),
        scratch_shapes=[pltpu.VMEM((2 * KV, n + SUBLANES, CMP_HIDDEN), F32)],
        compiler_params=_cparams("parallel"),
        name="cmp_finish",
    )(pre, pe8, w1, w2d)


def cmp_pre_prompt(proj, w_pair, rows):
    B, L, _ = proj.shape
    nsub = rows // CMP_STRIDE
    return pl.pallas_call(
        functools.partial(_cmp_pre_kernel, nsrc=1, nsub=nsub, n_prefetch=0, n_native=0),
        grid=(B, L // rows),
        in_specs=[pl.BlockSpec((1, rows, 512), lambda b, t: (b, t, 2)),
                  pl.BlockSpec(w_pair.shape, lambda b, t: (0, 0, 0))],
        out_specs=pl.BlockSpec((1, 2, ATT_KV, nsub, 256), lambda b, t: (b, 0, 0, t, 0)),
        out_shape=jax.ShapeDtypeStruct((B, 2, ATT_KV, L // CMP_STRIDE, 256), F32),
        scratch_shapes=[pltpu.VMEM((rows, LANES), F32)],
        compiler_params=_cparams("parallel", "parallel"),
        name="cmp_pre_prompt",
    )(proj, w_pair)


def cmp_pre_sample(page_table, cache_t, li, new_page, w_pair):
    B, n_pages = page_table.shape
    nsub = PAGE_SIZE // CMP_STRIDE
    nsrc = n_pages + 1

    def page_spec(p):
        return pl.BlockSpec((1, 1, 2, ATT_KV, ATT_HEAD_DIM, PAGE_SIZE), lambda b, pt: (li, pt[b, p], 0, 0, 0, 0))

    grid_spec = pltpu.PrefetchScalarGridSpec(
        num_scalar_prefetch=1,
        grid=(B,),
        in_specs=[page_spec(p) for p in range(n_pages)]
        + [pl.BlockSpec((1, PAGE_SIZE, 512), lambda b, pt: (b, 0, 0)),
           pl.BlockSpec(w_pair.shape, lambda b, pt: (0, 0, 0))],
        out_specs=pl.BlockSpec((1, 2, ATT_KV, nsrc * nsub, 256), lambda b, pt: (b, 0, 0, 0, 0)),
        scratch_shapes=[pltpu.VMEM((nsrc * PAGE_SIZE, LANES), F32)],
    )
    return pl.pallas_call(
        functools.partial(_cmp_pre_kernel, nsrc=nsrc, nsub=nsub, n_prefetch=1, n_native=n_pages),
        grid_spec=grid_spec,
        out_shape=jax.ShapeDtypeStruct((B, 2, ATT_KV, nsrc * nsub, 256), F32),
        compiler_params=_cparams("parallel"),
        name="cmp_pre_sample",
    )(page_table, *([cache_t] * n_pages), new_page, w_pair)


def _importance_matrix(n_rows, front):
    m = np.arange(n_rows)[:, None] - front
    j = np.arange(LANES)[None, :]
    ratio = SEL_BLOCK // CMP_STRIDE
    a = ((m >= ratio * j) & (m <= ratio * j + ratio - 1)).astype(np.float32) \
        + ((m >= ratio * j - 1) & (m <= ratio * j + ratio - 2)).astype(np.float32)
    return jnp.asarray(a)


def _softmax_rows(s):
    m = jnp.max(s, axis=-1, keepdims=True)
    m = jnp.where(m > NEG_INF, m, 0.0)
    e = jnp.exp(s - m)
    return e / jnp.maximum(jnp.sum(e, axis=-1, keepdims=True), 1e-30)


def _top_blocks(score, n_sel, axis=1):
    jb = lax.broadcasted_iota(jnp.int32, score.shape, axis).astype(F32)
    sel = jnp.zeros(score.shape, F32)
    sc = score
    for _ in range(n_sel):
        mx = jnp.max(sc, axis=axis, keepdims=True)
        idx = jnp.min(jnp.where(sc == mx, jb, 1e9), axis=axis, keepdims=True)
        pick = jb == idx
        sel = jnp.where(pick, 1.0, sel)
        sc = jnp.where(pick, NEG_INF, sc)
    return sel


def _tile4(x):
    return jnp.concatenate([x, x, x, x], axis=0)


def _attn_prompt_kernel(q_ref, gate_ref, kc_ref, vc_ref, ks_ref, vs_ref, kw_ref, vw_ref, tp_ref, amat_ref, o_ref,
                        mx_scr, acc_scr, *, nsub):
    i = pl.program_id(2)
    QT = Q_TILE
    R2 = 2 * QT
    low = lax.broadcasted_iota(jnp.int32, (QT, LANES), 1) < ATT_HEAD_DIM
    qpp = jnp.concatenate([q_ref[0, :, 0:LANES], q_ref[0, :, LANES:2 * LANES]], axis=0).astype(BF16)
    r1 = lax.broadcasted_iota(jnp.int32, (QT, 1), 0)
    c1 = lax.broadcasted_iota(jnp.int32, (1, LANES), 1)
    lower = r1 >= c1
    upper = r1 <= c1
    i_vec = jnp.full((QT, LANES), i, jnp.int32)

    def half_masks(n):
        lo = lax.broadcasted_iota(jnp.int32, (n, LANES), 1) < ATT_HEAD_DIM
        return jnp.where(lo, 1.0, 0.0).astype(BF16), jnp.where(lo, 0.0, 1.0).astype(BF16)

    def bd(x2):
        m_lo, m_hi = half_masks(x2.shape[0])
        return jnp.concatenate([x2 * m_lo, x2 * m_hi], axis=0)

    def spread(madd):
        m2 = jnp.concatenate([madd, madd], axis=1)
        return jnp.concatenate([m2, m2], axis=0)

    def bias_full(kind):
        return jnp.concatenate([jnp.concatenate([tp_ref[2 * p, kind], tp_ref[2 * p + 1, kind]], axis=1)
                                for p in range(2)], axis=0)

    def row_max2(x):
        parts = []
        for w in range(2):
            m = jnp.max(x[:, LANES * w:LANES * (w + 1)], axis=-1, keepdims=True)
            parts.append(jnp.broadcast_to(jnp.where(m > NEG_INF, m, 0.0), (R2, LANES)))
        return jnp.concatenate(parts, axis=1)

    def widen(m2, reps):
        if reps == 1:
            return m2
        return jnp.concatenate([m2[:, :LANES]] * reps + [m2[:, LANES:]] * reps, axis=1)

    def cols(ref, key0, n):
        return ref[0, :, pl.ds(pl.multiple_of(key0, LANES), n)]

    def bd_t(xt):
        z = jnp.zeros_like(xt)
        return jnp.concatenate([jnp.concatenate([xt, z], axis=0), jnp.concatenate([z, xt], axis=0)], axis=1)

    def scores(kt, add):
        return _dot(qpp, bd_t(kt)) + add

    def scores_nt(k2, add):
        return _dot_nt(qpp, bd(k2)) + add

    def weighted(s, m_wide, vt):
        n = vt.shape[1]
        same = jnp.right_shift(lax.broadcasted_iota(jnp.int32, (LANES, 2 * n), 0), ATT_HEAD_DIM.bit_length() - 1) \
            == (lax.broadcasted_iota(jnp.int32, (LANES, 2 * n), 1) >= n).astype(jnp.int32)
        rhs_t = jnp.concatenate([bd_t(vt), jnp.where(same, 1.0, 0.0).astype(BF16)], axis=0)
        return _dot_nt(jnp.exp(s - m_wide).astype(BF16), rhs_t)

    def normalized(acc):
        return acc[:, :LANES] / jnp.maximum(acc[:, LANES:], 1e-30)

    nback = WINDOW // QT
    ws, wt = [], []
    for back in range(nback, -1, -1):
        tc = jnp.maximum(i - back, 0)
        ok = i_vec >= back
        if back == nback:
            ok = ok & upper
        if back == 0:
            ok = lower
        madd = spread(jnp.where(ok, 0.0, NEG_INF))
        ws.append(scores(cols(kw_ref, tc * QT, QT), madd + bias_full(back) if back <= 1 else madd))
        wt.append(tc)
    mel = ws[0]
    for s in ws[1:]:
        mel = jnp.maximum(mel, s)
    m2_w = row_max2(mel)
    acc_w = None
    for s, tc in zip(ws, wt):
        term = weighted(s, m2_w, cols(vw_ref, tc * QT, QT))
        acc_w = term if acc_w is None else acc_w + term
    o_w = normalized(acc_w)

    st = pl.multiple_of(i * SUBLANES, SUBLANES)
    d_cmp = r1 - CMP_STRIDE * (c1 - CMP_FRONT) - (CMP_BLOCK - 1)
    ctiles = []
    for j in range(nsub // QT):
        rows = slice(CMP_FRONT + j * QT, CMP_FRONT + (j + 1) * QT)
        ok = jnp.broadcast_to(j * QT + c1 < i * SUBLANES - CMP_FRONT, (QT, LANES))
        ctiles.append((kc_ref[0, 0, 0, rows, :], vc_ref[0, 0, 0, rows, :], amat_ref[rows, :], ok, False))
    ctiles.append((kc_ref[0, 0, 0, pl.ds(st, QT), :], vc_ref[0, 0, 0, pl.ds(st, QT), :], amat_ref[pl.ds(st, QT), :],
                   (d_cmp >= 0) & (c1 >= CMP_FRONT - i * SUBLANES), True))
    ss = []
    for kt, _, _, ok, diag in ctiles:
        add = spread(jnp.where(ok, 0.0, NEG_INF))
        ss.append(scores_nt(kt.astype(BF16), add + bias_full(2) if diag else add))
    mel = ss[0]
    for s in ss[1:]:
        mel = jnp.maximum(mel, s)
    m2 = row_max2(mel)
    es = [jnp.exp(s - m2) for s in ss]
    lel = es[0]
    for e in es[1:]:
        lel = lel + e
    inv = []
    for w in range(2):
        l = jnp.sum(lel[:, LANES * w:LANES * (w + 1)], axis=-1, keepdims=True)
        inv.append(jnp.broadcast_to(1.0 / jnp.maximum(l, 1e-30), (R2, LANES)))
    inv2 = jnp.concatenate(inv, axis=1)
    o_c = None
    phs = []
    for e, (_, vt, _, _, _) in zip(es, ctiles):
        pn = e * inv2
        term = _dot(pn.astype(BF16), bd(vt.astype(BF16)))
        o_c = term if o_c is None else o_c + term
        phs.append(pn[:QT, :LANES] + pn[:QT, LANES:] + pn[QT:, :LANES] + pn[QT:, LANES:])
    imp = _dot_split3(jnp.concatenate(phs, axis=1),
                      jnp.concatenate([t[2] for t in ctiles], axis=0).astype(BF16))

    qblk = 2 * i + (c1 >= SEL_BLOCK).astype(jnp.int32)
    lag = qblk - r1
    allowed = lag >= 0
    forced = (r1 == 0) | (allowed & (lag < N_LOCAL))
    score_t = jnp.where(allowed, imp.T + jnp.where(forced, FORCE_BONUS, 0.0), -1.0)
    sel = _top_blocks(score_t, N_SEL, axis=0).T.astype(BF16)

    FK = FAR_KEYS
    tm1 = jnp.maximum(i - 1, 0)
    limit = tm1 * QT
    n_grp = jnp.right_shift(tm1 + FK // QT - 1, (FK // QT).bit_length() - 1)
    shift = SEL_BLOCK.bit_length() - 1

    def key_mask(key0, n):
        jb = lax.broadcasted_iota(jnp.int32, (LANES, n), 0)
        key = key0 + lax.broadcasted_iota(jnp.int32, (LANES, n), 1)
        hit = jb == jnp.right_shift(key, shift)
        if n == FK:
            hit = hit & (key < limit)
        return jnp.where(_dot(sel, jnp.where(hit, 1.0, 0.0).astype(BF16)) > 0.5, 0.0, NEG_INF)

    low2 = lax.broadcasted_iota(jnp.int32, (R2, LANES), 1) < ATT_HEAD_DIM

    def online_step(s, vt):
        n = vt.shape[1]
        reps = n // LANES
        halves = []
        for w in range(2):
            m = s[:, w * n:w * n + LANES]
            for c4 in range(1, reps):
                m = jnp.maximum(m, s[:, w * n + c4 * LANES:w * n + (c4 + 1) * LANES])
            halves.append(jnp.broadcast_to(jnp.max(m, axis=-1, keepdims=True), (R2, LANES)))
        m_old = mx_scr[...]
        m_new = jnp.maximum(m_old, jnp.concatenate(halves, axis=1))
        m_safe = jnp.where(m_new > NEG_INF, m_new, 0.0)
        alpha = jnp.exp(m_old - m_safe)
        a_mix = jnp.where(low2, alpha[:, :LANES], alpha[:, LANES:])
        mx_scr[...] = m_new
        acc_scr[...] = acc_scr[...] * jnp.concatenate([a_mix, a_mix], axis=1) + weighted(s, widen(m_safe, reps), vt)

    mx_scr[...] = jnp.full(mx_scr.shape, NEG_INF, F32)
    acc_scr[...] = jnp.zeros_like(acc_scr)

    def far_body(gi, c):
        s = scores(cols(ks_ref, gi * FK, FK), spread(key_mask(gi * FK, FK)))
        online_step(s, cols(vs_ref, gi * FK, FK))
        return c

    lax.fori_loop(0, n_grp, far_body, 0)
    mk_s = spread(jnp.where(i_vec >= 1, key_mask(tm1 * QT, QT), NEG_INF))
    online_step(scores(cols(ks_ref, tm1 * QT, QT), mk_s + bias_full(1)), cols(vs_ref, tm1 * QT, QT))
    mk_d = spread(jnp.where(lower, key_mask(i * QT, QT), NEG_INF))
    online_step(scores(cols(ks_ref, i * QT, QT), mk_d + bias_full(0)), cols(vs_ref, i * QT, QT))
    o_s = normalized(acc_scr[...])

    g = _sigmoid(gate_ref[0])

    def gate(br):
        tiles = []
        for p in range(2):
            c0 = br * ATT_HG + 2 * p
            tiles.append(jnp.where(low, jnp.broadcast_to(g[:, c0:c0 + 1], (QT, LANES)),
                                   jnp.broadcast_to(g[:, c0 + 1:c0 + 2], (QT, LANES))))
        return jnp.concatenate(tiles, axis=0)

    o = gate(0) * o_c + gate(1) * o_s + gate(2) * o_w
    o_ref[0] = jnp.concatenate([o[:QT], o[QT:]], axis=1)


def _nsa_in_proj_kernel(x_ref, g_ref, wr_ref, wt_ref, row_ref, kvt_ref, kvt16_ref):
    h = _rms(x_ref[0], g_ref[...]).astype(BF16)
    row_ref[0] = _dot(h, wr_ref[...])
    kv = _dot_nt(wt_ref[...], h)
    kvt_ref[0] = kv
    kvt16_ref[0] = kv[kv.shape[0] - kvt16_ref.shape[1]:].astype(BF16)


def nsa_in_proj_prompt(x, g, w_row, w_kvt, tm):
    B, L, D = x.shape
    NR, NT = w_row.shape[1], w_kvt.shape[0]
    n16 = 4 * ATT_KV * ATT_HEAD_DIM
    return pl.pallas_call(
        _nsa_in_proj_kernel,
        grid=(B, L // tm),
        in_specs=[pl.BlockSpec((1, tm, D), lambda b, l: (b, l, 0)),
                  pl.BlockSpec((1, D), lambda b, l: (0, 0)),
                  pl.BlockSpec((D, NR), lambda b, l: (0, 0)),
                  pl.BlockSpec((NT, D), lambda b, l: (0, 0))],
        out_specs=[pl.BlockSpec((1, tm, NR), lambda b, l: (b, l, 0)),
                   pl.BlockSpec((1, NT, tm), lambda b, l: (b, 0, l)),
                   pl.BlockSpec((1, n16, tm), lambda b, l: (b, 0, l))],
        out_shape=[jax.ShapeDtypeStruct((B, L, NR), F32), jax.ShapeDtypeStruct((B, NT, L), F32),
                   jax.ShapeDtypeStruct((B, n16, L), BF16)],
        compiler_params=_cparams("parallel", "parallel"),
        name="nsa_in_proj_prompt",
    )(x, g, w_row, w_kvt)


def attn_prompt(rowp, cmp_kv, kvt16, tp, amat):
    B, L, _ = rowp.shape
    nsub = L // CMP_STRIDE
    ncp = cmp_kv.shape[3]

    def kv_spec(f):
        return pl.BlockSpec((1, ATT_HEAD_DIM, L), lambda b, k, i: (b, f * ATT_KV + k, 0))

    def cmp_spec(f):
        return pl.BlockSpec((1, 1, 1, ncp, LANES), lambda b, k, i: (b, f, k, 0, 0))

    return pl.pallas_call(
        functools.partial(_attn_prompt_kernel, nsub=nsub),
        grid=(B, ATT_KV, L // Q_TILE),
        in_specs=[pl.BlockSpec((1, Q_TILE, 256), lambda b, k, i: (b, i, k)),
                  pl.BlockSpec((1, Q_TILE, LANES), lambda b, k, i: (b, i, 12 + k)),
                  cmp_spec(0), cmp_spec(1), kv_spec(0), kv_spec(1), kv_spec(2), kv_spec(3),
                  pl.BlockSpec((ATT_HG, 3, Q_TILE, Q_TILE), lambda b, k, i: (k, 0, 0, 0)),
                  pl.BlockSpec(amat.shape, lambda b, k, i: (0, 0))],
        out_specs=pl.BlockSpec((1, Q_TILE, 256), lambda b, k, i: (b, i, k)),
        out_shape=jax.ShapeDtypeStruct((B, L, ATT_HEADS * ATT_HEAD_DIM), F32),
        scratch_shapes=[pltpu.VMEM((2 * Q_TILE, 2 * LANES), F32),
                        pltpu.VMEM((2 * Q_TILE, 2 * LANES), F32)],
        compiler_params=_cparams("parallel", "parallel", "arbitrary"),
        name="attn_prompt",
    )(rowp, rowp, cmp_kv, cmp_kv, kvt16, kvt16, kvt16, kvt16, tp, amat)


def _attn_sample_kernel(*refs, n_pages, past_len):
    pages = refs[1:1 + n_pages]
    q_ref, gate_ref, new_s_ref, new_w_ref, win_ref, cmp_ref, ts_ref, amat_ref, o_ref = refs[1 + n_pages:]
    T = SUBLANES
    R = ATT_HG * T
    n_cmp_rows = cmp_ref.shape[3]
    lane_t = lax.broadcasted_iota(jnp.int32, (T, LANES), 1)
    low_t = lane_t < ATT_HEAD_DIM
    t_r = lax.broadcasted_iota(jnp.int32, (R, 1), 0) & (T - 1)
    c1 = lax.broadcasted_iota(jnp.int32, (1, LANES), 1)
    g = _sigmoid(gate_ref[0])
    zeros_new = jnp.zeros((LANES - T, LANES), F32)
    new_ok = (c1 <= t_r) & (c1 < T)
    n_c = lax.broadcasted_iota(jnp.int32, (1, 256), 1)
    d_c = past_len + t_r - (CMP_STRIDE * n_c + CMP_BLOCK - 1)
    c_w = lax.broadcasted_iota(jnp.int32, (1, WINDOW), 1)
    cmp_pad = jnp.zeros((256 - n_cmp_rows, LANES), F32)

    def lane_tile(kv):
        return LANES * (kv // 2)

    def queries(kv):
        par = kv % 2
        keep = (lane_t >= ATT_HEAD_DIM) if par else low_t
        qrows = []
        for hg in range(ATT_HG):
            h = kv * ATT_HG + hg
            q2 = q_ref[0, :, LANES * (h // 2):LANES * (h // 2 + 1)]
            if h % 2 != par:
                q2 = pltpu.roll(q2, ATT_HEAD_DIM, axis=1)
            qrows.append(jnp.where(keep, q2, 0.0))
        return jnp.concatenate(qrows, axis=0).astype(BF16)

    def new_rows(ref, l0):
        return jnp.concatenate([ref[0, :, l0:l0 + LANES], zeros_new], axis=0).astype(BF16)

    qs_all = [queries(kv) for kv in range(ATT_KV)]
    tabs = [ts_ref[kv * ATT_HG:(kv + 1) * ATT_HG].reshape(R, 1024) for kv in range(ATT_KV)]

    o_cs, p_sums = [], []
    for kv in range(ATT_KV):
        kc = jnp.concatenate([cmp_ref[0, 0, kv], cmp_pad], axis=0).astype(BF16)
        vc = jnp.concatenate([cmp_ref[0, 1, kv], cmp_pad], axis=0).astype(BF16)
        s_c = _dot_nt(qs_all[kv], kc) + tabs[kv][:, 0:256]
        p_c = _softmax_rows(jnp.where(d_c >= 0, s_c, NEG_INF))
        o_cs.append(_dot(p_c.astype(BF16), vc))
        p_sums.append(p_c[0:T] + p_c[T:2 * T] + p_c[2 * T:3 * T] + p_c[3 * T:4 * T])

    p_all = jnp.concatenate(p_sums + [jnp.zeros((LANES - ATT_KV * T, 256), F32)], axis=0)
    imp_t = _dot_split3(p_all, amat_ref[...].astype(BF16)).T
    rb = lax.broadcasted_iota(jnp.int32, (LANES, 1), 0)
    qblk = jnp.right_shift(past_len + (c1 & (T - 1)), SEL_BLOCK.bit_length() - 1)
    lag = qblk - rb
    allowed = lag >= 0
    forced = (rb == 0) | (allowed & (lag < N_LOCAL))
    score_t = jnp.where(allowed, imp_t + jnp.where(forced, FORCE_BONUS, 0.0), -1.0)
    sel_all = _top_blocks(score_t, N_SEL, axis=0).T

    out_tiles = []
    for kv in range(ATT_KV):
        par = kv % 2
        pl0 = lane_tile(kv)
        qs, tab = qs_all[kv], tabs[kv]
        sel = sel_all[kv * T:(kv + 1) * T]
        o_c = o_cs[kv]

        nb = 2 * n_pages
        n_past = n_pages * PAGE_SIZE
        pair = slice(2 * (kv // 2), 2 * (kv // 2) + 2)
        k_t = jnp.concatenate([pages[p][0, 0, 0, pair].reshape(LANES, PAGE_SIZE).astype(BF16)
                               for p in range(n_pages)], axis=1)
        v_t = jnp.concatenate([pages[p][0, 0, 1, pair].reshape(LANES, PAGE_SIZE).astype(BF16)
                               for p in range(n_pages)], axis=1)
        m_parts = [jnp.where(c1 < SEL_BLOCK, sel[:, 2 * p:2 * p + 1], sel[:, 2 * p + 1:2 * p + 2])
                   for p in range(n_pages)]
        m_parts.append(jnp.where(new_ok[0:T], jnp.broadcast_to(sel[:, nb:nb + 1], (T, LANES)), 0.0))
        zero = jnp.zeros((R, LANES), F32)
        bias_s = jnp.concatenate([zero] * (n_pages - 1) + [tab[:, 256:384], tab[:, 384:512]], axis=1)
        s_raw = jnp.concatenate([_dot(qs, k_t), _dot_nt(qs, new_rows(new_s_ref, pl0))], axis=1)
        s_all = jnp.where(_tile4(jnp.concatenate(m_parts, axis=1)) > 0.5, s_raw + bias_s, NEG_INF)
        p_s = _softmax_rows(s_all).astype(BF16)
        o_s = _dot_nt(p_s[:, :n_past], v_t) + _dot(p_s[:, n_past:], new_rows(new_s_ref, 256 + pl0))

        kw_t = win_ref[0, 0, 0, pair].reshape(LANES, WINDOW).astype(BF16)
        vw_t = win_ref[0, 0, 1, pair].reshape(LANES, WINDOW).astype(BF16)
        bias_w = jnp.concatenate([tab[:, 512:1024] + jnp.where(c_w >= t_r, 0.0, NEG_INF),
                                  tab[:, 384:512] + jnp.where(new_ok, 0.0, NEG_INF)], axis=1)
        s_w = jnp.concatenate([_dot(qs, kw_t), _dot_nt(qs, new_rows(new_w_ref, pl0))], axis=1) + bias_w
        p_w = _softmax_rows(s_w).astype(BF16)
        o_w = _dot_nt(p_w[:, :WINDOW], vw_t) + _dot(p_w[:, WINDOW:], new_rows(new_w_ref, 256 + pl0))

        gk = g[:, LANES * kv:LANES * (kv + 1)]
        for pair in range(2):
            halves = []
            for which in range(2):
                hg = 2 * pair + which
                rows = slice(hg * T, (hg + 1) * T)
                o = gk[:, hg:hg + 1] * o_c[rows] + gk[:, 4 + hg:5 + hg] * o_s[rows] + gk[:, 8 + hg:9 + hg] * o_w[rows]
                if which != par:
                    o = pltpu.roll(o, ATT_HEAD_DIM, axis=1)
                halves.append(o)
            out_tiles.append(jnp.where(low_t, halves[0], halves[1]))
    o_ref[0] = jnp.concatenate(out_tiles, axis=1)


def attn_sample(page_table, cache_t, win_t, li, proj, cmp_kv, ts, amat, past_len):
    B, n_pages = page_table.shape
    T = proj.shape[1]

    def page_spec(p):
        return pl.BlockSpec((1, 1, 2, ATT_KV, ATT_HEAD_DIM, PAGE_SIZE), lambda b, pt: (li, pt[b, p], 1, 0, 0, 0))

    grid_spec = pltpu.PrefetchScalarGridSpec(
        num_scalar_prefetch=1,
        grid=(B,),
        in_specs=[page_spec(p) for p in range(n_pages)]
        + [pl.BlockSpec((1, T, 1024), lambda b, pt: (b, 0, 0)),
           pl.BlockSpec((1, T, 512), lambda b, pt: (b, 0, 5)),
           pl.BlockSpec((1, T, 512), lambda b, pt: (b, 0, 3)),
           pl.BlockSpec((1, T, 512), lambda b, pt: (b, 0, 4)),
           pl.BlockSpec((1, 1) + win_t.shape[2:], lambda b, pt: (li, b, 0, 0, 0, 0)),
           pl.BlockSpec((1,) + cmp_kv.shape[1:], lambda b, pt: (b, 0, 0, 0, 0)),
           pl.BlockSpec(ts.shape, lambda b, pt: (0, 0, 0)),
           pl.BlockSpec(amat.shape, lambda b, pt: (0, 0))],
        out_specs=pl.BlockSpec((1, T, 1024), lambda b, pt: (b, 0, 0)),
    )
    return pl.pallas_call(
        functools.partial(_attn_sample_kernel, n_pages=n_pages, past_len=past_len),
        grid_spec=grid_spec,
        out_shape=jax.ShapeDtypeStruct((B, T, ATT_HEADS * ATT_HEAD_DIM), F32),
        compiler_params=_cparams("parallel"),
        name="attn_sample",
    )(page_table, *([cache_t] * n_pages), proj, proj, proj, proj, win_t, cmp_kv, ts, amat)


def _ssd_weights(i, li, norm_mix, ssd_w_in, ssd_conv_w, ssd_conv_b, ssd_dt_bias, ssd_a_log, ssd_d, ssd_norm, ssd_w_out):
    d_inner = ssd_w_out.shape[1]
    conv_dim = ssd_conv_w.shape[2]
    heads = ssd_dt_bias.shape[1]
    hpg = heads // SSD_GROUPS
    w = ssd_w_in[li]

    def per_group(v):
        v = v.reshape(v.shape[:-1] + (SSD_GROUPS, hpg))
        return jnp.pad(v, [(0, 0)] * (v.ndim - 1) + [(0, LANES - hpg)]).reshape(v.shape[:-2] + (SSD_GROUPS * LANES,))

    w_dt = per_group(w[:, d_inner + conv_dim:])
    w_in = jnp.concatenate([w[:, :d_inner], w_dt, jnp.zeros_like(w_dt), w[:, d_inner:d_inner + conv_dim]], axis=1)
    return dict(
        norm=norm_mix[i][None],
        w_in=w_in.astype(BF16),
        conv_w=ssd_conv_w[li], conv_b=ssd_conv_b[li][None],
        dt_bias=per_group(ssd_dt_bias[li])[None],
        a_log=per_group(ssd_a_log[li]).reshape(SSD_GROUPS, 1, LANES),
        d_exp=jnp.repeat(ssd_d[li], SSD_HEAD_DIM)[None],
        norm_g=ssd_norm[li][None],
        w_out=ssd_w_out[li].astype(BF16),
    )


def _nsa_in_weight(w):
    q_dim = ATT_HEADS * ATT_HEAD_DIM
    kv_dim = 6 * ATT_KV * ATT_HEAD_DIM
    idx = np.zeros((ATT_KV, LANES), np.int32)
    ok = np.zeros((ATT_KV, LANES), bool)
    for kv in range(ATT_KV):
        for br in range(3):
            for hg in range(ATT_HG):
                idx[kv, br * ATT_HG + hg] = q_dim + kv_dim + (kv * ATT_HG + hg) * 3 + br
                ok[kv, br * ATT_HG + hg] = True
    w_g = jnp.where(jnp.asarray(ok.reshape(-1))[None, :], w[:, idx.reshape(-1)], 0.0)
    return jnp.concatenate([w[:, :q_dim] * (ATT_HEAD_DIM ** -0.5), w[:, q_dim:q_dim + kv_dim], w_g], axis=1).astype(BF16)


def _prompt_kv_layouts(kv6):
    B, L = kv6.shape[:2]
    k = jnp.transpose(kv6[:, :, 2::2].astype(BF16), (0, 2, 3, 4, 1))
    k = jnp.concatenate([k, k], axis=3).reshape(B, 2, ATT_KV, LANES, L // Q_TILE, Q_TILE)
    v = jnp.transpose(kv6[:, :, 3::2].astype(BF16), (0, 2, 3, 1, 4))
    return jnp.transpose(k, (0, 1, 2, 4, 3, 5)), jnp.concatenate([v, v], axis=-1)


def _cmp_pair_weight(w1):
    w1r = w1.reshape(2, 2, CMP_STRIDE, ATT_HEAD_DIM, CMP_HIDDEN)
    eye = jnp.eye(2, dtype=w1.dtype)
    wp = jnp.einsum("fjsde,wv->fswdvje", w1r, eye)
    return wp.reshape(2, CMP_STRIDE * 2 * ATT_HEAD_DIM, 2 * 2 * CMP_HIDDEN).astype(BF16)


def kernel(x_prompt, x_sample, cache_nsa_kv, state_nsa_win, state_ssm, state_conv, state_pool, page_table, rel_bias,
           norm_mix, norm_ffn, norm_out, ffn_w_up, ffn_w_down, ssd_w_in, ssd_conv_w, ssd_conv_b, ssd_dt_bias,
           ssd_a_log, ssd_d, ssd_norm, ssd_w_out, pool_w, pool_scale, nsa_w_in, nsa_cmp_pe, nsa_cmp_w1, nsa_cmp_w2,
           nsa_w_out):
    bp, lp, d_model = x_prompt.shape
    bs, ls, _ = x_sample.shape
    depth = norm_mix.shape[0]
    n_pages = page_table.shape[1]
    past_len = n_pages * PAGE_SIZE
    assert ls == SUBLANES and lp % (16 * Q_TILE) == 0 and past_len >= WINDOW and state_nsa_win.shape[2] == WINDOW
    xp = x_prompt.reshape(bp * lp, d_model)
    xs = x_sample.reshape(bs * ls, d_model)
    tm_p, tm_s = 512, 512
    outs = {k: [] for k in ("kv_p", "kv_s", "win_p", "win_s", "ssm_p", "ssm_s", "conv_p", "conv_s", "pool_p", "pool_s")}
    for i in range(depth):
        kind, li = i % 3, i // 3
        if kind == 0:
            w = _ssd_weights(i, li, norm_mix, ssd_w_in, ssd_conv_w, ssd_conv_b, ssd_dt_bias, ssd_a_log, ssd_d,
                             ssd_norm, ssd_w_out)
            conv_dim = ssd_conv_w.shape[2]
            xp, c_p, s_p = ssd_layer(xp, bp, jnp.zeros((bp, SSD_CONV - 1, conv_dim), F32),
                                     jnp.zeros((1, bp) + state_ssm.shape[2:], F32), 0, SSD_CHUNK, w, tm_p)
            xs, c_s, s_s = ssd_layer(xs, bs, state_conv[li], state_ssm, li, ls, w, tm_s)
            outs["conv_p"].append(c_p)
            outs["conv_s"].append(c_s)
            outs["ssm_p"].append(s_p)
            outs["ssm_s"].append(s_s)
        elif kind == 1:
            g = norm_mix[i][None]
            pw = pool_w[li].astype(BF16)
            sc = pool_scale[li][None]
            xp3, tail_p = pool_mixer_residual(xp.reshape(bp, lp, d_model), jnp.zeros((bp, POOL_HALO, d_model), F32),
                                              g, pw, sc, 0, 512)
            halo_s = jnp.pad(state_pool[li], ((0, 0), (1, 0), (0, 0)))
            xs3, tail_s = pool_mixer_residual(xs.reshape(bs, ls, d_model), halo_s, g, pw, sc, past_len, ls)
            xp, xs = xp3.reshape(bp * lp, d_model), xs3.reshape(bs * ls, d_model)
            outs["pool_p"].append(tail_p[:, 1:])
            outs["pool_s"].append(tail_s[:, 1:])
        else:
            g = norm_mix[i][None]
            w_in = _nsa_in_weight(nsa_w_in[li])
            w_pair = _cmp_pair_weight(nsa_cmp_w1[li])
            pe8 = jnp.broadcast_to(nsa_cmp_pe[li].reshape(2, 1, -1), (2, SUBLANES, CMP_BLOCK * ATT_HEAD_DIM))
            w1 = nsa_cmp_w1[li].astype(BF16)
            w2d = jnp.concatenate([nsa_cmp_w2[li], nsa_cmp_w2[li]], axis=-1).astype(BF16)
            tp, ts = bias_tables(rel_bias, past_len)
            w_out = nsa_w_out[li].astype(BF16)
            kvw = ATT_KV * ATT_HEAD_DIM
            cache_t = jnp.transpose(cache_nsa_kv, (0, 1, 3, 4, 5, 2))
            win_t = jnp.transpose(state_nsa_win, (0, 1, 3, 4, 5, 2))
            w_row = jnp.concatenate([w_in[:, :1024 + 2 * kvw], w_in[:, 1024 + 6 * kvw:]], axis=1)
            w_kvt = w_in[:, 1024:1024 + 6 * kvw].T
            rowp, kvt, kvt16 = nsa_in_proj_prompt(xp.reshape(bp, lp, d_model), g, w_row, w_kvt, tm_p)
            pre_p = cmp_pre_prompt(rowp, w_pair, 2048)
            cmp_p = jnp.pad(cmp_finish(pre_p, pe8, w1, w2d), ((0, 0), (0, 0), (0, 0), (CMP_FRONT, CMP_BACK), (0, 0)))
            amat_p = _importance_matrix(lp // CMP_STRIDE + CMP_FRONT + CMP_BACK, CMP_FRONT)
            o_p = attn_prompt(rowp, cmp_p, kvt16, tp, amat_p)
            xp = matmul_residual(o_p.reshape(bp * lp, -1), w_out, xp, tm_p)
            kvt6 = kvt.reshape(bp, 6, ATT_KV, ATT_HEAD_DIM, lp)
            outs["kv_p"].append(jnp.transpose(kvt6[:, 0:4], (0, 4, 1, 2, 3)))
            outs["win_p"].append(jnp.transpose(kvt6[:, 4:6, :, :, lp - WINDOW:], (0, 4, 1, 2, 3)))
            proj_s = norm_matmul(xs, g, w_in, tm_s, 1024).reshape(bs, ls, -1)
            new_page = jnp.pad(proj_s[:, :, 1024:1024 + 2 * kvw], ((0, 0), (0, PAGE_SIZE - ls), (0, 0)))
            pre_s = cmp_pre_sample(page_table, cache_t, li, new_page, w_pair)
            cmp_s = cmp_finish(pre_s, pe8, w1, w2d)
            amat_s = _importance_matrix(256, 0)
            o_s = attn_sample(page_table, cache_t, win_t, li, proj_s, cmp_s, ts, amat_s, past_len)
            xs = matmul_residual(o_s.reshape(bs * ls, -1), w_out, xs, tm_s)
            kv6s = proj_s[:, :, 1024:2560].reshape(bs, ls, 6, ATT_KV, ATT_HEAD_DIM)
            outs["kv_s"].append(kv6s[:, :, 0:4])
            new_win_t = jnp.concatenate([win_t[li][..., ls:], jnp.transpose(kv6s[:, :, 4:6], (0, 2, 3, 4, 1))], axis=-1)
            outs["win_s"].append(jnp.transpose(new_win_t, (0, 4, 1, 2, 3)))
        last = i == depth - 1
        xp = sqrelu_mlp_residual(xp, norm_ffn[i][None], ffn_w_up[i].astype(BF16), ffn_w_down[i].astype(BF16),
                                 norm_out[None], last, tm_p, 512)
        xs = sqrelu_mlp_residual(xs, norm_ffn[i][None], ffn_w_up[i].astype(BF16), ffn_w_down[i].astype(BF16),
                                 norm_out[None], last, tm_s, 512)
    st = lambda k: jnp.stack(outs[k])
    return (xp.reshape(bp, lp, d_model), xs.reshape(bs, ls, d_model), st("kv_p"), st("kv_s"), st("win_p"),
            st("win_s"), st("ssm_p"), st("ssm_s"), st("conv_p"), st("conv_s"), st("pool_p"), st("pool_s"))
```

```python
import functools
import math

import numpy as np
import jax
import jax.numpy as jnp
from jax import lax
from jax.experimental import pallas as pl
from jax.experimental.pallas import tpu as pltpu

F32 = jnp.float32
BF16 = jnp.bfloat16
HIGHEST = lax.Precision.HIGHEST
EPS = 1e-6
NEG_INF = float("-inf")
LOG2E = math.log2(math.e)

V7X_VMEM_LIMIT_BYTES = 56 * 1024 * 1024
LANES = 128
SUBLANES = 8

D_MODEL = 1024
SSD_HEAD_DIM = 64
SSD_GROUPS = 4
SSD_STATE = 128
SSD_CONV = 4
SSD_CHUNK = 128
POOL_WINDOWS = (2, 4, 8, 16)
POOL_HALO = 16
ATT_HEADS = 16
ATT_HEAD_DIM = 64
ATT_KV = 4
ATT_HG = 4
CMP_BLOCK = 32
CMP_STRIDE = 16
CMP_HIDDEN = 128
SEL_BLOCK = 64
N_SEL = 8
N_LOCAL = 2
FORCE_BONUS = 1000.0
WINDOW = 512
Q_TILE = 128
PAGE_SIZE = 128
N_BUCKETS = 32
MAX_DISTANCE = 128
CMP_FRONT = 112
CMP_BACK = 16
FAR_KEYS = 1024


def _cparams(*sem):
    return pltpu.CompilerParams(dimension_semantics=sem, vmem_limit_bytes=V7X_VMEM_LIMIT_BYTES)


def _bucket_thresholds():
    d = np.arange(0, MAX_DISTANCE + 1)
    max_exact = N_BUCKETS // 2
    nf = np.maximum(d, 1).astype(np.float32)
    large = max_exact + (np.log(nf / np.float32(max_exact)) / np.float32(math.log(MAX_DISTANCE / max_exact))
                         * np.float32(N_BUCKETS - max_exact)).astype(np.int32)
    large = np.minimum(large, N_BUCKETS - 1)
    b = np.where(d < max_exact, d, large)
    return [int(np.argmax(b >= k)) for k in range(N_BUCKETS)]


BUCKET_THR = _bucket_thresholds()


def _rms(x, g):
    return x * lax.rsqrt(jnp.mean(x * x, axis=-1, keepdims=True) + EPS) * g


def _sigmoid(x):
    return 0.5 * jnp.tanh(0.5 * x) + 0.5


def _silu(x):
    return x * _sigmoid(x)


def _softplus(x):
    return jnp.maximum(x, 0.0) + jnp.log1p(jnp.exp(-jnp.abs(x)))


def _dot(a, b):
    return jnp.dot(a, b, preferred_element_type=F32)


def _dot_nt(a, b):
    return lax.dot_general(a, b, (((1,), (1,)), ((), ())), preferred_element_type=F32)


def _dot_exact(a, b):
    return jnp.dot(a, b, precision=HIGHEST, preferred_element_type=F32)


def _dot_split3(a, b_bf16):
    hi = a.astype(BF16)
    r1 = a - hi.astype(F32)
    mid = r1.astype(BF16)
    lo = (r1 - mid.astype(F32)).astype(BF16)
    return _dot(hi, b_bf16) + _dot(mid, b_bf16) + _dot(lo, b_bf16)


def _norm_mm_kernel(x_ref, g_ref, w_ref, o_ref, h_scr):
    @pl.when(pl.program_id(1) == 0)
    def _():
        h_scr[...] = _rms(x_ref[...], g_ref[...]).astype(BF16)

    o_ref[...] = _dot(h_scr[...], w_ref[...])


def norm_matmul(x, g, w, tm, tn):
    T, D = x.shape
    N = w.shape[1]
    return pl.pallas_call(
        _norm_mm_kernel,
        grid=(T // tm, N // tn),
        in_specs=[pl.BlockSpec((tm, D), lambda i, j: (i, 0)),
                  pl.BlockSpec((1, D), lambda i, j: (0, 0)),
                  pl.BlockSpec((D, tn), lambda i, j: (0, j))],
        out_specs=pl.BlockSpec((tm, tn), lambda i, j: (i, j)),
        out_shape=jax.ShapeDtypeStruct((T, N), F32),
        scratch_shapes=[pltpu.VMEM((tm, D), BF16)],
        compiler_params=_cparams("parallel", "arbitrary"),
        name="norm_matmul",
    )(x, g, w)


def _mlp_kernel(x_ref, g_ref, wu_ref, wd_ref, go_ref, o_ref, h_scr, acc, *, final_norm):
    j = pl.program_id(1)

    @pl.when(j == 0)
    def _():
        h_scr[...] = _rms(x_ref[...], g_ref[...]).astype(BF16)
        acc[...] = jnp.zeros_like(acc)

    a = jnp.maximum(_dot(h_scr[...], wu_ref[...]), 0.0)
    acc[...] += _dot((a * a).astype(BF16), wd_ref[...])

    @pl.when(j == pl.num_programs(1) - 1)
    def _():
        y = x_ref[...] + acc[...]
        o_ref[...] = _rms(y, go_ref[...]) if final_norm else y


def sqrelu_mlp_residual(x, g, w_up, w_down, g_out, final_norm, tm, tf):
    T, D = x.shape
    F = w_up.shape[1]
    return pl.pallas_call(
        functools.partial(_mlp_kernel, final_norm=final_norm),
        grid=(T // tm, F // tf),
        in_specs=[pl.BlockSpec((tm, D), lambda i, j: (i, 0)),
                  pl.BlockSpec((1, D), lambda i, j: (0, 0)),
                  pl.BlockSpec((D, tf), lambda i, j: (0, j)),
                  pl.BlockSpec((tf, D), lambda i, j: (j, 0)),
                  pl.BlockSpec((1, D), lambda i, j: (0, 0))],
        out_specs=pl.BlockSpec((tm, D), lambda i, j: (i, 0)),
        out_shape=jax.ShapeDtypeStruct((T, D), F32),
        scratch_shapes=[pltpu.VMEM((tm, D), BF16), pltpu.VMEM((tm, D), F32)],
        compiler_params=_cparams("parallel", "arbitrary"),
        name="sqrelu_mlp",
    )(x, g, w_up, w_down, g_out)


def _mm_res_kernel(a_ref, w_ref, r_ref, o_ref):
    o_ref[...] = r_ref[...] + _dot(a_ref[...].astype(BF16), w_ref[...])


def matmul_residual(a, w, res, tm):
    T, K = a.shape
    D = w.shape[1]
    return pl.pallas_call(
        _mm_res_kernel,
        grid=(T // tm,),
        in_specs=[pl.BlockSpec((tm, K), lambda i: (i, 0)),
                  pl.BlockSpec((K, D), lambda i: (0, 0)),
                  pl.BlockSpec((tm, D), lambda i: (i, 0))],
        out_specs=pl.BlockSpec((tm, D), lambda i: (i, 0)),
        out_shape=jax.ShapeDtypeStruct((T, D), F32),
        compiler_params=_cparams("parallel"),
        name="matmul_residual",
    )(a, w, res)


def _pool_kernel(x_ref, halo_ref, g_ref, w_ref, sc_ref, o_ref, tail_ref, buf, *, start, tm):
    l = pl.program_id(1)

    @pl.when(l == 0)
    def _():
        buf[0:POOL_HALO] = halo_ref[0]

    x = x_ref[0]
    h = _rms(x, g_ref[...])
    buf[POOL_HALO:POOL_HALO + tm] = h
    pos = start + l * tm + lax.broadcasted_iota(jnp.int32, (tm, 1), 0)
    gc = x.shape[1] // len(POOL_WINDOWS)
    parts = []
    for gi, w in enumerate(POOL_WINDOWS):
        lo, hi = gi * gc, (gi + 1) * gc
        tot = buf[POOL_HALO:POOL_HALO + tm, lo:hi]
        for k in range(1, w):
            tot = tot + buf[POOL_HALO - k:POOL_HALO - k + tm, lo:hi]
        inv_cnt = 1.0 / jnp.minimum(pos + 1, w).astype(F32)
        diff = tot * inv_cnt - h[:, lo:hi]
        parts.append(_dot(diff.astype(BF16), w_ref[gi]))
    y = jnp.concatenate(parts, axis=1) * sc_ref[...]
    o_ref[0] = x + y
    t = buf[tm:tm + POOL_HALO]
    tail_ref[0] = t
    buf[0:POOL_HALO] = t


def pool_mixer_residual(x, halo, g, w_grp, scale, start, tm):
    B, L, D = x.shape
    return pl.pallas_call(
        functools.partial(_pool_kernel, start=start, tm=tm),
        grid=(B, L // tm),
        in_specs=[pl.BlockSpec((1, tm, D), lambda b, l: (b, l, 0)),
                  pl.BlockSpec((1, POOL_HALO, D), lambda b, l: (b, 0, 0)),
                  pl.BlockSpec((1, D), lambda b, l: (0, 0)),
                  pl.BlockSpec(w_grp.shape, lambda b, l: (0, 0, 0)),
                  pl.BlockSpec((1, D), lambda b, l: (0, 0))],
        out_specs=[pl.BlockSpec((1, tm, D), lambda b, l: (b, l, 0)),
                   pl.BlockSpec((1, POOL_HALO, D), lambda b, l: (b, 0, 0))],
        out_shape=[jax.ShapeDtypeStruct((B, L, D), F32), jax.ShapeDtypeStruct((B, POOL_HALO, D), F32)],
        scratch_shapes=[pltpu.VMEM((POOL_HALO + tm, D), F32)],
        compiler_params=_cparams("parallel", "arbitrary"),
        name="pool_mixer",
    )(x, halo, g, w_grp, scale)


def _ssd_pre_kernel(xbc_ref, dtr_ref, c0_ref, cw_ref, cb_ref, dtb_ref, xc_ref, dt_ref, tail_ref, ext, *, tm):
    @pl.when(pl.program_id(1) == 0)
    def _():
        ext[0:SUBLANES] = c0_ref[0]

    ext[SUBLANES:SUBLANES + tm] = xbc_ref[0]
    u = cb_ref[...]
    for k in range(SSD_CONV):
        off = SUBLANES - (SSD_CONV - 1) + k
        u = u + ext[off:off + tm] * cw_ref[k:k + 1]
    xc_ref[0] = _silu(u)
    t = ext[tm:tm + SUBLANES]
    tail_ref[0] = t
    ext[0:SUBLANES] = t
    dt_ref[0] = _softplus(dtr_ref[0] + dtb_ref[...])


def ssd_pre(proj, conv0, conv_w, conv_b, dt_bias, tm):
    B, L, _ = proj.shape
    C = conv_w.shape[1]
    NDT = dt_bias.shape[1]
    return pl.pallas_call(
        functools.partial(_ssd_pre_kernel, tm=tm),
        grid=(B, L // tm),
        in_specs=[pl.BlockSpec((1, tm, C), lambda b, l: (b, l, 1)),
                  pl.BlockSpec((1, tm, NDT), lambda b, l: (b, l, 4)),
                  pl.BlockSpec((1, SUBLANES, C), lambda b, l: (b, 0, 0)),
                  pl.BlockSpec((SSD_CONV, C), lambda b, l: (0, 0)),
                  pl.BlockSpec((1, C), lambda b, l: (0, 0)),
                  pl.BlockSpec((1, NDT), lambda b, l: (0, 0))],
        out_specs=[pl.BlockSpec((1, tm, C), lambda b, l: (b, l, 0)),
                   pl.BlockSpec((1, tm, NDT), lambda b, l: (b, l, 0)),
                   pl.BlockSpec((1, SUBLANES, C), lambda b, l: (b, 0, 0))],
        out_shape=[jax.ShapeDtypeStruct((B, L, C), F32), jax.ShapeDtypeStruct((B, L, NDT), F32),
                   jax.ShapeDtypeStruct((B, SUBLANES, C), F32)],
        scratch_shapes=[pltpu.VMEM((SUBLANES + tm, C), F32)],
        compiler_params=_cparams("parallel", "arbitrary"),
        name="ssd_pre",
    )(proj, proj, conv0, conv_w, conv_b, dt_bias)


def _head_expand_matrix():
    h = np.arange(LANES)[:, None]
    lane = np.arange(8 * SSD_HEAD_DIM)[None, :]
    return jnp.asarray((lane // SSD_HEAD_DIM == h).astype(np.float32)).astype(BF16)


def _ssd_scan_kernel(xh_ref, b_ref, c_ref, dt_ref, alog_ref, tri_ref, e_ref, sin_ref, y_ref, sout_ref, *, seg, gps):
    @pl.when(pl.program_id(2) == 0)
    def _():
        sout_ref[...] = sin_ref[0]

    for gg in range(gps):
        _ssd_scan_group(xh_ref, b_ref, c_ref, dt_ref, alog_ref, tri_ref, e_ref, y_ref, sout_ref, seg, gg)


def _ssd_scan_group(xh_ref, b_ref, c_ref, dt_ref, alog_ref, tri_ref, e_ref, y_ref, sout_ref, seg, gg):
    Q = SSD_CHUNK
    nseg = Q // seg
    hpg = 8
    HD = hpg * SSD_HEAD_DIM
    glanes = slice(gg * LANES, (gg + 1) * LANES)
    xh = xh_ref[0, :, gg * HD:(gg + 1) * HD]
    bg = b_ref[0, :, glanes].astype(BF16)
    cg = c_ref[0, :, glanes]
    dt = dt_ref[0, :, glanes]
    a = -jnp.exp(alog_ref[gg])
    tri = tri_ref[...]
    acum = _dot_exact(tri, dt * a)
    acum_t = acum.T
    dt_t = dt.T
    causal = tri > 0.5
    cb = _dot_nt(cg.astype(BF16), bg)
    lane = lax.broadcasted_iota(jnp.int32, (Q, LANES), 1)
    low = lane < SSD_HEAD_DIM
    ys = []
    for p in range(hpg // 2):
        xpair = xh[:, LANES * p:LANES * (p + 1)]
        acc = None
        for which in range(2):
            h = 2 * p + which
            sg = acum[:, h:h + 1] - acum_t[h:h + 1, :]
            dec = jnp.exp(jnp.where(causal, sg, NEG_INF))
            wts = cb * dec * dt_t[h:h + 1, :]
            xm = jnp.where(low if which == 0 else jnp.logical_not(low), xpair, 0.0)
            term = _dot(wts.astype(BF16), xm.astype(BF16))
            acc = term if acc is None else acc + term
        ys.append(acc)
    y_intra = jnp.concatenate(ys, axis=1)

    if nseg == 1:
        alast = jnp.broadcast_to(acum[Q - 1:Q, :], (Q, LANES))
    else:
        r = lax.broadcasted_iota(jnp.int32, (Q, Q), 0)
        s = lax.broadcasted_iota(jnp.int32, (Q, Q), 1)
        lastsel = (s == r - (r & (seg - 1)) + (seg - 1)).astype(F32)
        alast = _dot_exact(lastsel, acum)
    wcol = jnp.exp(alast - acum) * dt
    xw_t = (xh * _dot_split3(wcol, e_ref[...])).T
    col = lax.broadcasted_iota(jnp.int32, (hpg * SSD_HEAD_DIM, Q), 1)
    y_parts = []
    for si in range(nseg):
        r0 = si * seg
        h0 = sout_ref[0, si, gg * hpg:(gg + 1) * hpg].reshape(HD, SSD_STATE)
        y_parts.append(_dot_nt(cg[r0:r0 + seg], h0))
        xm = xw_t if nseg == 1 else jnp.where((col >= r0) & (col < r0 + seg), xw_t, 0.0)
        s_new = _dot(xm.astype(BF16), bg)
        for h in range(hpg):
            cd = jnp.exp(alast[r0:r0 + 1, h:h + 1])
            rows = slice(SSD_HEAD_DIM * h, SSD_HEAD_DIM * (h + 1))
            sout_ref[0, si, gg * hpg + h] = h0[rows] * cd + s_new[rows]
    y_inter = y_parts[0] if nseg == 1 else jnp.concatenate(y_parts, axis=0)
    y_ref[0, :, gg * HD:(gg + 1) * HD] = y_intra + y_inter * _dot_split3(jnp.exp(acum), e_ref[...])


def ssd_scan(xc, dt, a_log, tri, state0, li, seg):
    NB, R, _ = xc.shape
    Q = SSD_CHUNK
    nseg = Q // seg
    NC = R // Q
    G = SSD_GROUPS
    HD = 8 * SSD_HEAD_DIM
    gps = G if nseg == 1 else 1
    x_blocks = (G * HD) // (gps * SSD_STATE)
    return pl.pallas_call(
        functools.partial(_ssd_scan_kernel, seg=seg, gps=gps),
        grid=(NB, G // gps, NC),
        in_specs=[pl.BlockSpec((1, Q, gps * HD), lambda b, g, c: (b, c, g)),
                  pl.BlockSpec((1, Q, gps * SSD_STATE), lambda b, g, c: (b, c, x_blocks + g)),
                  pl.BlockSpec((1, Q, gps * SSD_STATE), lambda b, g, c: (b, c, x_blocks + G // gps + g)),
                  pl.BlockSpec((1, Q, gps * LANES), lambda b, g, c: (b, c, g)),
                  pl.BlockSpec((gps, 1, LANES), lambda b, g, c: (g, 0, 0)),
                  pl.BlockSpec((Q, Q), lambda b, g, c: (0, 0)),
                  pl.BlockSpec((LANES, HD), lambda b, g, c: (0, 0)),
                  pl.BlockSpec((1, 1, nseg, gps * 8, SSD_HEAD_DIM, SSD_STATE), lambda b, g, c: (li, b, 0, g, 0, 0))],
        out_specs=[pl.BlockSpec((1, Q, gps * HD), lambda b, g, c: (b, c, g)),
                   pl.BlockSpec((1, nseg, gps * 8, SSD_HEAD_DIM, SSD_STATE), lambda b, g, c: (b, 0, g, 0, 0))],
        out_shape=[jax.ShapeDtypeStruct((NB, R, G * HD), F32), jax.ShapeDtypeStruct(state0.shape[1:], F32)],
        compiler_params=_cparams("parallel", "parallel", "arbitrary"),
        name="ssd_scan",
    )(xc, xc, xc, dt, a_log, tri, _head_expand_matrix(), state0)


def _ssd_post_kernel(y_ref, xh_ref, z_ref, d_ref, g_ref, w_ref, r_ref, o_ref):
    y = (y_ref[...] + xh_ref[...] * d_ref[...]) * _silu(z_ref[...])
    gw = y.shape[1] // SSD_GROUPS
    parts = []
    for gi in range(SSD_GROUPS):
        parts.append(_rms(y[:, gi * gw:(gi + 1) * gw], g_ref[:, gi * gw:(gi + 1) * gw]).astype(BF16))
    o_ref[...] = r_ref[...] + _dot(jnp.concatenate(parts, axis=1), w_ref[...])


def ssd_post(y, xc, proj, d_exp, norm_g, w_out, res, tm):
    T, DI = y.shape
    D = w_out.shape[1]
    return pl.pallas_call(
        _ssd_post_kernel,
        grid=(T // tm,),
        in_specs=[pl.BlockSpec((tm, DI), lambda i: (i, 0)),
                  pl.BlockSpec((tm, DI), lambda i: (i, 0)),
                  pl.BlockSpec((tm, DI), lambda i: (i, 0)),
                  pl.BlockSpec((1, DI), lambda i: (0, 0)),
                  pl.BlockSpec((1, DI), lambda i: (0, 0)),
                  pl.BlockSpec((DI, D), lambda i: (0, 0)),
                  pl.BlockSpec((tm, D), lambda i: (i, 0))],
        out_specs=pl.BlockSpec((tm, D), lambda i: (i, 0)),
        out_shape=jax.ShapeDtypeStruct((T, D), F32),
        compiler_params=_cparams("parallel"),
        name="ssd_post",
    )(y, xc, proj, d_exp, norm_g, w_out, res)


def _segment_tri(seg):
    r = np.arange(SSD_CHUNK)
    return jnp.asarray(((r[:, None] // seg == r[None, :] // seg) & (r[None, :] <= r[:, None])).astype(np.float32))


def ssd_layer(x, nb, conv_state, ssm_states, li, seg, w, tm_mm):
    T, D = x.shape
    L = T // nb
    proj = norm_matmul(x, w["norm"], w["w_in"], tm_mm, 1024)
    conv0 = jnp.pad(conv_state, ((0, 0), (SUBLANES - (SSD_CONV - 1), 0), (0, 0)))
    xc, dt, tail = ssd_pre(proj.reshape(nb, L, -1), conv0, w["conv_w"], w["conv_b"], w["dt_bias"],
                           min(L, SSD_CHUNK))
    rows = SSD_CHUNK if seg < SSD_CHUNK else L
    ngrp = T // rows
    nseg = SSD_CHUNK // seg
    st0 = ssm_states.reshape((ssm_states.shape[0], ngrp, nseg) + ssm_states.shape[2:])
    y, st = ssd_scan(xc.reshape(ngrp, rows, -1), dt.reshape(ngrp, rows, -1), w["a_log"], _segment_tri(seg), st0, li,
                     seg)
    x_new = ssd_post(y.reshape(T, -1), xc.reshape(T, -1), proj, w["d_exp"], w["norm_g"], w["w_out"], x, tm_mm)
    return x_new, tail[:, SUBLANES - (SSD_CONV - 1):], st.reshape(ssm_states.shape[1:])


def _bias_of(d, rb_ref, h):
    val = jnp.full(d.shape, rb_ref[0, h], F32)
    for k in range(1, N_BUCKETS):
        val = jnp.where(d >= BUCKET_THR[k], rb_ref[k, h], val)
    return jnp.where(d >= 0, val - rb_ref[N_BUCKETS - 1, h], 0.0)


def _bias_tables_kernel(rb_ref, tp_ref, ts_ref, *, past_len):
    r = lax.broadcasted_iota(jnp.int32, (Q_TILE, Q_TILE), 0)
    c = lax.broadcasted_iota(jnp.int32, (Q_TILE, Q_TILE), 1)
    t = lax.broadcasted_iota(jnp.int32, (SUBLANES, 1024), 0)
    j = lax.broadcasted_iota(jnp.int32, (SUBLANES, 1024), 1)
    win0 = past_len - WINDOW
    d_s = jnp.where(j < 256, past_len + t - (CMP_STRIDE * j + CMP_BLOCK - 1),
                    jnp.where(j < 384, past_len + t - (past_len - PAGE_SIZE + (j - 256)),
                              jnp.where(j < 512, t - (j - 384), past_len + t - (win0 + (j - 512)))))

    def body(h, carry):
        tp_ref[h, 0] = LOG2E * _bias_of(r - c, rb_ref, h)
        tp_ref[h, 1] = LOG2E * _bias_of(Q_TILE + r - c, rb_ref, h)
        tp_ref[h, 2] = LOG2E * _bias_of(r - CMP_STRIDE * (c - CMP_FRONT) - (CMP_BLOCK - 1), rb_ref, h)
        ts_ref[h] = _bias_of(d_s, rb_ref, h)
        return carry

    lax.fori_loop(0, ATT_HEADS, body, 0)


def bias_tables(rel_bias, past_len):
    return pl.pallas_call(
        functools.partial(_bias_tables_kernel, past_len=past_len),
        in_specs=[pl.BlockSpec(memory_space=pltpu.SMEM)],
        out_specs=[pl.BlockSpec(memory_space=pltpu.VMEM), pl.BlockSpec(memory_space=pltpu.VMEM)],
        out_shape=[jax.ShapeDtypeStruct((ATT_HEADS, 3, Q_TILE, Q_TILE), F32),
                   jax.ShapeDtypeStruct((ATT_HEADS, SUBLANES, 1024), F32)],
        name="bias_tables",
    )(rel_bias)


def _cmp_pre_kernel(*refs, nsrc, nsub, n_prefetch, n_native):
    refs = refs[n_prefetch:]
    srcs = refs[:nsrc]
    w_ref, o_ref, stage = refs[nsrc], refs[nsrc + 1], refs[nsrc + 2]
    rows = nsub * CMP_STRIDE
    n = nsrc * nsub
    for f in range(2):
        xs = []
        for pair in range(2):
            l0 = f * 256 + pair * LANES
            for si, src in enumerate(srcs):
                if si < n_native:
                    stage[si * rows:(si + 1) * rows] = src[0, 0, f, 2 * pair:2 * pair + 2].reshape(LANES, rows).T
                else:
                    stage[si * rows:(si + 1) * rows] = src[0, :, l0:l0 + LANES]
            cols = [stage[pl.ds(s, n, stride=CMP_STRIDE)] for s in range(CMP_STRIDE)]
            xs.append(jnp.concatenate(cols, axis=1))
        x2 = jnp.concatenate(xs, axis=0).astype(BF16)
        pre = _dot(x2, w_ref[f])
        for pair in range(2):
            o_ref[0, f, 2 * pair] = pre[pair * n:(pair + 1) * n, 0:256]
            o_ref[0, f, 2 * pair + 1] = pre[pair * n:(pair + 1) * n, 256:512]


def _cmp_fin_kernel(pre_ref, pe_ref, w1_ref, w2_ref, o_ref, scr, *, n):
    for f in range(2):
        cvec = _dot(pe_ref[f].astype(BF16), w1_ref[f])[0:1]
        for kv in range(ATT_KV):
            pre = pre_ref[0, f, kv]
            slot = f * ATT_KV + kv
            scr[slot, 0:n] = pre[:, CMP_HIDDEN:2 * CMP_HIDDEN]
            scr[slot, n:n + SUBLANES] = jnp.zeros((SUBLANES, CMP_HIDDEN), F32)
            hid = pre[:, 0:CMP_HIDDEN] + scr[slot, pl.ds(1, n)] + cvec
            o_ref[0, f, kv] = _dot(_silu(hid).astype(BF16), w2_ref[f])


def cmp_finish(pre, pe8, w1, w2d):
    B, _, KV, n, _ = pre.shape
    return pl.pallas_call(
        functools.partial(_cmp_fin_kernel, n=n),
        grid=(B,),
        in_specs=[pl.BlockSpec((1, 2, KV, n, 256), lambda b: (b, 0, 0, 0, 0)),
                  pl.BlockSpec(pe8.shape, lambda b: (0, 0, 0)),
                  pl.BlockSpec(w1.shape, lambda b: (0, 0, 0)),
                  pl.BlockSpec(w2d.shape, lambda b: (0, 0, 0))],
        out_specs=pl.BlockSpec((1, 2, KV, n, LANES), lambda b: (b, 0, 0, 0, 0)),
        out_shape=jax.ShapeDtypeStruct((B, 2, KV, n, LANES), F32),
        scratch_shapes=[pltpu.VMEM((2 * KV, n + SUBLANES, CMP_HIDDEN), F32)],
        compiler_params=_cparams("parallel"),
        name="cmp_finish",
    )(pre, pe8, w1, w2d)


def cmp_pre_prompt(proj, w_pair, rows):
    B, L, _ = proj.shape
    nsub = rows // CMP_STRIDE
    return pl.pallas_call(
        functools.partial(_cmp_pre_kernel, nsrc=1, nsub=nsub, n_prefetch=0, n_native=0),
        grid=(B, L // rows),
        in_specs=[pl.BlockSpec((1, rows, 512), lambda b, t: (b, t, 2)),
                  pl.BlockSpec(w_pair.shape, lambda b, t: (0, 0, 0))],
        out_specs=pl.BlockSpec((1, 2, ATT_KV, nsub, 256), lambda b, t: (b, 0, 0, t, 0)),
        out_shape=jax.ShapeDtypeStruct((B, 2, ATT_KV, L // CMP_STRIDE, 256), F32),
        scratch_shapes=[pltpu.VMEM((rows, LANES), F32)],
        compiler_params=_cparams("parallel", "parallel"),
        name="cmp_pre_prompt",
    )(proj, w_pair)


def cmp_pre_sample(page_table, cache_t, li, new_page, w_pair):
    B, n_pages = page_table.shape
    nsub = PAGE_SIZE // CMP_STRIDE
    nsrc = n_pages + 1

    def page_spec(p):
        return pl.BlockSpec((1, 1, 2, ATT_KV, ATT_HEAD_DIM, PAGE_SIZE), lambda b, pt: (li, pt[b, p], 0, 0, 0, 0))

    grid_spec = pltpu.PrefetchScalarGridSpec(
        num_scalar_prefetch=1,
        grid=(B,),
        in_specs=[page_spec(p) for p in range(n_pages)]
        + [pl.BlockSpec((1, PAGE_SIZE, 512), lambda b, pt: (b, 0, 0)),
           pl.BlockSpec(w_pair.shape, lambda b, pt: (0, 0, 0))],
        out_specs=pl.BlockSpec((1, 2, ATT_KV, nsrc * nsub, 256), lambda b, pt: (b, 0, 0, 0, 0)),
        scratch_shapes=[pltpu.VMEM((nsrc * PAGE_SIZE, LANES), F32)],
    )
    return pl.pallas_call(
        functools.partial(_cmp_pre_kernel, nsrc=nsrc, nsub=nsub, n_prefetch=1, n_native=n_pages),
        grid_spec=grid_spec,
        out_shape=jax.ShapeDtypeStruct((B, 2, ATT_KV, nsrc * nsub, 256), F32),
        compiler_params=_cparams("parallel"),
        name="cmp_pre_sample",
    )(page_table, *([cache_t] * n_pages), new_page, w_pair)


def _importance_matrix(n_rows, front):
    m = np.arange(n_rows)[:, None] - front
    j = np.arange(LANES)[None, :]
    ratio = SEL_BLOCK // CMP_STRIDE
    a = ((m >= ratio * j) & (m <= ratio * j + ratio - 1)).astype(np.float32) \
        + ((m >= ratio * j - 1) & (m <= ratio * j + ratio - 2)).astype(np.float32)
    return jnp.asarray(a)


def _softmax_rows(s):
    m = jnp.max(s, axis=-1, keepdims=True)
    m = jnp.where(m > NEG_INF, m, 0.0)
    e = jnp.exp(s - m)
    return e / jnp.maximum(jnp.sum(e, axis=-1, keepdims=True), 1e-30)


def _top_blocks(score, n_sel, axis=1):
    jb = lax.broadcasted_iota(jnp.int32, score.shape, axis).astype(F32)
    sel = jnp.zeros(score.shape, F32)
    sc = score
    for _ in range(n_sel):
        mx = jnp.max(sc, axis=axis, keepdims=True)
        idx = jnp.min(jnp.where(sc == mx, jb, 1e9), axis=axis, keepdims=True)
        pick = jb == idx
        sel = jnp.where(pick, 1.0, sel)
        sc = jnp.where(pick, NEG_INF, sc)
    return sel


def _tile4(x):
    return jnp.concatenate([x, x, x, x], axis=0)


def _attn_prompt_kernel(q_ref, gate_ref, kc_ref, vc_ref, ks_ref, vs_ref, kw_ref, vw_ref, tp_ref, amat_ref, o_ref,
                        mx_scr, acc_scr, *, nsub):
    i = pl.program_id(2)
    QT = Q_TILE
    R2 = 2 * QT
    low = lax.broadcasted_iota(jnp.int32, (QT, LANES), 1) < ATT_HEAD_DIM
    qpp = jnp.concatenate([q_ref[0, :, 0:LANES], q_ref[0, :, LANES:2 * LANES]], axis=0).astype(BF16)
    r1 = lax.broadcasted_iota(jnp.int32, (QT, 1), 0)
    c1 = lax.broadcasted_iota(jnp.int32, (1, LANES), 1)
    lower = r1 >= c1
    upper = r1 <= c1
    i_vec = jnp.full((QT, LANES), i, jnp.int32)

    def half_masks(n):
        lo = lax.broadcasted_iota(jnp.int32, (n, LANES), 1) < ATT_HEAD_DIM
        return jnp.where(lo, 1.0, 0.0).astype(BF16), jnp.where(lo, 0.0, 1.0).astype(BF16)

    def bd(x2):
        m_lo, m_hi = half_masks(x2.shape[0])
        return jnp.concatenate([x2 * m_lo, x2 * m_hi], axis=0)

    def spread(madd):
        m2 = jnp.concatenate([madd, madd], axis=1)
        return jnp.concatenate([m2, m2], axis=0)

    def bias_full(kind):
        return jnp.concatenate([jnp.concatenate([tp_ref[2 * p, kind], tp_ref[2 * p + 1, kind]], axis=1)
                                for p in range(2)], axis=0)

    def row_max2(x):
        parts = []
        for w in range(2):
            m = jnp.max(x[:, LANES * w:LANES * (w + 1)], axis=-1, keepdims=True)
            parts.append(jnp.broadcast_to(jnp.where(m > NEG_INF, m, 0.0), (R2, LANES)))
        return jnp.concatenate(parts, axis=1)

    def widen(m2, reps):
        if reps == 1:
            return m2
        return jnp.concatenate([m2[:, :LANES]] * reps + [m2[:, LANES:]] * reps, axis=1)

    def cols(ref, key0, n):
        return ref[0, :, pl.ds(pl.multiple_of(key0, LANES), n)]

    def bd_t(xt):
        z = jnp.zeros_like(xt)
        return jnp.concatenate([jnp.concatenate([xt, z], axis=0), jnp.concatenate([z, xt], axis=0)], axis=1)

    def scores(kt, add):
        return _dot(qpp, bd_t(kt)) + add

    def scores_nt(k2, add):
        return _dot_nt(qpp, bd(k2)) + add

    def weighted(s, m_wide, vt):
        n = vt.shape[1]
        same = jnp.right_shift(lax.broadcasted_iota(jnp.int32, (LANES, 2 * n), 0), ATT_HEAD_DIM.bit_length() - 1) \
            == (lax.broadcasted_iota(jnp.int32, (LANES, 2 * n), 1) >= n).astype(jnp.int32)
        rhs_t = jnp.concatenate([bd_t(vt), jnp.where(same, 1.0, 0.0).astype(BF16)], axis=0)
        return _dot_nt(jnp.exp2(s - m_wide).astype(BF16), rhs_t)

    def normalized(acc):
        return acc[:, :LANES] / jnp.maximum(acc[:, LANES:], 1e-30)

    nback = WINDOW // QT
    ws, wt = [], []
    for back in range(nback, -1, -1):
        tc = jnp.maximum(i - back, 0)
        ok = i_vec >= back
        if back == nback:
            ok = ok & upper
        if back == 0:
            ok = lower
        madd = spread(jnp.where(ok, 0.0, NEG_INF))
        ws.append(scores(cols(kw_ref, tc * QT, QT), madd + bias_full(back) if back <= 1 else madd))
        wt.append(tc)
    mel = ws[0]
    for s in ws[1:]:
        mel = jnp.maximum(mel, s)
    m2_w = row_max2(mel)
    acc_w = None
    for s, tc in zip(ws, wt):
        term = weighted(s, m2_w, cols(vw_ref, tc * QT, QT))
        acc_w = term if acc_w is None else acc_w + term
    o_w = normalized(acc_w)

    st = pl.multiple_of(i * SUBLANES, SUBLANES)
    d_cmp = r1 - CMP_STRIDE * (c1 - CMP_FRONT) - (CMP_BLOCK - 1)
    ctiles = []
    for j in range(nsub // QT):
        rows = slice(CMP_FRONT + j * QT, CMP_FRONT + (j + 1) * QT)
        ok = jnp.broadcast_to(j * QT + c1 < i * SUBLANES - CMP_FRONT, (QT, LANES))
        ctiles.append((kc_ref[0, 0, 0, rows, :], vc_ref[0, 0, 0, rows, :], amat_ref[rows, :], ok, False))
    ctiles.append((kc_ref[0, 0, 0, pl.ds(st, QT), :], vc_ref[0, 0, 0, pl.ds(st, QT), :], amat_ref[pl.ds(st, QT), :],
                   (d_cmp >= 0) & (c1 >= CMP_FRONT - i * SUBLANES), True))
    ss = []
    for kt, _, _, ok, diag in ctiles:
        add = spread(jnp.where(ok, 0.0, NEG_INF))
        ss.append(scores_nt(kt.astype(BF16), add + bias_full(2) if diag else add))
    mel = ss[0]
    for s in ss[1:]:
        mel = jnp.maximum(mel, s)
    m2 = row_max2(mel)
    es = [jnp.exp2(s - m2) for s in ss]
    lel = es[0]
    for e in es[1:]:
        lel = lel + e
    inv = []
    for w in range(2):
        l = jnp.sum(lel[:, LANES * w:LANES * (w + 1)], axis=-1, keepdims=True)
        inv.append(jnp.broadcast_to(1.0 / jnp.maximum(l, 1e-30), (R2, LANES)))
    inv2 = jnp.concatenate(inv, axis=1)
    o_c = None
    phs = []
    for e, (_, vt, _, _, _) in zip(es, ctiles):
        pn = e * inv2
        term = _dot(pn.astype(BF16), bd(vt.astype(BF16)))
        o_c = term if o_c is None else o_c + term
        phs.append(pn[:QT, :LANES] + pn[:QT, LANES:] + pn[QT:, :LANES] + pn[QT:, LANES:])
    imp = _dot_split3(jnp.concatenate(phs, axis=1),
                      jnp.concatenate([t[2] for t in ctiles], axis=0).astype(BF16))

    qblk = 2 * i + (c1 >= SEL_BLOCK).astype(jnp.int32)
    lag = qblk - r1
    allowed = lag >= 0
    forced = (r1 == 0) | (allowed & (lag < N_LOCAL))
    score_t = jnp.where(allowed, imp.T + jnp.where(forced, FORCE_BONUS, 0.0), -1.0)
    sel = _top_blocks(score_t, N_SEL, axis=0).T.astype(BF16)

    FK = FAR_KEYS
    tm1 = jnp.maximum(i - 1, 0)
    limit = tm1 * QT
    n_grp = jnp.right_shift(tm1 + FK // QT - 1, (FK // QT).bit_length() - 1)
    shift = SEL_BLOCK.bit_length() - 1

    def key_mask(key0, n):
        jb = lax.broadcasted_iota(jnp.int32, (LANES, n), 0)
        key = key0 + lax.broadcasted_iota(jnp.int32, (LANES, n), 1)
        hit = jb == jnp.right_shift(key, shift)
        if n == FK:
            hit = hit & (key < limit)
        return jnp.where(_dot(sel, jnp.where(hit, 1.0, 0.0).astype(BF16)) > 0.5, 0.0, NEG_INF)

    low2 = lax.broadcasted_iota(jnp.int32, (R2, LANES), 1) < ATT_HEAD_DIM

    def online_step(s, vt):
        n = vt.shape[1]
        reps = n // LANES
        halves = []
        for w in range(2):
            m = s[:, w * n:w * n + LANES]
            for c4 in range(1, reps):
                m = jnp.maximum(m, s[:, w * n + c4 * LANES:w * n + (c4 + 1) * LANES])
            halves.append(jnp.broadcast_to(jnp.max(m, axis=-1, keepdims=True), (R2, LANES)))
        m_old = mx_scr[...]
        m_new = jnp.maximum(m_old, jnp.concatenate(halves, axis=1))
        m_safe = jnp.where(m_new > NEG_INF, m_new, 0.0)
        alpha = jnp.exp2(m_old - m_safe)
        a_mix = jnp.where(low2, alpha[:, :LANES], alpha[:, LANES:])
        mx_scr[...] = m_new
        acc_scr[...] = acc_scr[...] * jnp.concatenate([a_mix, a_mix], axis=1) + weighted(s, widen(m_safe, reps), vt)

    mx_scr[...] = jnp.full(mx_scr.shape, NEG_INF, F32)
    acc_scr[...] = jnp.zeros_like(acc_scr)

    def far_body(gi, c):
        s = scores(cols(ks_ref, gi * FK, FK), spread(key_mask(gi * FK, FK)))
        online_step(s, cols(vs_ref, gi * FK, FK))
        return c

    lax.fori_loop(0, n_grp, far_body, 0)
    mk_s = spread(jnp.where(i_vec >= 1, key_mask(tm1 * QT, QT), NEG_INF))
    online_step(scores(cols(ks_ref, tm1 * QT, QT), mk_s + bias_full(1)), cols(vs_ref, tm1 * QT, QT))
    mk_d = spread(jnp.where(lower, key_mask(i * QT, QT), NEG_INF))
    online_step(scores(cols(ks_ref, i * QT, QT), mk_d + bias_full(0)), cols(vs_ref, i * QT, QT))
    o_s = normalized(acc_scr[...])

    g = _sigmoid(gate_ref[0])

    def gate(br):
        tiles = []
        for p in range(2):
            c0 = br * ATT_HG + 2 * p
            tiles.append(jnp.where(low, jnp.broadcast_to(g[:, c0:c0 + 1], (QT, LANES)),
                                   jnp.broadcast_to(g[:, c0 + 1:c0 + 2], (QT, LANES))))
        return jnp.concatenate(tiles, axis=0)

    o = gate(0) * o_c + gate(1) * o_s + gate(2) * o_w
    o_ref[0] = jnp.concatenate([o[:QT], o[QT:]], axis=1)


def _nsa_in_proj_kernel(x_ref, g_ref, wr_ref, wt_ref, row_ref, kvt_ref, kvt16_ref):
    h = _rms(x_ref[0], g_ref[...]).astype(BF16)
    row_ref[0] = _dot(h, wr_ref[...])
    kv = _dot_nt(wt_ref[...], h)
    kvt_ref[0] = kv
    kvt16_ref[0] = kv[kv.shape[0] - kvt16_ref.shape[1]:].astype(BF16)


def nsa_in_proj_prompt(x, g, w_row, w_kvt, tm):
    B, L, D = x.shape
    NR, NT = w_row.shape[1], w_kvt.shape[0]
    n16 = 4 * ATT_KV * ATT_HEAD_DIM
    return pl.pallas_call(
        _nsa_in_proj_kernel,
        grid=(B, L // tm),
        in_specs=[pl.BlockSpec((1, tm, D), lambda b, l: (b, l, 0)),
                  pl.BlockSpec((1, D), lambda b, l: (0, 0)),
                  pl.BlockSpec((D, NR), lambda b, l: (0, 0)),
                  pl.BlockSpec((NT, D), lambda b, l: (0, 0))],
        out_specs=[pl.BlockSpec((1, tm, NR), lambda b, l: (b, l, 0)),
                   pl.BlockSpec((1, NT, tm), lambda b, l: (b, 0, l)),
                   pl.BlockSpec((1, n16, tm), lambda b, l: (b, 0, l))],
        out_shape=[jax.ShapeDtypeStruct((B, L, NR), F32), jax.ShapeDtypeStruct((B, NT, L), F32),
                   jax.ShapeDtypeStruct((B, n16, L), BF16)],
        compiler_params=_cparams("parallel", "parallel"),
        name="nsa_in_proj_prompt",
    )(x, g, w_row, w_kvt)


def attn_prompt(rowp, cmp_kv, kvt16, tp, amat):
    B, L, _ = rowp.shape
    nsub = L // CMP_STRIDE
    ncp = cmp_kv.shape[3]

    def kv_spec(f):
        return pl.BlockSpec((1, ATT_HEAD_DIM, L), lambda b, k, i: (b, f * ATT_KV + k, 0))

    def cmp_spec(f):
        return pl.BlockSpec((1, 1, 1, ncp, LANES), lambda b, k, i: (b, f, k, 0, 0))

    return pl.pallas_call(
        functools.partial(_attn_prompt_kernel, nsub=nsub),
        grid=(B, ATT_KV, L // Q_TILE),
        in_specs=[pl.BlockSpec((1, Q_TILE, 256), lambda b, k, i: (b, i, k)),
                  pl.BlockSpec((1, Q_TILE, LANES), lambda b, k, i: (b, i, 12 + k)),
                  cmp_spec(0), cmp_spec(1), kv_spec(0), kv_spec(1), kv_spec(2), kv_spec(3),
                  pl.BlockSpec((ATT_HG, 3, Q_TILE, Q_TILE), lambda b, k, i: (k, 0, 0, 0)),
                  pl.BlockSpec(amat.shape, lambda b, k, i: (0, 0))],
        out_specs=pl.BlockSpec((1, Q_TILE, 256), lambda b, k, i: (b, i, k)),
        out_shape=jax.ShapeDtypeStruct((B, L, ATT_HEADS * ATT_HEAD_DIM), F32),
        scratch_shapes=[pltpu.VMEM((2 * Q_TILE, 2 * LANES), F32),
                        pltpu.VMEM((2 * Q_TILE, 2 * LANES), F32)],
        compiler_params=_cparams("parallel", "parallel", "arbitrary"),
        name="attn_prompt",
    )(rowp, rowp, cmp_kv, cmp_kv, kvt16, kvt16, kvt16, kvt16, tp, amat)


def _attn_sample_kernel(*refs, n_pages, past_len):
    pages = refs[1:1 + n_pages]
    q_ref, gate_ref, new_s_ref, new_w_ref, win_ref, cmp_ref, ts_ref, amat_ref, o_ref = refs[1 + n_pages:]
    T = SUBLANES
    R = ATT_HG * T
    n_cmp_rows = cmp_ref.shape[3]
    lane_t = lax.broadcasted_iota(jnp.int32, (T, LANES), 1)
    low_t = lane_t < ATT_HEAD_DIM
    t_r = lax.broadcasted_iota(jnp.int32, (R, 1), 0) & (T - 1)
    c1 = lax.broadcasted_iota(jnp.int32, (1, LANES), 1)
    g = _sigmoid(gate_ref[0])
    zeros_new = jnp.zeros((LANES - T, LANES), F32)
    new_ok = (c1 <= t_r) & (c1 < T)
    n_c = lax.broadcasted_iota(jnp.int32, (1, 256), 1)
    d_c = past_len + t_r - (CMP_STRIDE * n_c + CMP_BLOCK - 1)
    c_w = lax.broadcasted_iota(jnp.int32, (1, WINDOW), 1)
    cmp_pad = jnp.zeros((256 - n_cmp_rows, LANES), F32)

    def lane_tile(kv):
        return LANES * (kv // 2)

    def queries(kv):
        par = kv % 2
        keep = (lane_t >= ATT_HEAD_DIM) if par else low_t
        qrows = []
        for hg in range(ATT_HG):
            h = kv * ATT_HG + hg
            q2 = q_ref[0, :, LANES * (h // 2):LANES * (h // 2 + 1)]
            if h % 2 != par:
                q2 = pltpu.roll(q2, ATT_HEAD_DIM, axis=1)
            qrows.append(jnp.where(keep, q2, 0.0))
        return jnp.concatenate(qrows, axis=0).astype(BF16)

    def new_rows(ref, l0):
        return jnp.concatenate([ref[0, :, l0:l0 + LANES], zeros_new], axis=0).astype(BF16)

    qs_all = [queries(kv) for kv in range(ATT_KV)]
    tabs = [ts_ref[kv * ATT_HG:(kv + 1) * ATT_HG].reshape(R, 1024) for kv in range(ATT_KV)]

    o_cs, p_sums = [], []
    for kv in range(ATT_KV):
        kc = jnp.concatenate([cmp_ref[0, 0, kv], cmp_pad], axis=0).astype(BF16)
        vc = jnp.concatenate([cmp_ref[0, 1, kv], cmp_pad], axis=0).astype(BF16)
        s_c = _dot_nt(qs_all[kv], kc) + tabs[kv][:, 0:256]
        p_c = _softmax_rows(jnp.where(d_c >= 0, s_c, NEG_INF))
        o_cs.append(_dot(p_c.astype(BF16), vc))
        p_sums.append(p_c[0:T] + p_c[T:2 * T] + p_c[2 * T:3 * T] + p_c[3 * T:4 * T])

    p_all = jnp.concatenate(p_sums + [jnp.zeros((LANES - ATT_KV * T, 256), F32)], axis=0)
    imp_t = _dot_split3(p_all, amat_ref[...].astype(BF16)).T
    rb = lax.broadcasted_iota(jnp.int32, (LANES, 1), 0)
    qblk = jnp.right_shift(past_len + (c1 & (T - 1)), SEL_BLOCK.bit_length() - 1)
    lag = qblk - rb
    allowed = lag >= 0
    forced = (rb == 0) | (allowed & (lag < N_LOCAL))
    score_t = jnp.where(allowed, imp_t + jnp.where(forced, FORCE_BONUS, 0.0), -1.0)
    sel_all = _top_blocks(score_t, N_SEL, axis=0).T

    out_tiles = []
    for kv in range(ATT_KV):
        par = kv % 2
        pl0 = lane_tile(kv)
        qs, tab = qs_all[kv], tabs[kv]
        sel = sel_all[kv * T:(kv + 1) * T]
        o_c = o_cs[kv]

        nb = 2 * n_pages
        n_past = n_pages * PAGE_SIZE
        pair = slice(2 * (kv // 2), 2 * (kv // 2) + 2)
        k_t = jnp.concatenate([pages[p][0, 0, 0, pair].reshape(LANES, PAGE_SIZE).astype(BF16)
                               for p in range(n_pages)], axis=1)
        v_t = jnp.concatenate([pages[p][0, 0, 1, pair].reshape(LANES, PAGE_SIZE).astype(BF16)
                               for p in range(n_pages)], axis=1)
        m_parts = [jnp.where(c1 < SEL_BLOCK, sel[:, 2 * p:2 * p + 1], sel[:, 2 * p + 1:2 * p + 2])
                   for p in range(n_pages)]
        m_parts.append(jnp.where(new_ok[0:T], jnp.broadcast_to(sel[:, nb:nb + 1], (T, LANES)), 0.0))
        zero = jnp.zeros((R, LANES), F32)
        bias_s = jnp.concatenate([zero] * (n_pages - 1) + [tab[:, 256:384], tab[:, 384:512]], axis=1)
        s_raw = jnp.concatenate([_dot(qs, k_t), _dot_nt(qs, new_rows(new_s_ref, pl0))], axis=1)
        s_all = jnp.where(_tile4(jnp.concatenate(m_parts, axis=1)) > 0.5, s_raw + bias_s, NEG_INF)
        p_s = _softmax_rows(s_all).astype(BF16)
        o_s = _dot_nt(p_s[:, :n_past], v_t) + _dot(p_s[:, n_past:], new_rows(new_s_ref, 256 + pl0))

        kw_t = win_ref[0, 0, 0, pair].reshape(LANES, WINDOW).astype(BF16)
        vw_t = win_ref[0, 0, 1, pair].reshape(LANES, WINDOW).astype(BF16)
        bias_w = jnp.concatenate([tab[:, 512:1024] + jnp.where(c_w >= t_r, 0.0, NEG_INF),
                                  tab[:, 384:512] + jnp.where(new_ok, 0.0, NEG_INF)], axis=1)
        s_w = jnp.concatenate([_dot(qs, kw_t), _dot_nt(qs, new_rows(new_w_ref, pl0))], axis=1) + bias_w
        p_w = _softmax_rows(s_w).astype(BF16)
        o_w = _dot_nt(p_w[:, :WINDOW], vw_t) + _dot(p_w[:, WINDOW:], new_rows(new_w_ref, 256 + pl0))

        gk = g[:, LANES * kv:LANES * (kv + 1)]
        for pair in range(2):
            halves = []
            for which in range(2):
                hg = 2 * pair + which
                rows = slice(hg * T, (hg + 1) * T)
                o = gk[:, hg:hg + 1] * o_c[rows] + gk[:, 4 + hg:5 + hg] * o_s[rows] + gk[:, 8 + hg:9 + hg] * o_w[rows]
                if which != par:
                    o = pltpu.roll(o, ATT_HEAD_DIM, axis=1)
                halves.append(o)
            out_tiles.append(jnp.where(low_t, halves[0], halves[1]))
    o_ref[0] = jnp.concatenate(out_tiles, axis=1)


def attn_sample(page_table, cache_t, win_t, li, proj, cmp_kv, ts, amat, past_len):
    B, n_pages = page_table.shape
    T = proj.shape[1]

    def page_spec(p):
        return pl.BlockSpec((1, 1, 2, ATT_KV, ATT_HEAD_DIM, PAGE_SIZE), lambda b, pt: (li, pt[b, p], 1, 0, 0, 0))

    grid_spec = pltpu.PrefetchScalarGridSpec(
        num_scalar_prefetch=1,
        grid=(B,),
        in_specs=[page_spec(p) for p in range(n_pages)]
        + [pl.BlockSpec((1, T, 1024), lambda b, pt: (b, 0, 0)),
           pl.BlockSpec((1, T, 512), lambda b, pt: (b, 0, 5)),
           pl.BlockSpec((1, T, 512), lambda b, pt: (b, 0, 3)),
           pl.BlockSpec((1, T, 512), lambda b, pt: (b, 0, 4)),
           pl.BlockSpec((1, 1) + win_t.shape[2:], lambda b, pt: (li, b, 0, 0, 0, 0)),
           pl.BlockSpec((1,) + cmp_kv.shape[1:], lambda b, pt: (b, 0, 0, 0, 0)),
           pl.BlockSpec(ts.shape, lambda b, pt: (0, 0, 0)),
           pl.BlockSpec(amat.shape, lambda b, pt: (0, 0))],
        out_specs=pl.BlockSpec((1, T, 1024), lambda b, pt: (b, 0, 0)),
    )
    return pl.pallas_call(
        functools.partial(_attn_sample_kernel, n_pages=n_pages, past_len=past_len),
        grid_spec=grid_spec,
        out_shape=jax.ShapeDtypeStruct((B, T, ATT_HEADS * ATT_HEAD_DIM), F32),
        compiler_params=_cparams("parallel"),
        name="attn_sample",
    )(page_table, *([cache_t] * n_pages), proj, proj, proj, proj, win_t, cmp_kv, ts, amat)


def _ssd_weights(i, li, norm_mix, ssd_w_in, ssd_conv_w, ssd_conv_b, ssd_dt_bias, ssd_a_log, ssd_d, ssd_norm, ssd_w_out):
    d_inner = ssd_w_out.shape[1]
    conv_dim = ssd_conv_w.shape[2]
    heads = ssd_dt_bias.shape[1]
    hpg = heads // SSD_GROUPS
    w = ssd_w_in[li]

    def per_group(v):
        v = v.reshape(v.shape[:-1] + (SSD_GROUPS, hpg))
        return jnp.pad(v, [(0, 0)] * (v.ndim - 1) + [(0, LANES - hpg)]).reshape(v.shape[:-2] + (SSD_GROUPS * LANES,))

    w_dt = per_group(w[:, d_inner + conv_dim:])
    w_in = jnp.concatenate([w[:, :d_inner], w_dt, jnp.zeros_like(w_dt), w[:, d_inner:d_inner + conv_dim]], axis=1)
    return dict(
        norm=norm_mix[i][None],
        w_in=w_in.astype(BF16),
        conv_w=ssd_conv_w[li], conv_b=ssd_conv_b[li][None],
        dt_bias=per_group(ssd_dt_bias[li])[None],
        a_log=per_group(ssd_a_log[li]).reshape(SSD_GROUPS, 1, LANES),
        d_exp=jnp.repeat(ssd_d[li], SSD_HEAD_DIM)[None],
        norm_g=ssd_norm[li][None],
        w_out=ssd_w_out[li].astype(BF16),
    )


def _nsa_in_weight(w, q_scale):
    q_dim = ATT_HEADS * ATT_HEAD_DIM
    kv_dim = 6 * ATT_KV * ATT_HEAD_DIM
    idx = np.zeros((ATT_KV, LANES), np.int32)
    ok = np.zeros((ATT_KV, LANES), bool)
    for kv in range(ATT_KV):
        for br in range(3):
            for hg in range(ATT_HG):
                idx[kv, br * ATT_HG + hg] = q_dim + kv_dim + (kv * ATT_HG + hg) * 3 + br
                ok[kv, br * ATT_HG + hg] = True
    w_g = jnp.where(jnp.asarray(ok.reshape(-1))[None, :], w[:, idx.reshape(-1)], 0.0)
    return jnp.concatenate([w[:, :q_dim] * q_scale, w[:, q_dim:q_dim + kv_dim], w_g], axis=1).astype(BF16)


def _prompt_kv_layouts(kv6):
    B, L = kv6.shape[:2]
    k = jnp.transpose(kv6[:, :, 2::2].astype(BF16), (0, 2, 3, 4, 1))
    k = jnp.concatenate([k, k], axis=3).reshape(B, 2, ATT_KV, LANES, L // Q_TILE, Q_TILE)
    v = jnp.transpose(kv6[:, :, 3::2].astype(BF16), (0, 2, 3, 1, 4))
    return jnp.transpose(k, (0, 1, 2, 4, 3, 5)), jnp.concatenate([v, v], axis=-1)


def _cmp_pair_weight(w1):
    w1r = w1.reshape(2, 2, CMP_STRIDE, ATT_HEAD_DIM, CMP_HIDDEN)
    eye = jnp.eye(2, dtype=w1.dtype)
    wp = jnp.einsum("fjsde,wv->fswdvje", w1r, eye)
    return wp.reshape(2, CMP_STRIDE * 2 * ATT_HEAD_DIM, 2 * 2 * CMP_HIDDEN).astype(BF16)


def kernel(x_prompt, x_sample, cache_nsa_kv, state_nsa_win, state_ssm, state_conv, state_pool, page_table, rel_bias,
           norm_mix, norm_ffn, norm_out, ffn_w_up, ffn_w_down, ssd_w_in, ssd_conv_w, ssd_conv_b, ssd_dt_bias,
           ssd_a_log, ssd_d, ssd_norm, ssd_w_out, pool_w, pool_scale, nsa_w_in, nsa_cmp_pe, nsa_cmp_w1, nsa_cmp_w2,
           nsa_w_out):
    bp, lp, d_model = x_prompt.shape
    bs, ls, _ = x_sample.shape
    depth = norm_mix.shape[0]
    n_pages = page_table.shape[1]
    past_len = n_pages * PAGE_SIZE
    assert ls == SUBLANES and lp % (16 * Q_TILE) == 0 and past_len >= WINDOW and state_nsa_win.shape[2] == WINDOW
    xp = x_prompt.reshape(bp * lp, d_model)
    xs = x_sample.reshape(bs * ls, d_model)
    tm_p, tm_s = 512, 512
    outs = {k: [] for k in ("kv_p", "kv_s", "win_p", "win_s", "ssm_p", "ssm_s", "conv_p", "conv_s", "pool_p", "pool_s")}
    for i in range(depth):
        kind, li = i % 3, i // 3
        if kind == 0:
            w = _ssd_weights(i, li, norm_mix, ssd_w_in, ssd_conv_w, ssd_conv_b, ssd_dt_bias, ssd_a_log, ssd_d,
                             ssd_norm, ssd_w_out)
            conv_dim = ssd_conv_w.shape[2]
            xp, c_p, s_p = ssd_layer(xp, bp, jnp.zeros((bp, SSD_CONV - 1, conv_dim), F32),
                                     jnp.zeros((1, bp) + state_ssm.shape[2:], F32), 0, SSD_CHUNK, w, tm_p)
            xs, c_s, s_s = ssd_layer(xs, bs, state_conv[li], state_ssm, li, ls, w, tm_s)
            outs["conv_p"].append(c_p)
            outs["conv_s"].append(c_s)
            outs["ssm_p"].append(s_p)
            outs["ssm_s"].append(s_s)
        elif kind == 1:
            g = norm_mix[i][None]
            pw = pool_w[li].astype(BF16)
            sc = pool_scale[li][None]
            xp3, tail_p = pool_mixer_residual(xp.reshape(bp, lp, d_model), jnp.zeros((bp, POOL_HALO, d_model), F32),
                                              g, pw, sc, 0, 512)
            halo_s = jnp.pad(state_pool[li], ((0, 0), (1, 0), (0, 0)))
            xs3, tail_s = pool_mixer_residual(xs.reshape(bs, ls, d_model), halo_s, g, pw, sc, past_len, ls)
            xp, xs = xp3.reshape(bp * lp, d_model), xs3.reshape(bs * ls, d_model)
            outs["pool_p"].append(tail_p[:, 1:])
            outs["pool_s"].append(tail_s[:, 1:])
        else:
            g = norm_mix[i][None]
            w_in = _nsa_in_weight(nsa_w_in[li], ATT_HEAD_DIM ** -0.5)
            w_in2 = _nsa_in_weight(nsa_w_in[li], ATT_HEAD_DIM ** -0.5 * LOG2E)
            w_pair = _cmp_pair_weight(nsa_cmp_w1[li])
            pe8 = jnp.broadcast_to(nsa_cmp_pe[li].reshape(2, 1, -1), (2, SUBLANES, CMP_BLOCK * ATT_HEAD_DIM))
            w1 = nsa_cmp_w1[li].astype(BF16)
            w2d = jnp.concatenate([nsa_cmp_w2[li], nsa_cmp_w2[li]], axis=-1).astype(BF16)
            tp, ts = bias_tables(rel_bias, past_len)
            w_out = nsa_w_out[li].astype(BF16)
            kvw = ATT_KV * ATT_HEAD_DIM
            cache_t = jnp.transpose(cache_nsa_kv, (0, 1, 3, 4, 5, 2))
            win_t = jnp.transpose(state_nsa_win, (0, 1, 3, 4, 5, 2))
            w_row = jnp.concatenate([w_in2[:, :1024 + 2 * kvw], w_in2[:, 1024 + 6 * kvw:]], axis=1)
            w_kvt = w_in[:, 1024:1024 + 6 * kvw].T
            rowp, kvt, kvt16 = nsa_in_proj_prompt(xp.reshape(bp, lp, d_model), g, w_row, w_kvt, tm_p)
            pre_p = cmp_pre_prompt(rowp, w_pair, 2048)
            cmp_p = jnp.pad(cmp_finish(pre_p, pe8, w1, w2d), ((0, 0), (0, 0), (0, 0), (CMP_FRONT, CMP_BACK), (0, 0)))
            amat_p = _importance_matrix(lp // CMP_STRIDE + CMP_FRONT + CMP_BACK, CMP_FRONT)
            o_p = attn_prompt(rowp, cmp_p, kvt16, tp, amat_p)
            xp = matmul_residual(o_p.reshape(bp * lp, -1), w_out, xp, tm_p)
            kvt6 = kvt.reshape(bp, 6, ATT_KV, ATT_HEAD_DIM, lp)
            outs["kv_p"].append(jnp.transpose(kvt6[:, 0:4], (0, 4, 1, 2, 3)))
            outs["win_p"].append(jnp.transpose(kvt6[:, 4:6, :, :, lp - WINDOW:], (0, 4, 1, 2, 3)))
            proj_s = norm_matmul(xs, g, w_in, tm_s, 1024).reshape(bs, ls, -1)
            new_page = jnp.pad(proj_s[:, :, 1024:1024 + 2 * kvw], ((0, 0), (0, PAGE_SIZE - ls), (0, 0)))
            pre_s = cmp_pre_sample(page_table, cache_t, li, new_page, w_pair)
            cmp_s = cmp_finish(pre_s, pe8, w1, w2d)
            amat_s = _importance_matrix(256, 0)
            o_s = attn_sample(page_table, cache_t, win_t, li, proj_s, cmp_s, ts, amat_s, past_len)
            xs = matmul_residual(o_s.reshape(bs * ls, -1), w_out, xs, tm_s)
            kv6s = proj_s[:, :, 1024:2560].reshape(bs, ls, 6, ATT_KV, ATT_HEAD_DIM)
            outs["kv_s"].append(kv6s[:, :, 0:4])
            new_win_t = jnp.concatenate([win_t[li][..., ls:], jnp.transpose(kv6s[:, :, 4:6], (0, 2, 3, 4, 1))], axis=-1)
            outs["win_s"].append(jnp.transpose(new_win_t, (0, 4, 1, 2, 3)))
        last = i == depth - 1
        xp = sqrelu_mlp_residual(xp, norm_ffn[i][None], ffn_w_up[i].astype(BF16), ffn_w_down[i].astype(BF16),
                                 norm_out[None], last, 1024, 1024)
        xs = sqrelu_mlp_residual(xs, norm_ffn[i][None], ffn_w_up[i].astype(BF16), ffn_w_down[i].astype(BF16),
                                 norm_out[None], last, tm_s, 512)
    st = lambda k: jnp.stack(outs[k])
    return (xp.reshape(bp, lp, d_model), xs.reshape(bs, ls, d_model), st("kv_p"), st("kv_s"), st("win_p"),
            st("win_s"), st("ssm_p"), st("ssm_s"), st("conv_p"), st("conv_s"), st("pool_p"), st("pool_s"))
```

```python
import functools
import math

import numpy as np
import jax
import jax.numpy as jnp
from jax import lax
from jax.experimental import pallas as pl
from jax.experimental.pallas import tpu as pltpu

F32 = jnp.float32
BF16 = jnp.bfloat16
HIGHEST = lax.Precision.HIGHEST
EPS = 1e-6
NEG_INF = float("-inf")
LOG2E = math.log2(math.e)

V7X_VMEM_LIMIT_BYTES = 56 * 1024 * 1024
LANES = 128
SUBLANES = 8

D_MODEL = 1024
SSD_HEAD_DIM = 64
SSD_GROUPS = 4
SSD_STATE = 128
SSD_CONV = 4
SSD_CHUNK = 128
POOL_WINDOWS = (2, 4, 8, 16)
POOL_HALO = 16
ATT_HEADS = 16
ATT_HEAD_DIM = 64
ATT_KV = 4
ATT_HG = 4
CMP_BLOCK = 32
CMP_STRIDE = 16
CMP_HIDDEN = 128
SEL_BLOCK = 64
N_SEL = 8
N_LOCAL = 2
FORCE_BONUS = 1000.0
WINDOW = 512
Q_TILE = 128
PAGE_SIZE = 128
N_BUCKETS = 32
MAX_DISTANCE = 128
CMP_FRONT = 112
CMP_BACK = 16
FAR_KEYS = 1024

MM_ROWS = 1024
MM_COLS = (2048, 1536, 1024, 512)
MLP_FF = 1024
WIDE_ROWS = 512
POOL_ROWS = 512
CMP_ROWS = 2048
SAMPLE_SEQS_PER_STEP = 1


def _cparams(*sem):
    return pltpu.CompilerParams(dimension_semantics=sem, vmem_limit_bytes=V7X_VMEM_LIMIT_BYTES)


def _bucket_thresholds():
    d = np.arange(0, MAX_DISTANCE + 1)
    max_exact = N_BUCKETS // 2
    nf = np.maximum(d, 1).astype(np.float32)
    large = max_exact + (np.log(nf / np.float32(max_exact)) / np.float32(math.log(MAX_DISTANCE / max_exact))
                         * np.float32(N_BUCKETS - max_exact)).astype(np.int32)
    large = np.minimum(large, N_BUCKETS - 1)
    b = np.where(d < max_exact, d, large)
    return [int(np.argmax(b >= k)) for k in range(N_BUCKETS)]


BUCKET_THR = _bucket_thresholds()


def _rms(x, g):
    return x * lax.rsqrt(jnp.mean(x * x, axis=-1, keepdims=True) + EPS) * g


def _sigmoid(x):
    return 0.5 * jnp.tanh(0.5 * x) + 0.5


def _silu(x):
    return x * _sigmoid(x)


def _softplus(x):
    return jnp.maximum(x, 0.0) + jnp.log1p(jnp.exp(-jnp.abs(x)))


def _dot(a, b):
    return jnp.dot(a, b, preferred_element_type=F32)


def _dot_nt(a, b):
    return lax.dot_general(a, b, (((1,), (1,)), ((), ())), preferred_element_type=F32)


def _dot_exact(a, b):
    return jnp.dot(a, b, precision=HIGHEST, preferred_element_type=F32)


def _dot_split3(a, b_bf16):
    hi = a.astype(BF16)
    r1 = a - hi.astype(F32)
    mid = r1.astype(BF16)
    lo = (r1 - mid.astype(F32)).astype(BF16)
    return _dot(hi, b_bf16) + _dot(mid, b_bf16) + _dot(lo, b_bf16)


def _norm_mm_kernel(x_ref, g_ref, w_ref, o_ref, h_scr):
    @pl.when(pl.program_id(1) == 0)
    def _():
        h_scr[...] = _rms(x_ref[...], g_ref[...]).astype(BF16)

    o_ref[...] = _dot(h_scr[...], w_ref[...])


def norm_matmul(x, g, w):
    T, D = x.shape
    N = w.shape[1]
    tm = min(T, MM_ROWS)
    tn = next(c for c in MM_COLS if N % c == 0)
    return pl.pallas_call(
        _norm_mm_kernel,
        grid=(T // tm, N // tn),
        in_specs=[pl.BlockSpec((tm, D), lambda i, j: (i, 0)),
                  pl.BlockSpec((1, D), lambda i, j: (0, 0)),
                  pl.BlockSpec((D, tn), lambda i, j: (0, j))],
        out_specs=pl.BlockSpec((tm, tn), lambda i, j: (i, j)),
        out_shape=jax.ShapeDtypeStruct((T, N), F32),
        scratch_shapes=[pltpu.VMEM((tm, D), BF16)],
        compiler_params=_cparams("parallel", "arbitrary"),
        name="norm_matmul",
    )(x, g, w)


def _mlp_kernel(x_ref, g_ref, wu_ref, wd_ref, go_ref, o_ref, h_scr, acc, *, final_norm):
    j = pl.program_id(1)

    @pl.when(j == 0)
    def _():
        h_scr[...] = _rms(x_ref[...], g_ref[...]).astype(BF16)
        acc[...] = jnp.zeros_like(acc)

    a = jnp.maximum(_dot(h_scr[...], wu_ref[...]), 0.0)
    acc[...] += _dot((a * a).astype(BF16), wd_ref[...])

    @pl.when(j == pl.num_programs(1) - 1)
    def _():
        y = x_ref[...] + acc[...]
        o_ref[...] = _rms(y, go_ref[...]) if final_norm else y


def sqrelu_mlp_residual(x, g, w_up, w_down, g_out, final_norm):
    T, D = x.shape
    F = w_up.shape[1]
    tm = min(T, MM_ROWS)
    tf = MLP_FF
    return pl.pallas_call(
        functools.partial(_mlp_kernel, final_norm=final_norm),
        grid=(T // tm, F // tf),
        in_specs=[pl.BlockSpec((tm, D), lambda i, j: (i, 0)),
                  pl.BlockSpec((1, D), lambda i, j: (0, 0)),
                  pl.BlockSpec((D, tf), lambda i, j: (0, j)),
                  pl.BlockSpec((tf, D), lambda i, j: (j, 0)),
                  pl.BlockSpec((1, D), lambda i, j: (0, 0))],
        out_specs=pl.BlockSpec((tm, D), lambda i, j: (i, 0)),
        out_shape=jax.ShapeDtypeStruct((T, D), F32),
        scratch_shapes=[pltpu.VMEM((tm, D), BF16), pltpu.VMEM((tm, D), F32)],
        compiler_params=_cparams("parallel", "arbitrary"),
        name="sqrelu_mlp",
    )(x, g, w_up, w_down, g_out)


def _mm_res_kernel(a_ref, w_ref, r_ref, o_ref):
    o_ref[...] = r_ref[...] + _dot(a_ref[...].astype(BF16), w_ref[...])


def matmul_residual(a, w, res):
    T, K = a.shape
    D = w.shape[1]
    tm = min(T, MM_ROWS)
    return pl.pallas_call(
        _mm_res_kernel,
        grid=(T // tm,),
        in_specs=[pl.BlockSpec((tm, K), lambda i: (i, 0)),
                  pl.BlockSpec((K, D), lambda i: (0, 0)),
                  pl.BlockSpec((tm, D), lambda i: (i, 0))],
        out_specs=pl.BlockSpec((tm, D), lambda i: (i, 0)),
        out_shape=jax.ShapeDtypeStruct((T, D), F32),
        compiler_params=_cparams("parallel"),
        name="matmul_residual",
    )(a, w, res)


def _pool_kernel(x_ref, halo_ref, g_ref, w_ref, sc_ref, o_ref, tail_ref, buf, *, start, tm):
    l = pl.program_id(1)

    @pl.when(l == 0)
    def _():
        buf[0:POOL_HALO] = halo_ref[0]

    x = x_ref[0]
    h = _rms(x, g_ref[...])
    buf[POOL_HALO:POOL_HALO + tm] = h
    pos = start + l * tm + lax.broadcasted_iota(jnp.int32, (tm, 1), 0)
    gc = x.shape[1] // len(POOL_WINDOWS)
    parts = []
    for gi, w in enumerate(POOL_WINDOWS):
        lo, hi = gi * gc, (gi + 1) * gc
        tot = buf[POOL_HALO:POOL_HALO + tm, lo:hi]
        for k in range(1, w):
            tot = tot + buf[POOL_HALO - k:POOL_HALO - k + tm, lo:hi]
        inv_cnt = 1.0 / jnp.minimum(pos + 1, w).astype(F32)
        diff = tot * inv_cnt - h[:, lo:hi]
        parts.append(_dot(diff.astype(BF16), w_ref[gi]))
    y = jnp.concatenate(parts, axis=1) * sc_ref[...]
    o_ref[0] = x + y
    t = buf[tm:tm + POOL_HALO]
    tail_ref[0] = t
    buf[0:POOL_HALO] = t


def pool_mixer_residual(x, halo, g, w_grp, scale, start, tm):
    B, L, D = x.shape
    return pl.pallas_call(
        functools.partial(_pool_kernel, start=start, tm=tm),
        grid=(B, L // tm),
        in_specs=[pl.BlockSpec((1, tm, D), lambda b, l: (b, l, 0)),
                  pl.BlockSpec((1, POOL_HALO, D), lambda b, l: (b, 0, 0)),
                  pl.BlockSpec((1, D), lambda b, l: (0, 0)),
                  pl.BlockSpec(w_grp.shape, lambda b, l: (0, 0, 0)),
                  pl.BlockSpec((1, D), lambda b, l: (0, 0))],
        out_specs=[pl.BlockSpec((1, tm, D), lambda b, l: (b, l, 0)),
                   pl.BlockSpec((1, POOL_HALO, D), lambda b, l: (b, 0, 0))],
        out_shape=[jax.ShapeDtypeStruct((B, L, D), F32), jax.ShapeDtypeStruct((B, POOL_HALO, D), F32)],
        scratch_shapes=[pltpu.VMEM((POOL_HALO + tm, D), F32)],
        compiler_params=_cparams("parallel", "arbitrary"),
        name="pool_mixer",
    )(x, halo, g, w_grp, scale)


def _ssd_pre_kernel(xbc_ref, dtr_ref, c0_ref, cw_ref, cb_ref, dtb_ref, xc_ref, dt_ref, tail_ref, ext, *, tm):
    @pl.when(pl.program_id(1) == 0)
    def _():
        ext[0:SUBLANES] = c0_ref[0]

    ext[SUBLANES:SUBLANES + tm] = xbc_ref[0]
    u = cb_ref[...]
    for k in range(SSD_CONV):
        off = SUBLANES - (SSD_CONV - 1) + k
        u = u + ext[off:off + tm] * cw_ref[k:k + 1]
    xc_ref[0] = _silu(u)
    t = ext[tm:tm + SUBLANES]
    tail_ref[0] = t
    ext[0:SUBLANES] = t
    dt_ref[0] = _softplus(dtr_ref[0] + dtb_ref[...])


def ssd_pre(proj, conv0, conv_w, conv_b, dt_bias, tm):
    B, L, _ = proj.shape
    C = conv_w.shape[1]
    NDT = dt_bias.shape[1]
    return pl.pallas_call(
        functools.partial(_ssd_pre_kernel, tm=tm),
        grid=(B, L // tm),
        in_specs=[pl.BlockSpec((1, tm, C), lambda b, l: (b, l, 1)),
                  pl.BlockSpec((1, tm, NDT), lambda b, l: (b, l, 4)),
                  pl.BlockSpec((1, SUBLANES, C), lambda b, l: (b, 0, 0)),
                  pl.BlockSpec((SSD_CONV, C), lambda b, l: (0, 0)),
                  pl.BlockSpec((1, C), lambda b, l: (0, 0)),
                  pl.BlockSpec((1, NDT), lambda b, l: (0, 0))],
        out_specs=[pl.BlockSpec((1, tm, C), lambda b, l: (b, l, 0)),
                   pl.BlockSpec((1, tm, NDT), lambda b, l: (b, l, 0)),
                   pl.BlockSpec((1, SUBLANES, C), lambda b, l: (b, 0, 0))],
        out_shape=[jax.ShapeDtypeStruct((B, L, C), F32), jax.ShapeDtypeStruct((B, L, NDT), F32),
                   jax.ShapeDtypeStruct((B, SUBLANES, C), F32)],
        scratch_shapes=[pltpu.VMEM((SUBLANES + tm, C), F32)],
        compiler_params=_cparams("parallel", "arbitrary"),
        name="ssd_pre",
    )(proj, proj, conv0, conv_w, conv_b, dt_bias)


def _head_expand_matrix():
    h = np.arange(LANES)[:, None]
    lane = np.arange(8 * SSD_HEAD_DIM)[None, :]
    return jnp.asarray((lane // SSD_HEAD_DIM == h).astype(np.float32)).astype(BF16)


def _ssd_scan_kernel(xh_ref, b_ref, c_ref, dt_ref, alog_ref, tri_ref, e_ref, sin_ref, y_ref, sout_ref, *, seg, gps):
    @pl.when(pl.program_id(2) == 0)
    def _():
        sout_ref[...] = sin_ref[0]

    for gg in range(gps):
        _ssd_scan_group(xh_ref, b_ref, c_ref, dt_ref, alog_ref, tri_ref, e_ref, y_ref, sout_ref, seg, gg)


def _ssd_scan_group(xh_ref, b_ref, c_ref, dt_ref, alog_ref, tri_ref, e_ref, y_ref, sout_ref, seg, gg):
    Q = SSD_CHUNK
    nseg = Q // seg
    hpg = 8
    HD = hpg * SSD_HEAD_DIM
    glanes = slice(gg * LANES, (gg + 1) * LANES)
    xh = xh_ref[0, :, gg * HD:(gg + 1) * HD]
    bg = b_ref[0, :, glanes].astype(BF16)
    cg = c_ref[0, :, glanes]
    dt = dt_ref[0, :, glanes]
    a = -jnp.exp(alog_ref[gg])
    tri = tri_ref[...]
    acum = _dot_exact(tri, dt * a)
    acum_t = acum.T
    dt_t = dt.T
    causal = tri > 0.5
    cb = _dot_nt(cg.astype(BF16), bg)
    lane = lax.broadcasted_iota(jnp.int32, (Q, LANES), 1)
    low = lane < SSD_HEAD_DIM
    wts = []
    for h in range(hpg):
        sg = acum[:, h:h + 1] - acum_t[h:h + 1, :]
        dec = jnp.exp(jnp.where(causal, sg, NEG_INF))
        wts.append((cb * dec * dt_t[h:h + 1, :]).astype(BF16))
    ys = []
    for p in range(hpg // 2):
        xpair = xh[:, LANES * p:LANES * (p + 1)]
        ys.append(_dot(wts[2 * p], jnp.where(low, xpair, 0.0).astype(BF16))
                  + _dot(wts[2 * p + 1], jnp.where(low, 0.0, xpair).astype(BF16)))
    y_intra = jnp.concatenate(ys, axis=1)

    if nseg == 1:
        alast = jnp.broadcast_to(acum[Q - 1:Q, :], (Q, LANES))
    else:
        r = lax.broadcasted_iota(jnp.int32, (Q, Q), 0)
        s = lax.broadcasted_iota(jnp.int32, (Q, Q), 1)
        lastsel = (s == r - (r & (seg - 1)) + (seg - 1)).astype(F32)
        alast = _dot_exact(lastsel, acum)
    wcol = jnp.exp(alast - acum) * dt
    xw_t = (xh * _dot_split3(wcol, e_ref[...])).T
    col = lax.broadcasted_iota(jnp.int32, (hpg * SSD_HEAD_DIM, Q), 1)
    y_parts = []
    for si in range(nseg):
        r0 = si * seg
        h0 = sout_ref[0, si, gg * hpg:(gg + 1) * hpg].reshape(HD, SSD_STATE)
        y_parts.append(_dot_nt(cg[r0:r0 + seg], h0))
        xm = xw_t if nseg == 1 else jnp.where((col >= r0) & (col < r0 + seg), xw_t, 0.0)
        s_new = _dot(xm.astype(BF16), bg)
        for h in range(hpg):
            cd = jnp.exp(alast[r0:r0 + 1, h:h + 1])
            rows = slice(SSD_HEAD_DIM * h, SSD_HEAD_DIM * (h + 1))
            sout_ref[0, si, gg * hpg + h] = h0[rows] * cd + s_new[rows]
    y_inter = y_parts[0] if nseg == 1 else jnp.concatenate(y_parts, axis=0)
    y_ref[0, :, gg * HD:(gg + 1) * HD] = y_intra + y_inter * _dot_split3(jnp.exp(acum), e_ref[...])


def ssd_scan(xc, dt, a_log, tri, state0, li, seg):
    NB, R, _ = xc.shape
    Q = SSD_CHUNK
    nseg = Q // seg
    NC = R // Q
    G = SSD_GROUPS
    HD = 8 * SSD_HEAD_DIM
    gps = G if nseg == 1 else 1
    x_blocks = (G * HD) // (gps * SSD_STATE)
    return pl.pallas_call(
        functools.partial(_ssd_scan_kernel, seg=seg, gps=gps),
        grid=(NB, G // gps, NC),
        in_specs=[pl.BlockSpec((1, Q, gps * HD), lambda b, g, c: (b, c, g)),
                  pl.BlockSpec((1, Q, gps * SSD_STATE), lambda b, g, c: (b, c, x_blocks + g)),
                  pl.BlockSpec((1, Q, gps * SSD_STATE), lambda b, g, c: (b, c, x_blocks + G // gps + g)),
                  pl.BlockSpec((1, Q, gps * LANES), lambda b, g, c: (b, c, g)),
                  pl.BlockSpec((gps, 1, LANES), lambda b, g, c: (g, 0, 0)),
                  pl.BlockSpec((Q, Q), lambda b, g, c: (0, 0)),
                  pl.BlockSpec((LANES, HD), lambda b, g, c: (0, 0)),
                  pl.BlockSpec((1, 1, nseg, gps * 8, SSD_HEAD_DIM, SSD_STATE), lambda b, g, c: (li, b, 0, g, 0, 0))],
        out_specs=[pl.BlockSpec((1, Q, gps * HD), lambda b, g, c: (b, c, g)),
                   pl.BlockSpec((1, nseg, gps * 8, SSD_HEAD_DIM, SSD_STATE), lambda b, g, c: (b, 0, g, 0, 0))],
        out_shape=[jax.ShapeDtypeStruct((NB, R, G * HD), F32), jax.ShapeDtypeStruct(state0.shape[1:], F32)],
        compiler_params=_cparams("parallel", "parallel", "arbitrary"),
        name="ssd_scan",
    )(xc, xc, xc, dt, a_log, tri, _head_expand_matrix(), state0)


def _ssd_post_kernel(y_ref, xh_ref, z_ref, d_ref, g_ref, w_ref, r_ref, o_ref):
    y = (y_ref[...] + xh_ref[...] * d_ref[...]) * _silu(z_ref[...])
    gw = y.shape[1] // SSD_GROUPS
    parts = []
    for gi in range(SSD_GROUPS):
        parts.append(_rms(y[:, gi * gw:(gi + 1) * gw], g_ref[:, gi * gw:(gi + 1) * gw]).astype(BF16))
    o_ref[...] = r_ref[...] + _dot(jnp.concatenate(parts, axis=1), w_ref[...])


def ssd_post(y, xc, proj, d_exp, norm_g, w_out, res):
    T, DI = y.shape
    D = w_out.shape[1]
    tm = min(T, WIDE_ROWS)
    return pl.pallas_call(
        _ssd_post_kernel,
        grid=(T // tm,),
        in_specs=[pl.BlockSpec((tm, DI), lambda i: (i, 0)),
                  pl.BlockSpec((tm, DI), lambda i: (i, 0)),
                  pl.BlockSpec((tm, DI), lambda i: (i, 0)),
                  pl.BlockSpec((1, DI), lambda i: (0, 0)),
                  pl.BlockSpec((1, DI), lambda i: (0, 0)),
                  pl.BlockSpec((DI, D), lambda i: (0, 0)),
                  pl.BlockSpec((tm, D), lambda i: (i, 0))],
        out_specs=pl.BlockSpec((tm, D), lambda i: (i, 0)),
        out_shape=jax.ShapeDtypeStruct((T, D), F32),
        compiler_params=_cparams("parallel"),
        name="ssd_post",
    )(y, xc, proj, d_exp, norm_g, w_out, res)


def _segment_tri(seg):
    r = np.arange(SSD_CHUNK)
    return jnp.asarray(((r[:, None] // seg == r[None, :] // seg) & (r[None, :] <= r[:, None])).astype(np.float32))


def ssd_layer(x, nb, conv_state, ssm_states, li, seg, w):
    T, D = x.shape
    L = T // nb
    proj = norm_matmul(x, w["norm"], w["w_in"])
    conv0 = jnp.pad(conv_state, ((0, 0), (SUBLANES - (SSD_CONV - 1), 0), (0, 0)))
    xc, dt, tail = ssd_pre(proj.reshape(nb, L, -1), conv0, w["conv_w"], w["conv_b"], w["dt_bias"],
                           min(L, SSD_CHUNK))
    rows = SSD_CHUNK if seg < SSD_CHUNK else L
    ngrp = T // rows
    nseg = SSD_CHUNK // seg
    st0 = ssm_states.reshape((ssm_states.shape[0], ngrp, nseg) + ssm_states.shape[2:])
    y, st = ssd_scan(xc.reshape(ngrp, rows, -1), dt.reshape(ngrp, rows, -1), w["a_log"], _segment_tri(seg), st0, li,
                     seg)
    x_new = ssd_post(y.reshape(T, -1), xc.reshape(T, -1), proj, w["d_exp"], w["norm_g"], w["w_out"], x)
    return x_new, tail[:, SUBLANES - (SSD_CONV - 1):], st.reshape(ssm_states.shape[1:])


def _bias_of(d, rb_ref, h):
    val = jnp.full(d.shape, rb_ref[0, h], F32)
    for k in range(1, N_BUCKETS):
        val = jnp.where(d >= BUCKET_THR[k], rb_ref[k, h], val)
    return jnp.where(d >= 0, val - rb_ref[N_BUCKETS - 1, h], 0.0)


def _bias_tables_kernel(rb_ref, tp_ref, ts_ref, *, past_len):
    r = lax.broadcasted_iota(jnp.int32, (Q_TILE, Q_TILE), 0)
    c = lax.broadcasted_iota(jnp.int32, (Q_TILE, Q_TILE), 1)
    t = lax.broadcasted_iota(jnp.int32, (SUBLANES, 1024), 0)
    j = lax.broadcasted_iota(jnp.int32, (SUBLANES, 1024), 1)
    win0 = past_len - WINDOW
    d_s = jnp.where(j < 256, past_len + t - (CMP_STRIDE * j + CMP_BLOCK - 1),
                    jnp.where(j < 384, past_len + t - (past_len - PAGE_SIZE + (j - 256)),
                              jnp.where(j < 512, t - (j - 384), past_len + t - (win0 + (j - 512)))))

    def body(h, carry):
        tp_ref[h, 0] = LOG2E * _bias_of(r - c, rb_ref, h)
        tp_ref[h, 1] = LOG2E * _bias_of(Q_TILE + r - c, rb_ref, h)
        tp_ref[h, 2] = LOG2E * _bias_of(r - CMP_STRIDE * (c - CMP_FRONT) - (CMP_BLOCK - 1), rb_ref, h)
        ts_ref[h] = _bias_of(d_s, rb_ref, h)
        return carry

    lax.fori_loop(0, ATT_HEADS, body, 0)


def bias_tables(rel_bias, past_len):
    return pl.pallas_call(
        functools.partial(_bias_tables_kernel, past_len=past_len),
        in_specs=[pl.BlockSpec(memory_space=pltpu.SMEM)],
        out_specs=[pl.BlockSpec(memory_space=pltpu.VMEM), pl.BlockSpec(memory_space=pltpu.VMEM)],
        out_shape=[jax.ShapeDtypeStruct((ATT_HEADS, 3, Q_TILE, Q_TILE), F32),
                   jax.ShapeDtypeStruct((ATT_HEADS, SUBLANES, 1024), F32)],
        name="bias_tables",
    )(rel_bias)


def _cmp_pre_kernel(*refs, nsrc, nsub, n_prefetch, n_native):
    refs = refs[n_prefetch:]
    srcs = refs[:nsrc]
    w_ref, o_ref, stage = refs[nsrc], refs[nsrc + 1], refs[nsrc + 2]
    rows = nsub * CMP_STRIDE
    n = nsrc * nsub
    for f in range(2):
        xs = []
        for pair in range(2):
            l0 = f * 256 + pair * LANES
            for si, src in enumerate(srcs):
                if si < n_native:
                    stage[si * rows:(si + 1) * rows] = src[0, 0, f, 2 * pair:2 * pair + 2].reshape(LANES, rows).T
                else:
                    stage[si * rows:(si + 1) * rows] = src[0, :, l0:l0 + LANES]
            cols = [stage[pl.ds(s, n, stride=CMP_STRIDE)] for s in range(CMP_STRIDE)]
            xs.append(jnp.concatenate(cols, axis=1))
        x2 = jnp.concatenate(xs, axis=0).astype(BF16)
        pre = _dot(x2, w_ref[f])
        for pair in range(2):
            o_ref[0, f, 2 * pair] = pre[pair * n:(pair + 1) * n, 0:256]
            o_ref[0, f, 2 * pair + 1] = pre[pair * n:(pair + 1) * n, 256:512]


def _cmp_fin_kernel(pre_ref, pe_ref, w1_ref, w2_ref, o_ref, scr, *, n):
    for f in range(2):
        cvec = _dot(pe_ref[f].astype(BF16), w1_ref[f])[0:1]
        for kv in range(ATT_KV):
            pre = pre_ref[0, f, kv]
            slot = f * ATT_KV + kv
            scr[slot, 0:n] = pre[:, CMP_HIDDEN:2 * CMP_HIDDEN]
            scr[slot, n:n + SUBLANES] = jnp.zeros((SUBLANES, CMP_HIDDEN), F32)
            hid = pre[:, 0:CMP_HIDDEN] + scr[slot, pl.ds(1, n)] + cvec
            o_ref[0, f, kv] = _dot(_silu(hid).astype(BF16), w2_ref[f])


def cmp_finish(pre, pe8, w1, w2d):
    B, _, KV, n, _ = pre.shape
    return pl.pallas_call(
        functools.partial(_cmp_fin_kernel, n=n),
        grid=(B,),
        in_specs=[pl.BlockSpec((1, 2, KV, n, 256), lambda b: (b, 0, 0, 0, 0)),
                  pl.BlockSpec(pe8.shape, lambda b: (0, 0, 0)),
                  pl.BlockSpec(w1.shape, lambda b: (0, 0, 0)),
                  pl.BlockSpec(w2d.shape, lambda b: (0, 0, 0))],
        out_specs=pl.BlockSpec((1, 2, KV, n, LANES), lambda b: (b, 0, 0, 0, 0)),
        out_shape=jax.ShapeDtypeStruct((B, 2, KV, n, LANES), F32),
        scratch_shapes=[pltpu.VMEM((2 * KV, n + SUBLANES, CMP_HIDDEN), F32)],
        compiler_params=_cparams("parallel"),
        name="cmp_finish",
    )(pre, pe8, w1, w2d)


def cmp_pre_prompt(proj, w_pair, rows):
    B, L, _ = proj.shape
    nsub = rows // CMP_STRIDE
    return pl.pallas_call(
        functools.partial(_cmp_pre_kernel, nsrc=1, nsub=nsub, n_prefetch=0, n_native=0),
        grid=(B, L // rows),
        in_specs=[pl.BlockSpec((1, rows, 512), lambda b, t: (b, t, 2)),
                  pl.BlockSpec(w_pair.shape, lambda b, t: (0, 0, 0))],
        out_specs=pl.BlockSpec((1, 2, ATT_KV, nsub, 256), lambda b, t: (b, 0, 0, t, 0)),
        out_shape=jax.ShapeDtypeStruct((B, 2, ATT_KV, L // CMP_STRIDE, 256), F32),
        scratch_shapes=[pltpu.VMEM((rows, LANES), F32)],
        compiler_params=_cparams("parallel", "parallel"),
        name="cmp_pre_prompt",
    )(proj, w_pair)


def cmp_pre_sample(page_table, cache_t, li, new_page, w_pair):
    B, n_pages = page_table.shape
    nsub = PAGE_SIZE // CMP_STRIDE
    nsrc = n_pages + 1

    def page_spec(p):
        return pl.BlockSpec((1, 1, 2, ATT_KV, ATT_HEAD_DIM, PAGE_SIZE), lambda b, pt: (li, pt[b, p], 0, 0, 0, 0))

    grid_spec = pltpu.PrefetchScalarGridSpec(
        num_scalar_prefetch=1,
        grid=(B,),
        in_specs=[page_spec(p) for p in range(n_pages)]
        + [pl.BlockSpec((1, PAGE_SIZE, 512), lambda b, pt: (b, 0, 0)),
           pl.BlockSpec(w_pair.shape, lambda b, pt: (0, 0, 0))],
        out_specs=pl.BlockSpec((1, 2, ATT_KV, nsrc * nsub, 256), lambda b, pt: (b, 0, 0, 0, 0)),
        scratch_shapes=[pltpu.VMEM((nsrc * PAGE_SIZE, LANES), F32)],
    )
    return pl.pallas_call(
        functools.partial(_cmp_pre_kernel, nsrc=nsrc, nsub=nsub, n_prefetch=1, n_native=n_pages),
        grid_spec=grid_spec,
        out_shape=jax.ShapeDtypeStruct((B, 2, ATT_KV, nsrc * nsub, 256), F32),
        compiler_params=_cparams("parallel"),
        name="cmp_pre_sample",
    )(page_table, *([cache_t] * n_pages), new_page, w_pair)


def _importance_matrix(n_rows, front):
    m = np.arange(n_rows)[:, None] - front
    j = np.arange(LANES)[None, :]
    ratio = SEL_BLOCK // CMP_STRIDE
    a = ((m >= ratio * j) & (m <= ratio * j + ratio - 1)).astype(np.float32) \
        + ((m >= ratio * j - 1) & (m <= ratio * j + ratio - 2)).astype(np.float32)
    return jnp.asarray(a)


def _softmax_rows(s):
    m = jnp.max(s, axis=-1, keepdims=True)
    m = jnp.where(m > NEG_INF, m, 0.0)
    e = jnp.exp(s - m)
    return e / jnp.maximum(jnp.sum(e, axis=-1, keepdims=True), 1e-30)


def _top_blocks(score, n_sel, axis=1):
    jb = lax.broadcasted_iota(jnp.int32, score.shape, axis).astype(F32)
    sel = jnp.zeros(score.shape, F32)
    sc = score
    for _ in range(n_sel):
        mx = jnp.max(sc, axis=axis, keepdims=True)
        idx = jnp.min(jnp.where(sc == mx, jb, 1e9), axis=axis, keepdims=True)
        pick = jb == idx
        sel = jnp.where(pick, 1.0, sel)
        sc = jnp.where(pick, NEG_INF, sc)
    return sel


def _tile4(x):
    return jnp.concatenate([x, x, x, x], axis=0)


def _attn_prompt_kernel(q_ref, gate_ref, kc_ref, vc_ref, ks_ref, vs_ref, kw_ref, vw_ref, tp_ref, amat_ref, o_ref,
                        mx_scr, acc_scr, *, nsub):
    i = pl.program_id(2)
    QT = Q_TILE
    R2 = 2 * QT
    low = lax.broadcasted_iota(jnp.int32, (QT, LANES), 1) < ATT_HEAD_DIM
    qpp = jnp.concatenate([q_ref[0, :, 0:LANES], q_ref[0, :, LANES:2 * LANES]], axis=0).astype(BF16)
    r1 = lax.broadcasted_iota(jnp.int32, (QT, 1), 0)
    c1 = lax.broadcasted_iota(jnp.int32, (1, LANES), 1)
    lower = r1 >= c1
    upper = r1 <= c1
    i_vec = jnp.full((QT, LANES), i, jnp.int32)

    def half_masks(n):
        lo = lax.broadcasted_iota(jnp.int32, (n, LANES), 1) < ATT_HEAD_DIM
        return jnp.where(lo, 1.0, 0.0).astype(BF16), jnp.where(lo, 0.0, 1.0).astype(BF16)

    def bd(x2):
        m_lo, m_hi = half_masks(x2.shape[0])
        return jnp.concatenate([x2 * m_lo, x2 * m_hi], axis=0)

    def spread(madd):
        m2 = jnp.concatenate([madd, madd], axis=1)
        return jnp.concatenate([m2, m2], axis=0)

    def bias_full(kind):
        return jnp.concatenate([jnp.concatenate([tp_ref[2 * p, kind], tp_ref[2 * p + 1, kind]], axis=1)
                                for p in range(2)], axis=0)

    def row_max2(x):
        parts = []
        for w in range(2):
            m = jnp.max(x[:, LANES * w:LANES * (w + 1)], axis=-1, keepdims=True)
            parts.append(jnp.broadcast_to(jnp.where(m > NEG_INF, m, 0.0), (R2, LANES)))
        return jnp.concatenate(parts, axis=1)

    def widen(m2, reps):
        if reps == 1:
            return m2
        return jnp.concatenate([m2[:, :LANES]] * reps + [m2[:, LANES:]] * reps, axis=1)

    def cols(ref, key0, n):
        return ref[0, :, pl.ds(pl.multiple_of(key0, LANES), n)]

    def bd_t(xt):
        z = jnp.zeros_like(xt)
        return jnp.concatenate([jnp.concatenate([xt, z], axis=0), jnp.concatenate([z, xt], axis=0)], axis=1)

    def scores(kt, add):
        return _dot(qpp, bd_t(kt)) + add

    def scores_nt(k2, add):
        return _dot_nt(qpp, bd(k2)) + add

    def weighted(s, m_wide, vt):
        n = vt.shape[1]
        same = jnp.right_shift(lax.broadcasted_iota(jnp.int32, (LANES, 2 * n), 0), ATT_HEAD_DIM.bit_length() - 1) \
            == (lax.broadcasted_iota(jnp.int32, (LANES, 2 * n), 1) >= n).astype(jnp.int32)
        rhs_t = jnp.concatenate([bd_t(vt), jnp.where(same, 1.0, 0.0).astype(BF16)], axis=0)
        return _dot_nt(jnp.exp2(s - m_wide).astype(BF16), rhs_t)

    def normalized(acc):
        return acc[:, :LANES] / jnp.maximum(acc[:, LANES:], 1e-30)

    nback = WINDOW // QT
    ws, wt = [], []
    for back in range(nback, -1, -1):
        tc = jnp.maximum(i - back, 0)
        ok = i_vec >= back
        if back == nback:
            ok = ok & upper
        if back == 0:
            ok = lower
        madd = spread(jnp.where(ok, 0.0, NEG_INF))
        ws.append(scores(cols(kw_ref, tc * QT, QT), madd + bias_full(back) if back <= 1 else madd))
        wt.append(tc)
    mel = ws[0]
    for s in ws[1:]:
        mel = jnp.maximum(mel, s)
    m2_w = row_max2(mel)
    acc_w = None
    for s, tc in zip(ws, wt):
        term = weighted(s, m2_w, cols(vw_ref, tc * QT, QT))
        acc_w = term if acc_w is None else acc_w + term
    o_w = normalized(acc_w)

    st = pl.multiple_of(i * SUBLANES, SUBLANES)
    d_cmp = r1 - CMP_STRIDE * (c1 - CMP_FRONT) - (CMP_BLOCK - 1)
    ctiles = []
    for j in range(nsub // QT):
        rows = slice(CMP_FRONT + j * QT, CMP_FRONT + (j + 1) * QT)
        ok = jnp.broadcast_to(j * QT + c1 < i * SUBLANES - CMP_FRONT, (QT, LANES))
        ctiles.append((kc_ref[0, 0, 0, rows, :], vc_ref[0, 0, 0, rows, :], amat_ref[rows, :], ok, False))
    ctiles.append((kc_ref[0, 0, 0, pl.ds(st, QT), :], vc_ref[0, 0, 0, pl.ds(st, QT), :], amat_ref[pl.ds(st, QT), :],
                   (d_cmp >= 0) & (c1 >= CMP_FRONT - i * SUBLANES), True))
    ss = []
    for kt, _, _, ok, diag in ctiles:
        add = spread(jnp.where(ok, 0.0, NEG_INF))
        ss.append(scores_nt(kt.astype(BF16), add + bias_full(2) if diag else add))
    mel = ss[0]
    for s in ss[1:]:
        mel = jnp.maximum(mel, s)
    m2 = row_max2(mel)
    es = [jnp.exp2(s - m2) for s in ss]
    lel = es[0]
    for e in es[1:]:
        lel = lel + e
    inv = []
    for w in range(2):
        l = jnp.sum(lel[:, LANES * w:LANES * (w + 1)], axis=-1, keepdims=True)
        inv.append(jnp.broadcast_to(1.0 / jnp.maximum(l, 1e-30), (R2, LANES)))
    inv2 = jnp.concatenate(inv, axis=1)
    o_c = None
    phs = []
    for e, (_, vt, _, _, _) in zip(es, ctiles):
        pn = e * inv2
        term = _dot(pn.astype(BF16), bd(vt.astype(BF16)))
        o_c = term if o_c is None else o_c + term
        phs.append(pn[:QT, :LANES] + pn[:QT, LANES:] + pn[QT:, :LANES] + pn[QT:, LANES:])
    imp = _dot_split3(jnp.concatenate(phs, axis=1),
                      jnp.concatenate([t[2] for t in ctiles], axis=0).astype(BF16))

    qblk = 2 * i + (c1 >= SEL_BLOCK).astype(jnp.int32)
    lag = qblk - r1
    allowed = lag >= 0
    forced = (r1 == 0) | (allowed & (lag < N_LOCAL))
    score_t = jnp.where(allowed, imp.T + jnp.where(forced, FORCE_BONUS, 0.0), -1.0)
    sel = _top_blocks(score_t, N_SEL, axis=0).T.astype(BF16)

    FK = FAR_KEYS
    tm1 = jnp.maximum(i - 1, 0)
    limit = tm1 * QT
    n_grp = jnp.right_shift(tm1 + FK // QT - 1, (FK // QT).bit_length() - 1)
    shift = SEL_BLOCK.bit_length() - 1

    def key_mask(key0, n):
        jb = lax.broadcasted_iota(jnp.int32, (LANES, n), 0)
        key = key0 + lax.broadcasted_iota(jnp.int32, (LANES, n), 1)
        hit = jb == jnp.right_shift(key, shift)
        if n == FK:
            hit = hit & (key < limit)
        return jnp.where(_dot(sel, jnp.where(hit, 1.0, 0.0).astype(BF16)) > 0.5, 0.0, NEG_INF)

    low2 = lax.broadcasted_iota(jnp.int32, (R2, LANES), 1) < ATT_HEAD_DIM

    def online_step(s, vt):
        n = vt.shape[1]
        reps = n // LANES
        halves = []
        for w in range(2):
            m = s[:, w * n:w * n + LANES]
            for c4 in range(1, reps):
                m = jnp.maximum(m, s[:, w * n + c4 * LANES:w * n + (c4 + 1) * LANES])
            halves.append(jnp.broadcast_to(jnp.max(m, axis=-1, keepdims=True), (R2, LANES)))
        m_old = mx_scr[...]
        m_new = jnp.maximum(m_old, jnp.concatenate(halves, axis=1))
        m_safe = jnp.where(m_new > NEG_INF, m_new, 0.0)
        alpha = jnp.exp2(m_old - m_safe)
        a_mix = jnp.where(low2, alpha[:, :LANES], alpha[:, LANES:])
        mx_scr[...] = m_new
        acc_scr[...] = acc_scr[...] * jnp.concatenate([a_mix, a_mix], axis=1) + weighted(s, widen(m_safe, reps), vt)

    mx_scr[...] = jnp.full(mx_scr.shape, NEG_INF, F32)
    acc_scr[...] = jnp.zeros_like(acc_scr)

    def far_body(gi, c):
        s = scores(cols(ks_ref, gi * FK, FK), spread(key_mask(gi * FK, FK)))
        online_step(s, cols(vs_ref, gi * FK, FK))
        return c

    lax.fori_loop(0, n_grp, far_body, 0)
    mk_s = spread(jnp.where(i_vec >= 1, key_mask(tm1 * QT, QT), NEG_INF))
    online_step(scores(cols(ks_ref, tm1 * QT, QT), mk_s + bias_full(1)), cols(vs_ref, tm1 * QT, QT))
    mk_d = spread(jnp.where(lower, key_mask(i * QT, QT), NEG_INF))
    online_step(scores(cols(ks_ref, i * QT, QT), mk_d + bias_full(0)), cols(vs_ref, i * QT, QT))
    o_s = normalized(acc_scr[...])

    g = _sigmoid(gate_ref[0])

    def gate(br):
        tiles = []
        for p in range(2):
            c0 = br * ATT_HG + 2 * p
            tiles.append(jnp.where(low, jnp.broadcast_to(g[:, c0:c0 + 1], (QT, LANES)),
                                   jnp.broadcast_to(g[:, c0 + 1:c0 + 2], (QT, LANES))))
        return jnp.concatenate(tiles, axis=0)

    o = gate(0) * o_c + gate(1) * o_s + gate(2) * o_w
    o_ref[0] = jnp.concatenate([o[:QT], o[QT:]], axis=1)


def _nsa_in_proj_kernel(x_ref, g_ref, wr_ref, wt_ref, row_ref, kvt_ref, kvt16_ref):
    h = _rms(x_ref[0], g_ref[...]).astype(BF16)
    row_ref[0] = _dot(h, wr_ref[...])
    kv = _dot_nt(wt_ref[...], h)
    kvt_ref[0] = kv
    kvt16_ref[0] = kv[kv.shape[0] - kvt16_ref.shape[1]:].astype(BF16)


def nsa_in_proj_prompt(x, g, w_row, w_kvt, tm):
    B, L, D = x.shape
    NR, NT = w_row.shape[1], w_kvt.shape[0]
    n16 = 4 * ATT_KV * ATT_HEAD_DIM
    return pl.pallas_call(
        _nsa_in_proj_kernel,
        grid=(B, L // tm),
        in_specs=[pl.BlockSpec((1, tm, D), lambda b, l: (b, l, 0)),
                  pl.BlockSpec((1, D), lambda b, l: (0, 0)),
                  pl.BlockSpec((D, NR), lambda b, l: (0, 0)),
                  pl.BlockSpec((NT, D), lambda b, l: (0, 0))],
        out_specs=[pl.BlockSpec((1, tm, NR), lambda b, l: (b, l, 0)),
                   pl.BlockSpec((1, NT, tm), lambda b, l: (b, 0, l)),
                   pl.BlockSpec((1, n16, tm), lambda b, l: (b, 0, l))],
        out_shape=[jax.ShapeDtypeStruct((B, L, NR), F32), jax.ShapeDtypeStruct((B, NT, L), F32),
                   jax.ShapeDtypeStruct((B, n16, L), BF16)],
        compiler_params=_cparams("parallel", "parallel"),
        name="nsa_in_proj_prompt",
    )(x, g, w_row, w_kvt)


def attn_prompt(rowp, cmp_kv, kvt16, tp, amat):
    B, L, _ = rowp.shape
    nsub = L // CMP_STRIDE
    ncp = cmp_kv.shape[3]

    def kv_spec(f):
        return pl.BlockSpec((1, ATT_HEAD_DIM, L), lambda b, k, i: (b, f * ATT_KV + k, 0))

    def cmp_spec(f):
        return pl.BlockSpec((1, 1, 1, ncp, LANES), lambda b, k, i: (b, f, k, 0, 0))

    return pl.pallas_call(
        functools.partial(_attn_prompt_kernel, nsub=nsub),
        grid=(B, ATT_KV, L // Q_TILE),
        in_specs=[pl.BlockSpec((1, Q_TILE, 256), lambda b, k, i: (b, i, k)),
                  pl.BlockSpec((1, Q_TILE, LANES), lambda b, k, i: (b, i, 12 + k)),
                  cmp_spec(0), cmp_spec(1), kv_spec(0), kv_spec(1), kv_spec(2), kv_spec(3),
                  pl.BlockSpec((ATT_HG, 3, Q_TILE, Q_TILE), lambda b, k, i: (k, 0, 0, 0)),
                  pl.BlockSpec(amat.shape, lambda b, k, i: (0, 0))],
        out_specs=pl.BlockSpec((1, Q_TILE, 256), lambda b, k, i: (b, i, k)),
        out_shape=jax.ShapeDtypeStruct((B, L, ATT_HEADS * ATT_HEAD_DIM), F32),
        scratch_shapes=[pltpu.VMEM((2 * Q_TILE, 2 * LANES), F32),
                        pltpu.VMEM((2 * Q_TILE, 2 * LANES), F32)],
        compiler_params=_cparams("parallel", "parallel", "arbitrary"),
        name="attn_prompt",
    )(rowp, rowp, cmp_kv, cmp_kv, kvt16, kvt16, kvt16, kvt16, tp, amat)


def _attn_sample_kernel(*refs, n_pages, past_len, nb):
    pages = refs[1:1 + nb * n_pages]
    for bb in range(nb):
        _attn_sample_one(pages[bb * n_pages:(bb + 1) * n_pages], *refs[1 + nb * n_pages:], bb=bb, n_pages=n_pages,
                         past_len=past_len)


def _attn_sample_one(pages, q_ref, gate_ref, new_s_ref, new_w_ref, win_ref, cmp_ref, ts_ref, amat_ref, o_ref, *,
                     bb, n_pages, past_len):
    T = SUBLANES
    R = ATT_HG * T
    n_cmp_rows = cmp_ref.shape[3]
    lane_t = lax.broadcasted_iota(jnp.int32, (T, LANES), 1)
    low_t = lane_t < ATT_HEAD_DIM
    t_r = lax.broadcasted_iota(jnp.int32, (R, 1), 0) & (T - 1)
    c1 = lax.broadcasted_iota(jnp.int32, (1, LANES), 1)
    g = _sigmoid(gate_ref[bb])
    zeros_new = jnp.zeros((LANES - T, LANES), F32)
    new_ok = (c1 <= t_r) & (c1 < T)
    n_c = lax.broadcasted_iota(jnp.int32, (1, 256), 1)
    d_c = past_len + t_r - (CMP_STRIDE * n_c + CMP_BLOCK - 1)
    c_w = lax.broadcasted_iota(jnp.int32, (1, WINDOW), 1)
    cmp_pad = jnp.zeros((256 - n_cmp_rows, LANES), F32)

    def lane_tile(kv):
        return LANES * (kv // 2)

    def queries(kv):
        par = kv % 2
        keep = (lane_t >= ATT_HEAD_DIM) if par else low_t
        qrows = []
        for hg in range(ATT_HG):
            h = kv * ATT_HG + hg
            q2 = q_ref[bb, :, LANES * (h // 2):LANES * (h // 2 + 1)]
            if h % 2 != par:
                q2 = pltpu.roll(q2, ATT_HEAD_DIM, axis=1)
            qrows.append(jnp.where(keep, q2, 0.0))
        return jnp.concatenate(qrows, axis=0).astype(BF16)

    def new_rows(ref, l0):
        return jnp.concatenate([ref[bb, :, l0:l0 + LANES], zeros_new], axis=0).astype(BF16)

    qs_all = [queries(kv) for kv in range(ATT_KV)]
    tabs = [ts_ref[kv * ATT_HG:(kv + 1) * ATT_HG].reshape(R, 1024) for kv in range(ATT_KV)]

    o_cs, p_sums = [], []
    for kv in range(ATT_KV):
        kc = jnp.concatenate([cmp_ref[bb, 0, kv], cmp_pad], axis=0).astype(BF16)
        vc = jnp.concatenate([cmp_ref[bb, 1, kv], cmp_pad], axis=0).astype(BF16)
        s_c = _dot_nt(qs_all[kv], kc) + tabs[kv][:, 0:256]
        p_c = _softmax_rows(jnp.where(d_c >= 0, s_c, NEG_INF))
        o_cs.append(_dot(p_c.astype(BF16), vc))
        p_sums.append(p_c[0:T] + p_c[T:2 * T] + p_c[2 * T:3 * T] + p_c[3 * T:4 * T])

    p_all = jnp.concatenate(p_sums + [jnp.zeros((LANES - ATT_KV * T, 256), F32)], axis=0)
    imp_t = _dot_split3(p_all, amat_ref[...].astype(BF16)).T
    rb = lax.broadcasted_iota(jnp.int32, (LANES, 1), 0)
    qblk = jnp.right_shift(past_len + (c1 & (T - 1)), SEL_BLOCK.bit_length() - 1)
    lag = qblk - rb
    allowed = lag >= 0
    forced = (rb == 0) | (allowed & (lag < N_LOCAL))
    score_t = jnp.where(allowed, imp_t + jnp.where(forced, FORCE_BONUS, 0.0), -1.0)
    sel_all = _top_blocks(score_t, N_SEL, axis=0).T

    nb = 2 * n_pages
    n_past = n_pages * PAGE_SIZE
    zero = jnp.zeros((R, LANES), F32)

    def pair_of(kv):
        return slice(2 * (kv // 2), 2 * (kv // 2) + 2)

    s_alls, s_ws = [], []
    for kv in range(ATT_KV):
        pl0, qs, tab = lane_tile(kv), qs_all[kv], tabs[kv]
        sel = sel_all[kv * T:(kv + 1) * T]
        k_t = jnp.concatenate([pages[p][0, 0, 0, pair_of(kv)].reshape(LANES, PAGE_SIZE).astype(BF16)
                               for p in range(n_pages)], axis=1)
        m_parts = [jnp.where(c1 < SEL_BLOCK, sel[:, 2 * p:2 * p + 1], sel[:, 2 * p + 1:2 * p + 2])
                   for p in range(n_pages)]
        m_parts.append(jnp.where(new_ok[0:T], jnp.broadcast_to(sel[:, nb:nb + 1], (T, LANES)), 0.0))
        bias_s = jnp.concatenate([zero] * (n_pages - 1) + [tab[:, 256:384], tab[:, 384:512]], axis=1)
        s_raw = jnp.concatenate([_dot(qs, k_t), _dot_nt(qs, new_rows(new_s_ref, pl0))], axis=1)
        s_alls.append(jnp.where(_tile4(jnp.concatenate(m_parts, axis=1)) > 0.5, s_raw + bias_s, NEG_INF))
        kw_t = win_ref[0, bb, 0, pair_of(kv)].reshape(LANES, WINDOW).astype(BF16)
        bias_w = jnp.concatenate([tab[:, 512:1024] + jnp.where(c_w >= t_r, 0.0, NEG_INF),
                                  tab[:, 384:512] + jnp.where(new_ok, 0.0, NEG_INF)], axis=1)
        s_ws.append(jnp.concatenate([_dot(qs, kw_t), _dot_nt(qs, new_rows(new_w_ref, pl0))], axis=1) + bias_w)
    p_ss = [_softmax_rows(s).astype(BF16) for s in s_alls]
    p_ws = [_softmax_rows(s).astype(BF16) for s in s_ws]
    o_ss, o_ws = [], []
    for kv in range(ATT_KV):
        pl0 = lane_tile(kv)
        v_t = jnp.concatenate([pages[p][0, 0, 1, pair_of(kv)].reshape(LANES, PAGE_SIZE).astype(BF16)
                               for p in range(n_pages)], axis=1)
        o_ss.append(_dot_nt(p_ss[kv][:, :n_past], v_t) + _dot(p_ss[kv][:, n_past:], new_rows(new_s_ref, 256 + pl0)))
        vw_t = win_ref[0, bb, 1, pair_of(kv)].reshape(LANES, WINDOW).astype(BF16)
        o_ws.append(_dot_nt(p_ws[kv][:, :WINDOW], vw_t) + _dot(p_ws[kv][:, WINDOW:], new_rows(new_w_ref, 256 + pl0)))

    out_tiles = []
    for kv in range(ATT_KV):
        par = kv % 2
        o_c, o_s, o_w = o_cs[kv], o_ss[kv], o_ws[kv]
        gk = g[:, LANES * kv:LANES * (kv + 1)]
        for pair in range(2):
            halves = []
            for which in range(2):
                hg = 2 * pair + which
                rows = slice(hg * T, (hg + 1) * T)
                o = gk[:, hg:hg + 1] * o_c[rows] + gk[:, 4 + hg:5 + hg] * o_s[rows] + gk[:, 8 + hg:9 + hg] * o_w[rows]
                if which != par:
                    o = pltpu.roll(o, ATT_HEAD_DIM, axis=1)
                halves.append(o)
            out_tiles.append(jnp.where(low_t, halves[0], halves[1]))
    o_ref[bb] = jnp.concatenate(out_tiles, axis=1)


def attn_sample(page_table, cache_t, win_t, li, proj, cmp_kv, ts, amat, past_len):
    B, n_pages = page_table.shape
    T = proj.shape[1]
    nb = SAMPLE_SEQS_PER_STEP

    def page_spec(bb, p):
        return pl.BlockSpec((1, 1, 2, ATT_KV, ATT_HEAD_DIM, PAGE_SIZE),
                            lambda b, pt: (li, pt[nb * b + bb, p], 1, 0, 0, 0))

    grid_spec = pltpu.PrefetchScalarGridSpec(
        num_scalar_prefetch=1,
        grid=(B // nb,),
        in_specs=[page_spec(bb, p) for bb in range(nb) for p in range(n_pages)]
        + [pl.BlockSpec((nb, T, 1024), lambda b, pt: (b, 0, 0)),
           pl.BlockSpec((nb, T, 512), lambda b, pt: (b, 0, 5)),
           pl.BlockSpec((nb, T, 512), lambda b, pt: (b, 0, 3)),
           pl.BlockSpec((nb, T, 512), lambda b, pt: (b, 0, 4)),
           pl.BlockSpec((1, nb) + win_t.shape[2:], lambda b, pt: (li, b, 0, 0, 0, 0)),
           pl.BlockSpec((nb,) + cmp_kv.shape[1:], lambda b, pt: (b, 0, 0, 0, 0)),
           pl.BlockSpec(ts.shape, lambda b, pt: (0, 0, 0)),
           pl.BlockSpec(amat.shape, lambda b, pt: (0, 0))],
        out_specs=pl.BlockSpec((nb, T, 1024), lambda b, pt: (b, 0, 0)),
    )
    return pl.pallas_call(
        functools.partial(_attn_sample_kernel, n_pages=n_pages, past_len=past_len, nb=nb),
        grid_spec=grid_spec,
        out_shape=jax.ShapeDtypeStruct((B, T, ATT_HEADS * ATT_HEAD_DIM), F32),
        compiler_params=_cparams("parallel"),
        name="attn_sample",
    )(page_table, *([cache_t] * (nb * n_pages)), proj, proj, proj, proj, win_t, cmp_kv, ts, amat)


def _ssd_weights(i, li, norm_mix, ssd_w_in, ssd_conv_w, ssd_conv_b, ssd_dt_bias, ssd_a_log, ssd_d, ssd_norm, ssd_w_out):
    d_inner = ssd_w_out.shape[1]
    conv_dim = ssd_conv_w.shape[2]
    heads = ssd_dt_bias.shape[1]
    hpg = heads // SSD_GROUPS
    w = ssd_w_in[li]

    def per_group(v):
        v = v.reshape(v.shape[:-1] + (SSD_GROUPS, hpg))
        return jnp.pad(v, [(0, 0)] * (v.ndim - 1) + [(0, LANES - hpg)]).reshape(v.shape[:-2] + (SSD_GROUPS * LANES,))

    w_dt = per_group(w[:, d_inner + conv_dim:])
    w_in = jnp.concatenate([w[:, :d_inner], w_dt, jnp.zeros_like(w_dt), w[:, d_inner:d_inner + conv_dim]], axis=1)
    return dict(
        norm=norm_mix[i][None],
        w_in=w_in.astype(BF16),
        conv_w=ssd_conv_w[li], conv_b=ssd_conv_b[li][None],
        dt_bias=per_group(ssd_dt_bias[li])[None],
        a_log=per_group(ssd_a_log[li]).reshape(SSD_GROUPS, 1, LANES),
        d_exp=jnp.repeat(ssd_d[li], SSD_HEAD_DIM)[None],
        norm_g=ssd_norm[li][None],
        w_out=ssd_w_out[li].astype(BF16),
    )


def _nsa_in_weight(w, q_scale):
    q_dim = ATT_HEADS * ATT_HEAD_DIM
    kv_dim = 6 * ATT_KV * ATT_HEAD_DIM
    idx = np.zeros((ATT_KV, LANES), np.int32)
    ok = np.zeros((ATT_KV, LANES), bool)
    for kv in range(ATT_KV):
        for br in range(3):
            for hg in range(ATT_HG):
                idx[kv, br * ATT_HG + hg] = q_dim + kv_dim + (kv * ATT_HG + hg) * 3 + br
                ok[kv, br * ATT_HG + hg] = True
    w_g = jnp.where(jnp.asarray(ok.reshape(-1))[None, :], w[:, idx.reshape(-1)], 0.0)
    return jnp.concatenate([w[:, :q_dim] * q_scale, w[:, q_dim:q_dim + kv_dim], w_g], axis=1).astype(BF16)


def _cmp_pair_weight(w1):
    w1r = w1.reshape(2, 2, CMP_STRIDE, ATT_HEAD_DIM, CMP_HIDDEN)
    eye = jnp.eye(2, dtype=w1.dtype)
    wp = jnp.einsum("fjsde,wv->fswdvje", w1r, eye)
    return wp.reshape(2, CMP_STRIDE * 2 * ATT_HEAD_DIM, 2 * 2 * CMP_HIDDEN).astype(BF16)


def kernel(x_prompt, x_sample, cache_nsa_kv, state_nsa_win, state_ssm, state_conv, state_pool, page_table, rel_bias,
           norm_mix, norm_ffn, norm_out, ffn_w_up, ffn_w_down, ssd_w_in, ssd_conv_w, ssd_conv_b, ssd_dt_bias,
           ssd_a_log, ssd_d, ssd_norm, ssd_w_out, pool_w, pool_scale, nsa_w_in, nsa_cmp_pe, nsa_cmp_w1, nsa_cmp_w2,
           nsa_w_out):
    bp, lp, d_model = x_prompt.shape
    bs, ls, _ = x_sample.shape
    depth = norm_mix.shape[0]
    n_pages = page_table.shape[1]
    past_len = n_pages * PAGE_SIZE
    assert ls == SUBLANES and lp % (16 * Q_TILE) == 0 and past_len >= WINDOW and state_nsa_win.shape[2] == WINDOW
    xp = x_prompt.reshape(bp * lp, d_model)
    xs = x_sample.reshape(bs * ls, d_model)
    outs = {k: [] for k in ("kv_p", "kv_s", "win_p", "win_s", "ssm_p", "ssm_s", "conv_p", "conv_s", "pool_p", "pool_s")}
    for i in range(depth):
        kind, li = i % 3, i // 3
        if kind == 0:
            w = _ssd_weights(i, li, norm_mix, ssd_w_in, ssd_conv_w, ssd_conv_b, ssd_dt_bias, ssd_a_log, ssd_d,
                             ssd_norm, ssd_w_out)
            conv_dim = ssd_conv_w.shape[2]
            xp, c_p, s_p = ssd_layer(xp, bp, jnp.zeros((bp, SSD_CONV - 1, conv_dim), F32),
                                     jnp.zeros((1, bp) + state_ssm.shape[2:], F32), 0, SSD_CHUNK, w)
            xs, c_s, s_s = ssd_layer(xs, bs, state_conv[li], state_ssm, li, ls, w)
            outs["conv_p"].append(c_p)
            outs["conv_s"].append(c_s)
            outs["ssm_p"].append(s_p)
            outs["ssm_s"].append(s_s)
        elif kind == 1:
            g = norm_mix[i][None]
            pw = pool_w[li].astype(BF16)
            sc = pool_scale[li][None]
            xp3, tail_p = pool_mixer_residual(xp.reshape(bp, lp, d_model), jnp.zeros((bp, POOL_HALO, d_model), F32),
                                              g, pw, sc, 0, POOL_ROWS)
            halo_s = jnp.pad(state_pool[li], ((0, 0), (1, 0), (0, 0)))
            xs3, tail_s = pool_mixer_residual(xs.reshape(bs, ls, d_model), halo_s, g, pw, sc, past_len, ls)
            xp, xs = xp3.reshape(bp * lp, d_model), xs3.reshape(bs * ls, d_model)
            outs["pool_p"].append(tail_p[:, 1:])
            outs["pool_s"].append(tail_s[:, 1:])
        else:
            g = norm_mix[i][None]
            w_in = _nsa_in_weight(nsa_w_in[li], ATT_HEAD_DIM ** -0.5)
            w_in2 = _nsa_in_weight(nsa_w_in[li], ATT_HEAD_DIM ** -0.5 * LOG2E)
            w_pair = _cmp_pair_weight(nsa_cmp_w1[li])
            pe8 = jnp.broadcast_to(nsa_cmp_pe[li].reshape(2, 1, -1), (2, SUBLANES, CMP_BLOCK * ATT_HEAD_DIM))
            w1 = nsa_cmp_w1[li].astype(BF16)
            w2d = jnp.concatenate([nsa_cmp_w2[li], nsa_cmp_w2[li]], axis=-1).astype(BF16)
            tp, ts = bias_tables(rel_bias, past_len)
            w_out = nsa_w_out[li].astype(BF16)
            kvw = ATT_KV * ATT_HEAD_DIM
            cache_t = jnp.transpose(cache_nsa_kv, (0, 1, 3, 4, 5, 2))
            win_t = jnp.transpose(state_nsa_win, (0, 1, 3, 4, 5, 2))
            w_row = jnp.concatenate([w_in2[:, :1024 + 2 * kvw], w_in2[:, 1024 + 6 * kvw:]], axis=1)
            w_kvt = w_in[:, 1024:1024 + 6 * kvw].T
            rowp, kvt, kvt16 = nsa_in_proj_prompt(xp.reshape(bp, lp, d_model), g, w_row, w_kvt, WIDE_ROWS)
            pre_p = cmp_pre_prompt(rowp, w_pair, CMP_ROWS)
            cmp_p = jnp.pad(cmp_finish(pre_p, pe8, w1, w2d), ((0, 0), (0, 0), (0, 0), (CMP_FRONT, CMP_BACK), (0, 0)))
            amat_p = _importance_matrix(lp // CMP_STRIDE + CMP_FRONT + CMP_BACK, CMP_FRONT)
            o_p = attn_prompt(rowp, cmp_p, kvt16, tp, amat_p)
            xp = matmul_residual(o_p.reshape(bp * lp, -1), w_out, xp)
            kvt6 = kvt.reshape(bp, 6, ATT_KV, ATT_HEAD_DIM, lp)
            outs["kv_p"].append(jnp.transpose(kvt6[:, 0:4], (0, 4, 1, 2, 3)))
            outs["win_p"].append(jnp.transpose(kvt6[:, 4:6, :, :, lp - WINDOW:], (0, 4, 1, 2, 3)))
            proj_s = norm_matmul(xs, g, w_in).reshape(bs, ls, -1)
            new_page = jnp.pad(proj_s[:, :, 1024:1024 + 2 * kvw], ((0, 0), (0, PAGE_SIZE - ls), (0, 0)))
            pre_s = cmp_pre_sample(page_table, cache_t, li, new_page, w_pair)
            cmp_s = cmp_finish(pre_s, pe8, w1, w2d)
            amat_s = _importance_matrix(256, 0)
            o_s = attn_sample(page_table, cache_t, win_t, li, proj_s, cmp_s, ts, amat_s, past_len)
            xs = matmul_residual(o_s.reshape(bs * ls, -1), w_out, xs)
            kv6s = proj_s[:, :, 1024:2560].reshape(bs, ls, 6, ATT_KV, ATT_HEAD_DIM)
            outs["kv_s"].append(kv6s[:, :, 0:4])
            new_win_t = jnp.concatenate([win_t[li][..., ls:], jnp.transpose(kv6s[:, :, 4:6], (0, 2, 3, 4, 1))], axis=-1)
            outs["win_s"].append(jnp.transpose(new_win_t, (0, 4, 1, 2, 3)))
        last = i == depth - 1
        xp = sqrelu_mlp_residual(xp, norm_ffn[i][None], ffn_w_up[i].astype(BF16), ffn_w_down[i].astype(BF16),
                                 norm_out[None], last)
        xs = sqrelu_mlp_residual(xs, norm_ffn[i][None], ffn_w_up[i].astype(BF16), ffn_w_down[i].astype(BF16),
                                 norm_out[None], last)
    st = lambda k: jnp.stack(outs[k])
    return (xp.reshape(bp, lp, d_model), xs.reshape(bs, ls, d_model), st("kv_p"), st("kv_s"), st("win_p"),
            st("win_s"), st("ssm_p"), st("ssm_s"), st("conv_p"), st("conv_s"), st("pool_p"), st("pool_s"))
```

```python
import functools
import math

import numpy as np
import jax
import jax.numpy as jnp
from jax import lax
from jax.experimental import pallas as pl
from jax.experimental.pallas import tpu as pltpu

F32 = jnp.float32
BF16 = jnp.bfloat16
HIGHEST = lax.Precision.HIGHEST
EPS = 1e-6
NEG_INF = float("-inf")
LOG2E = math.log2(math.e)

V7X_VMEM_LIMIT_BYTES = 56 * 1024 * 1024
LANES = 128
SUBLANES = 8

D_MODEL = 1024
SSD_HEAD_DIM = 64
SSD_GROUPS = 4
SSD_STATE = 128
SSD_CONV = 4
SSD_CHUNK = 128
POOL_WINDOWS = (2, 4, 8, 16)
POOL_HALO = 16
ATT_HEADS = 16
ATT_HEAD_DIM = 64
ATT_KV = 4
ATT_HG = 4
CMP_BLOCK = 32
CMP_STRIDE = 16
CMP_HIDDEN = 128
SEL_BLOCK = 64
N_SEL = 8
N_LOCAL = 2
FORCE_BONUS = 1000.0
WINDOW = 512
Q_TILE = 128
PAGE_SIZE = 128
N_BUCKETS = 32
MAX_DISTANCE = 128
CMP_FRONT = 112
CMP_BACK = 16
FAR_KEYS = 1024

MM_ROWS = 1024
MM_COLS = (2048, 1536, 1024, 512)
MLP_FF = 1024
WIDE_ROWS = 512
POOL_ROWS = 512
CMP_ROWS = 2048
SAMPLE_SEQS_PER_STEP = 1


def _cparams(*sem):
    return pltpu.CompilerParams(dimension_semantics=sem, vmem_limit_bytes=V7X_VMEM_LIMIT_BYTES)


def _bucket_thresholds():
    d = np.arange(0, MAX_DISTANCE + 1)
    max_exact = N_BUCKETS // 2
    nf = np.maximum(d, 1).astype(np.float32)
    large = max_exact + (np.log(nf / np.float32(max_exact)) / np.float32(math.log(MAX_DISTANCE / max_exact))
                         * np.float32(N_BUCKETS - max_exact)).astype(np.int32)
    large = np.minimum(large, N_BUCKETS - 1)
    b = np.where(d < max_exact, d, large)
    return [int(np.argmax(b >= k)) for k in range(N_BUCKETS)]


BUCKET_THR = _bucket_thresholds()


def _rms(x, g):
    return x * lax.rsqrt(jnp.mean(x * x, axis=-1, keepdims=True) + EPS) * g


def _sigmoid(x):
    return 0.5 * jnp.tanh(0.5 * x) + 0.5


def _silu(x):
    return x * _sigmoid(x)


def _softplus(x):
    return jnp.maximum(x, 0.0) + jnp.log1p(jnp.exp(-jnp.abs(x)))


def _dot(a, b):
    return jnp.dot(a, b, preferred_element_type=F32)


def _dot_nt(a, b):
    return lax.dot_general(a, b, (((1,), (1,)), ((), ())), preferred_element_type=F32)


def _dot_exact(a, b):
    return jnp.dot(a, b, precision=HIGHEST, preferred_element_type=F32)


def _dot_split3(a, b_bf16):
    hi = a.astype(BF16)
    r1 = a - hi.astype(F32)
    mid = r1.astype(BF16)
    lo = (r1 - mid.astype(F32)).astype(BF16)
    return _dot(hi, b_bf16) + _dot(mid, b_bf16) + _dot(lo, b_bf16)


def _norm_mm_kernel(x_ref, g_ref, w_ref, o_ref, h_scr):
    @pl.when(pl.program_id(1) == 0)
    def _():
        h_scr[...] = _rms(x_ref[...], g_ref[...]).astype(BF16)

    o_ref[...] = _dot(h_scr[...], w_ref[...])


def norm_matmul(x, g, w):
    T, D = x.shape
    N = w.shape[1]
    tm = min(T, MM_ROWS)
    tn = next(c for c in MM_COLS if N % c == 0)
    return pl.pallas_call(
        _norm_mm_kernel,
        grid=(T // tm, N // tn),
        in_specs=[pl.BlockSpec((tm, D), lambda i, j: (i, 0)),
                  pl.BlockSpec((1, D), lambda i, j: (0, 0)),
                  pl.BlockSpec((D, tn), lambda i, j: (0, j))],
        out_specs=pl.BlockSpec((tm, tn), lambda i, j: (i, j)),
        out_shape=jax.ShapeDtypeStruct((T, N), F32),
        scratch_shapes=[pltpu.VMEM((tm, D), BF16)],
        compiler_params=_cparams("parallel", "arbitrary"),
        name="norm_matmul",
    )(x, g, w)


def _mlp_kernel(x_ref, g_ref, wu_ref, wd_ref, go_ref, o_ref, h_scr, acc, *, final_norm):
    j = pl.program_id(1)

    @pl.when(j == 0)
    def _():
        h_scr[...] = _rms(x_ref[...], g_ref[...]).astype(BF16)
        acc[...] = jnp.zeros_like(acc)

    a = jnp.maximum(_dot(h_scr[...], wu_ref[...]), 0.0)
    acc[...] += _dot((a * a).astype(BF16), wd_ref[...])

    @pl.when(j == pl.num_programs(1) - 1)
    def _():
        y = x_ref[...] + acc[...]
        o_ref[...] = _rms(y, go_ref[...]) if final_norm else y


def sqrelu_mlp_residual(x, g, w_up, w_down, g_out, final_norm):
    T, D = x.shape
    F = w_up.shape[1]
    tm = min(T, MM_ROWS)
    tf = MLP_FF
    return pl.pallas_call(
        functools.partial(_mlp_kernel, final_norm=final_norm),
        grid=(T // tm, F // tf),
        in_specs=[pl.BlockSpec((tm, D), lambda i, j: (i, 0)),
                  pl.BlockSpec((1, D), lambda i, j: (0, 0)),
                  pl.BlockSpec((D, tf), lambda i, j: (0, j)),
                  pl.BlockSpec((tf, D), lambda i, j: (j, 0)),
                  pl.BlockSpec((1, D), lambda i, j: (0, 0))],
        out_specs=pl.BlockSpec((tm, D), lambda i, j: (i, 0)),
        out_shape=jax.ShapeDtypeStruct((T, D), F32),
        scratch_shapes=[pltpu.VMEM((tm, D), BF16), pltpu.VMEM((tm, D), F32)],
        compiler_params=_cparams("parallel", "arbitrary"),
        name="sqrelu_mlp",
    )(x, g, w_up, w_down, g_out)


def _mm_res_kernel(a_ref, w_ref, r_ref, o_ref):
    o_ref[...] = r_ref[...] + _dot(a_ref[...].astype(BF16), w_ref[...])


def matmul_residual(a, w, res):
    T, K = a.shape
    D = w.shape[1]
    tm = min(T, MM_ROWS)
    return pl.pallas_call(
        _mm_res_kernel,
        grid=(T // tm,),
        in_specs=[pl.BlockSpec((tm, K), lambda i: (i, 0)),
                  pl.BlockSpec((K, D), lambda i: (0, 0)),
                  pl.BlockSpec((tm, D), lambda i: (i, 0))],
        out_specs=pl.BlockSpec((tm, D), lambda i: (i, 0)),
        out_shape=jax.ShapeDtypeStruct((T, D), F32),
        compiler_params=_cparams("parallel"),
        name="matmul_residual",
    )(a, w, res)


def _pool_kernel(x_ref, halo_ref, g_ref, w_ref, sc_ref, o_ref, tail_ref, buf, *, start, tm):
    l = pl.program_id(1)

    @pl.when(l == 0)
    def _():
        buf[0:POOL_HALO] = halo_ref[0]

    x = x_ref[0]
    h = _rms(x, g_ref[...])
    buf[POOL_HALO:POOL_HALO + tm] = h
    pos = start + l * tm + lax.broadcasted_iota(jnp.int32, (tm, 1), 0)
    gc = x.shape[1] // len(POOL_WINDOWS)
    parts = []
    for gi, w in enumerate(POOL_WINDOWS):
        lo, hi = gi * gc, (gi + 1) * gc
        tot = buf[POOL_HALO:POOL_HALO + tm, lo:hi]
        for k in range(1, w):
            tot = tot + buf[POOL_HALO - k:POOL_HALO - k + tm, lo:hi]
        inv_cnt = 1.0 / jnp.minimum(pos + 1, w).astype(F32)
        diff = tot * inv_cnt - h[:, lo:hi]
        parts.append(_dot(diff.astype(BF16), w_ref[gi]))
    y = jnp.concatenate(parts, axis=1) * sc_ref[...]
    o_ref[0] = x + y
    t = buf[tm:tm + POOL_HALO]
    tail_ref[0] = t
    buf[0:POOL_HALO] = t


def pool_mixer_residual(x, halo, g, w_grp, scale, start, tm):
    B, L, D = x.shape
    return pl.pallas_call(
        functools.partial(_pool_kernel, start=start, tm=tm),
        grid=(B, L // tm),
        in_specs=[pl.BlockSpec((1, tm, D), lambda b, l: (b, l, 0)),
                  pl.BlockSpec((1, POOL_HALO, D), lambda b, l: (b, 0, 0)),
                  pl.BlockSpec((1, D), lambda b, l: (0, 0)),
                  pl.BlockSpec(w_grp.shape, lambda b, l: (0, 0, 0)),
                  pl.BlockSpec((1, D), lambda b, l: (0, 0))],
        out_specs=[pl.BlockSpec((1, tm, D), lambda b, l: (b, l, 0)),
                   pl.BlockSpec((1, POOL_HALO, D), lambda b, l: (b, 0, 0))],
        out_shape=[jax.ShapeDtypeStruct((B, L, D), F32), jax.ShapeDtypeStruct((B, POOL_HALO, D), F32)],
        scratch_shapes=[pltpu.VMEM((POOL_HALO + tm, D), F32)],
        compiler_params=_cparams("parallel", "arbitrary"),
        name="pool_mixer",
    )(x, halo, g, w_grp, scale)


def _ssd_pre_kernel(xbc_ref, dtr_ref, c0_ref, cw_ref, cb_ref, dtb_ref, xc_ref, dt_ref, tail_ref, ext, *, tm):
    @pl.when(pl.program_id(1) == 0)
    def _():
        ext[0:SUBLANES] = c0_ref[0]

    ext[SUBLANES:SUBLANES + tm] = xbc_ref[0]
    u = cb_ref[...]
    for k in range(SSD_CONV):
        off = SUBLANES - (SSD_CONV - 1) + k
        u = u + ext[off:off + tm] * cw_ref[k:k + 1]
    xc_ref[0] = _silu(u)
    t = ext[tm:tm + SUBLANES]
    tail_ref[0] = t
    ext[0:SUBLANES] = t
    dt_ref[0] = _softplus(dtr_ref[0] + dtb_ref[...])


def ssd_pre(proj, conv0, conv_w, conv_b, dt_bias, tm):
    B, L, _ = proj.shape
    C = conv_w.shape[1]
    NDT = dt_bias.shape[1]
    return pl.pallas_call(
        functools.partial(_ssd_pre_kernel, tm=tm),
        grid=(B, L // tm),
        in_specs=[pl.BlockSpec((1, tm, C), lambda b, l: (b, l, 1)),
                  pl.BlockSpec((1, tm, NDT), lambda b, l: (b, l, 4)),
                  pl.BlockSpec((1, SUBLANES, C), lambda b, l: (b, 0, 0)),
                  pl.BlockSpec((SSD_CONV, C), lambda b, l: (0, 0)),
                  pl.BlockSpec((1, C), lambda b, l: (0, 0)),
                  pl.BlockSpec((1, NDT), lambda b, l: (0, 0))],
        out_specs=[pl.BlockSpec((1, tm, C), lambda b, l: (b, l, 0)),
                   pl.BlockSpec((1, tm, NDT), lambda b, l: (b, l, 0)),
                   pl.BlockSpec((1, SUBLANES, C), lambda b, l: (b, 0, 0))],
        out_shape=[jax.ShapeDtypeStruct((B, L, C), F32), jax.ShapeDtypeStruct((B, L, NDT), F32),
                   jax.ShapeDtypeStruct((B, SUBLANES, C), F32)],
        scratch_shapes=[pltpu.VMEM((SUBLANES + tm, C), F32)],
        compiler_params=_cparams("parallel", "arbitrary"),
        name="ssd_pre",
    )(proj, proj, conv0, conv_w, conv_b, dt_bias)


def _head_expand_matrix():
    h = np.arange(LANES)[:, None]
    lane = np.arange(8 * SSD_HEAD_DIM)[None, :]
    return jnp.asarray((lane // SSD_HEAD_DIM == h).astype(np.float32)).astype(BF16)


def _ssd_scan_kernel(xh_ref, b_ref, c_ref, dt_ref, alog_ref, tri_ref, e_ref, sin_ref, y_ref, sout_ref, *, seg, gps):
    @pl.when(pl.program_id(2) == 0)
    def _():
        sout_ref[...] = sin_ref[0]

    Q = SSD_CHUNK
    nseg = Q // seg
    hpg = 8
    HD = hpg * SSD_HEAD_DIM
    tri = tri_ref[...]
    causal = tri > 0.5
    low = lax.broadcasted_iota(jnp.int32, (Q, LANES), 1) < SSD_HEAD_DIM
    groups = []
    for gg in range(gps):
        glanes = slice(gg * LANES, (gg + 1) * LANES)
        d = dict(gg=gg,
                 xh=xh_ref[0, :, gg * HD:(gg + 1) * HD],
                 bg=b_ref[0, :, glanes].astype(BF16),
                 cg=c_ref[0, :, glanes],
                 dt=dt_ref[0, :, glanes])
        d["acum"] = _dot_exact(tri, d["dt"] * -jnp.exp(alog_ref[gg]))
        d["cb"] = _dot_nt(d["cg"].astype(BF16), d["bg"])
        groups.append(d)
    for d in groups:
        acum, acum_t, dt_t = d["acum"], d["acum"].T, d["dt"].T
        d["wts"] = [(d["cb"] * jnp.exp(jnp.where(causal, acum[:, h:h + 1] - acum_t[h:h + 1, :], NEG_INF))
                     * dt_t[h:h + 1, :]).astype(BF16) for h in range(hpg)]
    for d in groups:
        ys = []
        for p in range(hpg // 2):
            xpair = d["xh"][:, LANES * p:LANES * (p + 1)]
            ys.append(_dot(d["wts"][2 * p], jnp.where(low, xpair, 0.0).astype(BF16))
                      + _dot(d["wts"][2 * p + 1], jnp.where(low, 0.0, xpair).astype(BF16)))
        d["y_intra"] = jnp.concatenate(ys, axis=1)
    for d in groups:
        acum = d["acum"]
        if nseg == 1:
            alast = jnp.broadcast_to(acum[Q - 1:Q, :], (Q, LANES))
        else:
            r = lax.broadcasted_iota(jnp.int32, (Q, Q), 0)
            s = lax.broadcasted_iota(jnp.int32, (Q, Q), 1)
            alast = _dot_exact((s == r - (r & (seg - 1)) + (seg - 1)).astype(F32), acum)
        d["alast"] = alast
        d["xw_t"] = (d["xh"] * _dot_split3(jnp.exp(alast - acum) * d["dt"], e_ref[...])).T
        d["eac"] = _dot_split3(jnp.exp(acum), e_ref[...])
    col = lax.broadcasted_iota(jnp.int32, (HD, Q), 1)
    for d in groups:
        gg, alast = d["gg"], d["alast"]
        y_parts = []
        for si in range(nseg):
            r0 = si * seg
            h0 = sout_ref[0, si, gg * hpg:(gg + 1) * hpg].reshape(HD, SSD_STATE)
            y_parts.append(_dot_nt(d["cg"][r0:r0 + seg], h0))
            xm = d["xw_t"] if nseg == 1 else jnp.where((col >= r0) & (col < r0 + seg), d["xw_t"], 0.0)
            s_new = _dot(xm.astype(BF16), d["bg"])
            for h in range(hpg):
                rows = slice(SSD_HEAD_DIM * h, SSD_HEAD_DIM * (h + 1))
                sout_ref[0, si, gg * hpg + h] = h0[rows] * jnp.exp(alast[r0:r0 + 1, h:h + 1]) + s_new[rows]
        y_inter = y_parts[0] if nseg == 1 else jnp.concatenate(y_parts, axis=0)
        y_ref[0, :, gg * HD:(gg + 1) * HD] = d["y_intra"] + y_inter * d["eac"]


def ssd_scan(xc, dt, a_log, tri, state0, li, seg):
    NB, R, _ = xc.shape
    Q = SSD_CHUNK
    nseg = Q // seg
    NC = R // Q
    G = SSD_GROUPS
    HD = 8 * SSD_HEAD_DIM
    gps = G if nseg == 1 else 1
    x_blocks = (G * HD) // (gps * SSD_STATE)
    return pl.pallas_call(
        functools.partial(_ssd_scan_kernel, seg=seg, gps=gps),
        grid=(NB, G // gps, NC),
        in_specs=[pl.BlockSpec((1, Q, gps * HD), lambda b, g, c: (b, c, g)),
                  pl.BlockSpec((1, Q, gps * SSD_STATE), lambda b, g, c: (b, c, x_blocks + g)),
                  pl.BlockSpec((1, Q, gps * SSD_STATE), lambda b, g, c: (b, c, x_blocks + G // gps + g)),
                  pl.BlockSpec((1, Q, gps * LANES), lambda b, g, c: (b, c, g)),
                  pl.BlockSpec((gps, 1, LANES), lambda b, g, c: (g, 0, 0)),
                  pl.BlockSpec((Q, Q), lambda b, g, c: (0, 0)),
                  pl.BlockSpec((LANES, HD), lambda b, g, c: (0, 0)),
                  pl.BlockSpec((1, 1, nseg, gps * 8, SSD_HEAD_DIM, SSD_STATE), lambda b, g, c: (li, b, 0, g, 0, 0))],
        out_specs=[pl.BlockSpec((1, Q, gps * HD), lambda b, g, c: (b, c, g)),
                   pl.BlockSpec((1, nseg, gps * 8, SSD_HEAD_DIM, SSD_STATE), lambda b, g, c: (b, 0, g, 0, 0))],
        out_shape=[jax.ShapeDtypeStruct((NB, R, G * HD), F32), jax.ShapeDtypeStruct(state0.shape[1:], F32)],
        compiler_params=_cparams("parallel", "parallel", "arbitrary"),
        name="ssd_scan",
    )(xc, xc, xc, dt, a_log, tri, _head_expand_matrix(), state0)


def _ssd_post_kernel(y_ref, xh_ref, z_ref, d_ref, g_ref, w_ref, r_ref, o_ref):
    y = (y_ref[...] + xh_ref[...] * d_ref[...]) * _silu(z_ref[...])
    gw = y.shape[1] // SSD_GROUPS
    parts = []
    for gi in range(SSD_GROUPS):
        parts.append(_rms(y[:, gi * gw:(gi + 1) * gw], g_ref[:, gi * gw:(gi + 1) * gw]).astype(BF16))
    o_ref[...] = r_ref[...] + _dot(jnp.concatenate(parts, axis=1), w_ref[...])


def ssd_post(y, xc, proj, d_exp, norm_g, w_out, res):
    T, DI = y.shape
    D = w_out.shape[1]
    tm = min(T, WIDE_ROWS)
    return pl.pallas_call(
        _ssd_post_kernel,
        grid=(T // tm,),
        in_specs=[pl.BlockSpec((tm, DI), lambda i: (i, 0)),
                  pl.BlockSpec((tm, DI), lambda i: (i, 0)),
                  pl.BlockSpec((tm, DI), lambda i: (i, 0)),
                  pl.BlockSpec((1, DI), lambda i: (0, 0)),
                  pl.BlockSpec((1, DI), lambda i: (0, 0)),
                  pl.BlockSpec((DI, D), lambda i: (0, 0)),
                  pl.BlockSpec((tm, D), lambda i: (i, 0))],
        out_specs=pl.BlockSpec((tm, D), lambda i: (i, 0)),
        out_shape=jax.ShapeDtypeStruct((T, D), F32),
        compiler_params=_cparams("parallel"),
        name="ssd_post",
    )(y, xc, proj, d_exp, norm_g, w_out, res)


def _segment_tri(seg):
    r = np.arange(SSD_CHUNK)
    return jnp.asarray(((r[:, None] // seg == r[None, :] // seg) & (r[None, :] <= r[:, None])).astype(np.float32))


def ssd_layer(x, nb, conv_state, ssm_states, li, seg, w):
    T, D = x.shape
    L = T // nb
    proj = norm_matmul(x, w["norm"], w["w_in"])
    conv0 = jnp.pad(conv_state, ((0, 0), (SUBLANES - (SSD_CONV - 1), 0), (0, 0)))
    xc, dt, tail = ssd_pre(proj.reshape(nb, L, -1), conv0, w["conv_w"], w["conv_b"], w["dt_bias"],
                           min(L, SSD_CHUNK))
    rows = SSD_CHUNK if seg < SSD_CHUNK else L
    ngrp = T // rows
    nseg = SSD_CHUNK // seg
    st0 = ssm_states.reshape((ssm_states.shape[0], ngrp, nseg) + ssm_states.shape[2:])
    y, st = ssd_scan(xc.reshape(ngrp, rows, -1), dt.reshape(ngrp, rows, -1), w["a_log"], _segment_tri(seg), st0, li,
                     seg)
    x_new = ssd_post(y.reshape(T, -1), xc.reshape(T, -1), proj, w["d_exp"], w["norm_g"], w["w_out"], x)
    return x_new, tail[:, SUBLANES - (SSD_CONV - 1):], st.reshape(ssm_states.shape[1:])


def _bias_of(d, rb_ref, h):
    val = jnp.full(d.shape, rb_ref[0, h], F32)
    for k in range(1, N_BUCKETS):
        val = jnp.where(d >= BUCKET_THR[k], rb_ref[k, h], val)
    return jnp.where(d >= 0, val - rb_ref[N_BUCKETS - 1, h], 0.0)


def _bias_tables_kernel(rb_ref, tp_ref, ts_ref, *, past_len):
    r = lax.broadcasted_iota(jnp.int32, (Q_TILE, Q_TILE), 0)
    c = lax.broadcasted_iota(jnp.int32, (Q_TILE, Q_TILE), 1)
    t = lax.broadcasted_iota(jnp.int32, (SUBLANES, 1024), 0)
    j = lax.broadcasted_iota(jnp.int32, (SUBLANES, 1024), 1)
    win0 = past_len - WINDOW
    d_s = jnp.where(j < 256, past_len + t - (CMP_STRIDE * j + CMP_BLOCK - 1),
                    jnp.where(j < 384, past_len + t - (past_len - PAGE_SIZE + (j - 256)),
                              jnp.where(j < 512, t - (j - 384), past_len + t - (win0 + (j - 512)))))

    def body(h, carry):
        tp_ref[h, 0] = LOG2E * _bias_of(r - c, rb_ref, h)
        tp_ref[h, 1] = LOG2E * _bias_of(Q_TILE + r - c, rb_ref, h)
        tp_ref[h, 2] = LOG2E * _bias_of(r - CMP_STRIDE * (c - CMP_FRONT) - (CMP_BLOCK - 1), rb_ref, h)
        ts_ref[h] = _bias_of(d_s, rb_ref, h)
        return carry

    lax.fori_loop(0, ATT_HEADS, body, 0)


def bias_tables(rel_bias, past_len):
    return pl.pallas_call(
        functools.partial(_bias_tables_kernel, past_len=past_len),
        in_specs=[pl.BlockSpec(memory_space=pltpu.SMEM)],
        out_specs=[pl.BlockSpec(memory_space=pltpu.VMEM), pl.BlockSpec(memory_space=pltpu.VMEM)],
        out_shape=[jax.ShapeDtypeStruct((ATT_HEADS, 3, Q_TILE, Q_TILE), F32),
                   jax.ShapeDtypeStruct((ATT_HEADS, SUBLANES, 1024), F32)],
        name="bias_tables",
    )(rel_bias)


def _cmp_pre_kernel(*refs, nsrc, nsub, n_prefetch, n_native):
    refs = refs[n_prefetch:]
    srcs = refs[:nsrc]
    w_ref, o_ref, stage = refs[nsrc], refs[nsrc + 1], refs[nsrc + 2]
    rows = nsub * CMP_STRIDE
    n = nsrc * nsub
    for f in range(2):
        for pair in range(2):
            slot = 2 * f + pair
            l0 = f * 256 + pair * LANES
            for si, src in enumerate(srcs):
                if si < n_native:
                    stage[slot, si * rows:(si + 1) * rows] = src[0, 0, f, 2 * pair:2 * pair + 2].reshape(LANES, rows).T
                else:
                    stage[slot, si * rows:(si + 1) * rows] = src[0, :, l0:l0 + LANES]
    for f in range(2):
        xs = [jnp.concatenate([stage[2 * f + pair, pl.ds(s, n, stride=CMP_STRIDE)] for s in range(CMP_STRIDE)], axis=1)
              for pair in range(2)]
        pre = _dot(jnp.concatenate(xs, axis=0).astype(BF16), w_ref[f])
        for pair in range(2):
            o_ref[0, f, 2 * pair] = pre[pair * n:(pair + 1) * n, 0:256]
            o_ref[0, f, 2 * pair + 1] = pre[pair * n:(pair + 1) * n, 256:512]


def _cmp_fin_kernel(pre_ref, pe_ref, w1_ref, w2_ref, o_ref, scr, *, n):
    slots = [(f, kv) for f in range(2) for kv in range(ATT_KV)]
    cvec = [_dot(pe_ref[f].astype(BF16), w1_ref[f])[0:1] for f in range(2)]
    for slot, (f, kv) in enumerate(slots):
        scr[slot, 0:n] = pre_ref[0, f, kv, :, CMP_HIDDEN:2 * CMP_HIDDEN]
        scr[slot, n:n + SUBLANES] = jnp.zeros((SUBLANES, CMP_HIDDEN), F32)
    acts = [_silu(pre_ref[0, f, kv, :, 0:CMP_HIDDEN] + scr[slot, pl.ds(1, n)] + cvec[f]).astype(BF16)
            for slot, (f, kv) in enumerate(slots)]
    for (f, kv), a in zip(slots, acts):
        o_ref[0, f, kv] = _dot(a, w2_ref[f])


def cmp_finish(pre, pe8, w1, w2d):
    B, _, KV, n, _ = pre.shape
    return pl.pallas_call(
        functools.partial(_cmp_fin_kernel, n=n),
        grid=(B,),
        in_specs=[pl.BlockSpec((1, 2, KV, n, 256), lambda b: (b, 0, 0, 0, 0)),
                  pl.BlockSpec(pe8.shape, lambda b: (0, 0, 0)),
                  pl.BlockSpec(w1.shape, lambda b: (0, 0, 0)),
                  pl.BlockSpec(w2d.shape, lambda b: (0, 0, 0))],
        out_specs=pl.BlockSpec((1, 2, KV, n, LANES), lambda b: (b, 0, 0, 0, 0)),
        out_shape=jax.ShapeDtypeStruct((B, 2, KV, n, LANES), F32),
        scratch_shapes=[pltpu.VMEM((2 * KV, n + SUBLANES, CMP_HIDDEN), F32)],
        compiler_params=_cparams("parallel"),
        name="cmp_finish",
    )(pre, pe8, w1, w2d)


def cmp_pre_prompt(proj, w_pair, rows):
    B, L, _ = proj.shape
    nsub = rows // CMP_STRIDE
    return pl.pallas_call(
        functools.partial(_cmp_pre_kernel, nsrc=1, nsub=nsub, n_prefetch=0, n_native=0),
        grid=(B, L // rows),
        in_specs=[pl.BlockSpec((1, rows, 512), lambda b, t: (b, t, 2)),
                  pl.BlockSpec(w_pair.shape, lambda b, t: (0, 0, 0))],
        out_specs=pl.BlockSpec((1, 2, ATT_KV, nsub, 256), lambda b, t: (b, 0, 0, t, 0)),
        out_shape=jax.ShapeDtypeStruct((B, 2, ATT_KV, L // CMP_STRIDE, 256), F32),
        scratch_shapes=[pltpu.VMEM((4, rows, LANES), F32)],
        compiler_params=_cparams("parallel", "parallel"),
        name="cmp_pre_prompt",
    )(proj, w_pair)


def cmp_pre_sample(page_table, cache_t, li, new_page, w_pair):
    B, n_pages = page_table.shape
    nsub = PAGE_SIZE // CMP_STRIDE
    nsrc = n_pages + 1

    def page_spec(p):
        return pl.BlockSpec((1, 1, 2, ATT_KV, ATT_HEAD_DIM, PAGE_SIZE), lambda b, pt: (li, pt[b, p], 0, 0, 0, 0))

    grid_spec = pltpu.PrefetchScalarGridSpec(
        num_scalar_prefetch=1,
        grid=(B,),
        in_specs=[page_spec(p) for p in range(n_pages)]
        + [pl.BlockSpec((1, PAGE_SIZE, 512), lambda b, pt: (b, 0, 0)),
           pl.BlockSpec(w_pair.shape, lambda b, pt: (0, 0, 0))],
        out_specs=pl.BlockSpec((1, 2, ATT_KV, nsrc * nsub, 256), lambda b, pt: (b, 0, 0, 0, 0)),
        scratch_shapes=[pltpu.VMEM((4, nsrc * PAGE_SIZE, LANES), F32)],
    )
    return pl.pallas_call(
        functools.partial(_cmp_pre_kernel, nsrc=nsrc, nsub=nsub, n_prefetch=1, n_native=n_pages),
        grid_spec=grid_spec,
        out_shape=jax.ShapeDtypeStruct((B, 2, ATT_KV, nsrc * nsub, 256), F32),
        compiler_params=_cparams("parallel"),
        name="cmp_pre_sample",
    )(page_table, *([cache_t] * n_pages), new_page, w_pair)


def _importance_matrix(n_rows, front):
    m = np.arange(n_rows)[:, None] - front
    j = np.arange(LANES)[None, :]
    ratio = SEL_BLOCK // CMP_STRIDE
    a = ((m >= ratio * j) & (m <= ratio * j + ratio - 1)).astype(np.float32) \
        + ((m >= ratio * j - 1) & (m <= ratio * j + ratio - 2)).astype(np.float32)
    return jnp.asarray(a)


def _softmax_rows(s):
    m = jnp.max(s, axis=-1, keepdims=True)
    m = jnp.where(m > NEG_INF, m, 0.0)
    e = jnp.exp(s - m)
    return e / jnp.maximum(jnp.sum(e, axis=-1, keepdims=True), 1e-30)


def _top_blocks(score, n_sel, axis=1):
    jb = lax.broadcasted_iota(jnp.int32, score.shape, axis).astype(F32)
    sel = jnp.zeros(score.shape, F32)
    sc = score
    for _ in range(n_sel):
        mx = jnp.max(sc, axis=axis, keepdims=True)
        idx = jnp.min(jnp.where(sc == mx, jb, 1e9), axis=axis, keepdims=True)
        pick = jb == idx
        sel = jnp.where(pick, 1.0, sel)
        sc = jnp.where(pick, NEG_INF, sc)
    return sel


def _tile4(x):
    return jnp.concatenate([x, x, x, x], axis=0)


def _attn_prompt_kernel(q_ref, gate_ref, kc_ref, vc_ref, ks_ref, vs_ref, kw_ref, vw_ref, tp_ref, amat_ref, o_ref,
                        mx_scr, acc_scr, *, nsub):
    i = pl.program_id(2)
    QT = Q_TILE
    R2 = 2 * QT
    low = lax.broadcasted_iota(jnp.int32, (QT, LANES), 1) < ATT_HEAD_DIM
    qpp = jnp.concatenate([q_ref[0, :, 0:LANES], q_ref[0, :, LANES:2 * LANES]], axis=0).astype(BF16)
    r1 = lax.broadcasted_iota(jnp.int32, (QT, 1), 0)
    c1 = lax.broadcasted_iota(jnp.int32, (1, LANES), 1)
    lower = r1 >= c1
    upper = r1 <= c1
    i_vec = jnp.full((QT, LANES), i, jnp.int32)

    def half_masks(n):
        lo = lax.broadcasted_iota(jnp.int32, (n, LANES), 1) < ATT_HEAD_DIM
        return jnp.where(lo, 1.0, 0.0).astype(BF16), jnp.where(lo, 0.0, 1.0).astype(BF16)

    def bd(x2):
        m_lo, m_hi = half_masks(x2.shape[0])
        return jnp.concatenate([x2 * m_lo, x2 * m_hi], axis=0)

    def spread(madd):
        m2 = jnp.concatenate([madd, madd], axis=1)
        return jnp.concatenate([m2, m2], axis=0)

    def bias_full(kind):
        return jnp.concatenate([jnp.concatenate([tp_ref[2 * p, kind], tp_ref[2 * p + 1, kind]], axis=1)
                                for p in range(2)], axis=0)

    def row_max2(x):
        parts = []
        for w in range(2):
            m = jnp.max(x[:, LANES * w:LANES * (w + 1)], axis=-1, keepdims=True)
            parts.append(jnp.broadcast_to(jnp.where(m > NEG_INF, m, 0.0), (R2, LANES)))
        return jnp.concatenate(parts, axis=1)

    def widen(m2, reps):
        if reps == 1:
            return m2
        return jnp.concatenate([m2[:, :LANES]] * reps + [m2[:, LANES:]] * reps, axis=1)

    def cols(ref, key0, n):
        return ref[0, :, pl.ds(pl.multiple_of(key0, LANES), n)]

    def bd_t(xt):
        z = jnp.zeros_like(xt)
        return jnp.concatenate([jnp.concatenate([xt, z], axis=0), jnp.concatenate([z, xt], axis=0)], axis=1)

    def scores(kt, add):
        return _dot(qpp, bd_t(kt)) + add

    def scores_nt(k2, add):
        return _dot_nt(qpp, bd(k2)) + add

    def weighted(s, m_wide, vt):
        n = vt.shape[1]
        same = jnp.right_shift(lax.broadcasted_iota(jnp.int32, (LANES, 2 * n), 0), ATT_HEAD_DIM.bit_length() - 1) \
            == (lax.broadcasted_iota(jnp.int32, (LANES, 2 * n), 1) >= n).astype(jnp.int32)
        rhs_t = jnp.concatenate([bd_t(vt), jnp.where(same, 1.0, 0.0).astype(BF16)], axis=0)
        return _dot_nt(jnp.exp2(s - m_wide).astype(BF16), rhs_t)

    def normalized(acc):
        return acc[:, :LANES] / jnp.maximum(acc[:, LANES:], 1e-30)

    nback = WINDOW // QT
    ws, wt = [], []
    for back in range(nback, -1, -1):
        tc = jnp.maximum(i - back, 0)
        ok = i_vec >= back
        if back == nback:
            ok = ok & upper
        if back == 0:
            ok = lower
        madd = spread(jnp.where(ok, 0.0, NEG_INF))
        ws.append(scores(cols(kw_ref, tc * QT, QT), madd + bias_full(back) if back <= 1 else madd))
        wt.append(tc)
    mel = ws[0]
    for s in ws[1:]:
        mel = jnp.maximum(mel, s)
    m2_w = row_max2(mel)
    acc_w = None
    for s, tc in zip(ws, wt):
        term = weighted(s, m2_w, cols(vw_ref, tc * QT, QT))
        acc_w = term if acc_w is None else acc_w + term
    o_w = normalized(acc_w)

    st = pl.multiple_of(i * SUBLANES, SUBLANES)
    d_cmp = r1 - CMP_STRIDE * (c1 - CMP_FRONT) - (CMP_BLOCK - 1)
    ctiles = []
    for j in range(nsub // QT):
        rows = slice(CMP_FRONT + j * QT, CMP_FRONT + (j + 1) * QT)
        ok = jnp.broadcast_to(j * QT + c1 < i * SUBLANES - CMP_FRONT, (QT, LANES))
        ctiles.append((kc_ref[0, 0, 0, rows, :], vc_ref[0, 0, 0, rows, :], amat_ref[rows, :], ok, False))
    ctiles.append((kc_ref[0, 0, 0, pl.ds(st, QT), :], vc_ref[0, 0, 0, pl.ds(st, QT), :], amat_ref[pl.ds(st, QT), :],
                   (d_cmp >= 0) & (c1 >= CMP_FRONT - i * SUBLANES), True))
    ss = []
    for kt, _, _, ok, diag in ctiles:
        add = spread(jnp.where(ok, 0.0, NEG_INF))
        ss.append(scores_nt(kt.astype(BF16), add + bias_full(2) if diag else add))
    mel = ss[0]
    for s in ss[1:]:
        mel = jnp.maximum(mel, s)
    m2 = row_max2(mel)
    es = [jnp.exp2(s - m2) for s in ss]
    lel = es[0]
    for e in es[1:]:
        lel = lel + e
    inv = []
    for w in range(2):
        l = jnp.sum(lel[:, LANES * w:LANES * (w + 1)], axis=-1, keepdims=True)
        inv.append(jnp.broadcast_to(1.0 / jnp.maximum(l, 1e-30), (R2, LANES)))
    inv2 = jnp.concatenate(inv, axis=1)
    o_c = None
    phs = []
    for e, (_, vt, _, _, _) in zip(es, ctiles):
        pn = e * inv2
        term = _dot(pn.astype(BF16), bd(vt.astype(BF16)))
        o_c = term if o_c is None else o_c + term
        phs.append(pn[:QT, :LANES] + pn[:QT, LANES:] + pn[QT:, :LANES] + pn[QT:, LANES:])
    imp = _dot_split3(jnp.concatenate(phs, axis=1),
                      jnp.concatenate([t[2] for t in ctiles], axis=0).astype(BF16))

    qblk = 2 * i + (c1 >= SEL_BLOCK).astype(jnp.int32)
    lag = qblk - r1
    allowed = lag >= 0
    forced = (r1 == 0) | (allowed & (lag < N_LOCAL))
    score_t = jnp.where(allowed, imp.T + jnp.where(forced, FORCE_BONUS, 0.0), -1.0)
    sel = _top_blocks(score_t, N_SEL, axis=0).T.astype(BF16)

    FK = FAR_KEYS
    tm1 = jnp.maximum(i - 1, 0)
    limit = tm1 * QT
    n_grp = jnp.right_shift(tm1 + FK // QT - 1, (FK // QT).bit_length() - 1)
    shift = SEL_BLOCK.bit_length() - 1

    def key_mask(key0, n):
        jb = lax.broadcasted_iota(jnp.int32, (LANES, n), 0)
        key = key0 + lax.broadcasted_iota(jnp.int32, (LANES, n), 1)
        hit = jb == jnp.right_shift(key, shift)
        if n == FK:
            hit = hit & (key < limit)
        return jnp.where(_dot(sel, jnp.where(hit, 1.0, 0.0).astype(BF16)) > 0.5, 0.0, NEG_INF)

    low2 = lax.broadcasted_iota(jnp.int32, (R2, LANES), 1) < ATT_HEAD_DIM

    def online_step(s, vt):
        n = vt.shape[1]
        reps = n // LANES
        halves = []
        for w in range(2):
            m = s[:, w * n:w * n + LANES]
            for c4 in range(1, reps):
                m = jnp.maximum(m, s[:, w * n + c4 * LANES:w * n + (c4 + 1) * LANES])
            halves.append(jnp.broadcast_to(jnp.max(m, axis=-1, keepdims=True), (R2, LANES)))
        m_old = mx_scr[...]
        m_new = jnp.maximum(m_old, jnp.concatenate(halves, axis=1))
        m_safe = jnp.where(m_new > NEG_INF, m_new, 0.0)
        alpha = jnp.exp2(m_old - m_safe)
        a_mix = jnp.where(low2, alpha[:, :LANES], alpha[:, LANES:])
        mx_scr[...] = m_new
        acc_scr[...] = acc_scr[...] * jnp.concatenate([a_mix, a_mix], axis=1) + weighted(s, widen(m_safe, reps), vt)

    mx_scr[...] = jnp.full(mx_scr.shape, NEG_INF, F32)
    acc_scr[...] = jnp.zeros_like(acc_scr)

    def far_body(gi, c):
        s = scores(cols(ks_ref, gi * FK, FK), spread(key_mask(gi * FK, FK)))
        online_step(s, cols(vs_ref, gi * FK, FK))
        return c

    lax.fori_loop(0, n_grp, far_body, 0)
    mk_s = spread(jnp.where(i_vec >= 1, key_mask(tm1 * QT, QT), NEG_INF))
    online_step(scores(cols(ks_ref, tm1 * QT, QT), mk_s + bias_full(1)), cols(vs_ref, tm1 * QT, QT))
    mk_d = spread(jnp.where(lower, key_mask(i * QT, QT), NEG_INF))
    online_step(scores(cols(ks_ref, i * QT, QT), mk_d + bias_full(0)), cols(vs_ref, i * QT, QT))
    o_s = normalized(acc_scr[...])

    g = _sigmoid(gate_ref[0])

    def gate(br):
        tiles = []
        for p in range(2):
            c0 = br * ATT_HG + 2 * p
            tiles.append(jnp.where(low, jnp.broadcast_to(g[:, c0:c0 + 1], (QT, LANES)),
                                   jnp.broadcast_to(g[:, c0 + 1:c0 + 2], (QT, LANES))))
        return jnp.concatenate(tiles, axis=0)

    o = gate(0) * o_c + gate(1) * o_s + gate(2) * o_w
    o_ref[0] = jnp.concatenate([o[:QT], o[QT:]], axis=1)


def _nsa_in_proj_kernel(x_ref, g_ref, wr_ref, wt_ref, row_ref, kvc_ref, kvw_ref, kvt16_ref):
    h = _rms(x_ref[0], g_ref[...]).astype(BF16)
    row_ref[0] = _dot(h, wr_ref[...])
    kv = _dot_nt(wt_ref[...], h)
    nc = kvc_ref.shape[1]
    kvc_ref[0] = kv[:nc]
    kvw_ref[0] = kv[nc:]
    kvt16_ref[0] = kv[kv.shape[0] - kvt16_ref.shape[1]:].astype(BF16)


def nsa_in_proj_prompt(x, g, w_row, w_kvt, tm):
    B, L, D = x.shape
    NR, NT = w_row.shape[1], w_kvt.shape[0]
    n16 = 4 * ATT_KV * ATT_HEAD_DIM
    return pl.pallas_call(
        _nsa_in_proj_kernel,
        grid=(B, L // tm),
        in_specs=[pl.BlockSpec((1, tm, D), lambda b, l: (b, l, 0)),
                  pl.BlockSpec((1, D), lambda b, l: (0, 0)),
                  pl.BlockSpec((D, NR), lambda b, l: (0, 0)),
                  pl.BlockSpec((NT, D), lambda b, l: (0, 0))],
        out_specs=[pl.BlockSpec((1, tm, NR), lambda b, l: (b, l, 0)),
                   pl.BlockSpec((1, n16, tm), lambda b, l: (b, 0, l)),
                   pl.BlockSpec((1, NT - n16, tm), lambda b, l: (b, 0, l)),
                   pl.BlockSpec((1, n16, tm), lambda b, l: (b, 0, l))],
        out_shape=[jax.ShapeDtypeStruct((B, L, NR), F32), jax.ShapeDtypeStruct((B, n16, L), F32),
                   jax.ShapeDtypeStruct((B, NT - n16, L), F32), jax.ShapeDtypeStruct((B, n16, L), BF16)],
        compiler_params=_cparams("parallel", "parallel"),
        name="nsa_in_proj_prompt",
    )(x, g, w_row, w_kvt)


def attn_prompt(rowp, cmp_kv, kvt16, tp, amat):
    B, L, _ = rowp.shape
    nsub = L // CMP_STRIDE
    ncp = cmp_kv.shape[3]

    def kv_spec(f):
        return pl.BlockSpec((1, ATT_HEAD_DIM, L), lambda b, k, i: (b, f * ATT_KV + k, 0))

    def cmp_spec(f):
        return pl.BlockSpec((1, 1, 1, ncp, LANES), lambda b, k, i: (b, f, k, 0, 0))

    return pl.pallas_call(
        functools.partial(_attn_prompt_kernel, nsub=nsub),
        grid=(B, ATT_KV, L // Q_TILE),
        in_specs=[pl.BlockSpec((1, Q_TILE, 256), lambda b, k, i: (b, i, k)),
                  pl.BlockSpec((1, Q_TILE, LANES), lambda b, k, i: (b, i, 12 + k)),
                  cmp_spec(0), cmp_spec(1), kv_spec(0), kv_spec(1), kv_spec(2), kv_spec(3),
                  pl.BlockSpec((ATT_HG, 3, Q_TILE, Q_TILE), lambda b, k, i: (k, 0, 0, 0)),
                  pl.BlockSpec(amat.shape, lambda b, k, i: (0, 0))],
        out_specs=pl.BlockSpec((1, Q_TILE, 256), lambda b, k, i: (b, i, k)),
        out_shape=jax.ShapeDtypeStruct((B, L, ATT_HEADS * ATT_HEAD_DIM), F32),
        scratch_shapes=[pltpu.VMEM((2 * Q_TILE, 2 * LANES), F32),
                        pltpu.VMEM((2 * Q_TILE, 2 * LANES), F32)],
        compiler_params=_cparams("parallel", "parallel", "arbitrary"),
        name="attn_prompt",
    )(rowp, rowp, cmp_kv, cmp_kv, kvt16, kvt16, kvt16, kvt16, tp, amat)


def _attn_sample_kernel(*refs, n_pages, past_len, nb):
    pages = refs[1:1 + nb * n_pages]
    for bb in range(nb):
        _attn_sample_one(pages[bb * n_pages:(bb + 1) * n_pages], *refs[1 + nb * n_pages:], bb=bb, n_pages=n_pages,
                         past_len=past_len)


def _attn_sample_one(pages, q_ref, gate_ref, new_s_ref, new_w_ref, win_ref, cmp_ref, ts_ref, amat_ref, o_ref, *,
                     bb, n_pages, past_len):
    T = SUBLANES
    R = ATT_HG * T
    n_cmp_rows = cmp_ref.shape[3]
    lane_t = lax.broadcasted_iota(jnp.int32, (T, LANES), 1)
    low_t = lane_t < ATT_HEAD_DIM
    t_r = lax.broadcasted_iota(jnp.int32, (R, 1), 0) & (T - 1)
    c1 = lax.broadcasted_iota(jnp.int32, (1, LANES), 1)
    g = _sigmoid(gate_ref[bb])
    zeros_new = jnp.zeros((LANES - T, LANES), F32)
    new_ok = (c1 <= t_r) & (c1 < T)
    n_c = lax.broadcasted_iota(jnp.int32, (1, 256), 1)
    d_c = past_len + t_r - (CMP_STRIDE * n_c + CMP_BLOCK - 1)
    c_w = lax.broadcasted_iota(jnp.int32, (1, WINDOW), 1)
    cmp_pad = jnp.zeros((256 - n_cmp_rows, LANES), F32)

    def lane_tile(kv):
        return LANES * (kv // 2)

    def queries(kv):
        par = kv % 2
        keep = (lane_t >= ATT_HEAD_DIM) if par else low_t
        qrows = []
        for hg in range(ATT_HG):
            h = kv * ATT_HG + hg
            q2 = q_ref[bb, :, LANES * (h // 2):LANES * (h // 2 + 1)]
            if h % 2 != par:
                q2 = pltpu.roll(q2, ATT_HEAD_DIM, axis=1)
            qrows.append(jnp.where(keep, q2, 0.0))
        return jnp.concatenate(qrows, axis=0).astype(BF16)

    def new_rows(ref, l0):
        return jnp.concatenate([ref[bb, :, l0:l0 + LANES], zeros_new], axis=0).astype(BF16)

    qs_all = [queries(kv) for kv in range(ATT_KV)]
    tabs = [ts_ref[kv * ATT_HG:(kv + 1) * ATT_HG].reshape(R, 1024) for kv in range(ATT_KV)]

    s_cs = []
    for kv in range(ATT_KV):
        kc = jnp.concatenate([cmp_ref[bb, 0, kv], cmp_pad], axis=0).astype(BF16)
        s_cs.append(jnp.where(d_c >= 0, _dot_nt(qs_all[kv], kc) + tabs[kv][:, 0:256], NEG_INF))
    p_cs = [_softmax_rows(s) for s in s_cs]
    p_sums = [p[0:T] + p[T:2 * T] + p[2 * T:3 * T] + p[3 * T:4 * T] for p in p_cs]
    o_cs = []
    for kv in range(ATT_KV):
        vc = jnp.concatenate([cmp_ref[bb, 1, kv], cmp_pad], axis=0).astype(BF16)
        o_cs.append(_dot(p_cs[kv].astype(BF16), vc))

    p_all = jnp.concatenate(p_sums + [jnp.zeros((LANES - ATT_KV * T, 256), F32)], axis=0)
    imp_t = _dot_split3(p_all, amat_ref[...].astype(BF16)).T
    rb = lax.broadcasted_iota(jnp.int32, (LANES, 1), 0)
    qblk = jnp.right_shift(past_len + (c1 & (T - 1)), SEL_BLOCK.bit_length() - 1)
    lag = qblk - rb
    allowed = lag >= 0
    forced = (rb == 0) | (allowed & (lag < N_LOCAL))
    score_t = jnp.where(allowed, imp_t + jnp.where(forced, FORCE_BONUS, 0.0), -1.0)
    sel_all = _top_blocks(score_t, N_SEL, axis=0).T

    nb = 2 * n_pages
    n_past = n_pages * PAGE_SIZE
    zero = jnp.zeros((R, LANES), F32)

    def pair_of(kv):
        return slice(2 * (kv // 2), 2 * (kv // 2) + 2)

    s_alls, s_ws = [], []
    for kv in range(ATT_KV):
        pl0, qs, tab = lane_tile(kv), qs_all[kv], tabs[kv]
        sel = sel_all[kv * T:(kv + 1) * T]
        k_t = jnp.concatenate([pages[p][0, 0, 0, pair_of(kv)].reshape(LANES, PAGE_SIZE).astype(BF16)
                               for p in range(n_pages)], axis=1)
        m_parts = [jnp.where(c1 < SEL_BLOCK, sel[:, 2 * p:2 * p + 1], sel[:, 2 * p + 1:2 * p + 2])
                   for p in range(n_pages)]
        m_parts.append(jnp.where(new_ok[0:T], jnp.broadcast_to(sel[:, nb:nb + 1], (T, LANES)), 0.0))
        bias_s = jnp.concatenate([zero] * (n_pages - 1) + [tab[:, 256:384], tab[:, 384:512]], axis=1)
        s_raw = jnp.concatenate([_dot(qs, k_t), _dot_nt(qs, new_rows(new_s_ref, pl0))], axis=1)
        s_alls.append(jnp.where(_tile4(jnp.concatenate(m_parts, axis=1)) > 0.5, s_raw + bias_s, NEG_INF))
        kw_t = win_ref[0, bb, 0, pair_of(kv)].reshape(LANES, WINDOW).astype(BF16)
        bias_w = jnp.concatenate([tab[:, 512:1024] + jnp.where(c_w >= t_r, 0.0, NEG_INF),
                                  tab[:, 384:512] + jnp.where(new_ok, 0.0, NEG_INF)], axis=1)
        s_ws.append(jnp.concatenate([_dot(qs, kw_t), _dot_nt(qs, new_rows(new_w_ref, pl0))], axis=1) + bias_w)
    p_ss = [_softmax_rows(s).astype(BF16) for s in s_alls]
    p_ws = [_softmax_rows(s).astype(BF16) for s in s_ws]
    o_ss, o_ws = [], []
    for kv in range(ATT_KV):
        pl0 = lane_tile(kv)
        v_t = jnp.concatenate([pages[p][0, 0, 1, pair_of(kv)].reshape(LANES, PAGE_SIZE).astype(BF16)
                               for p in range(n_pages)], axis=1)
        o_ss.append(_dot_nt(p_ss[kv][:, :n_past], v_t) + _dot(p_ss[kv][:, n_past:], new_rows(new_s_ref, 256 + pl0)))
        vw_t = win_ref[0, bb, 1, pair_of(kv)].reshape(LANES, WINDOW).astype(BF16)
        o_ws.append(_dot_nt(p_ws[kv][:, :WINDOW], vw_t) + _dot(p_ws[kv][:, WINDOW:], new_rows(new_w_ref, 256 + pl0)))

    out_tiles = []
    for kv in range(ATT_KV):
        par = kv % 2
        o_c, o_s, o_w = o_cs[kv], o_ss[kv], o_ws[kv]
        gk = g[:, LANES * kv:LANES * (kv + 1)]
        for pair in range(2):
            halves = []
            for which in range(2):
                hg = 2 * pair + which
                rows = slice(hg * T, (hg + 1) * T)
                o = gk[:, hg:hg + 1] * o_c[rows] + gk[:, 4 + hg:5 + hg] * o_s[rows] + gk[:, 8 + hg:9 + hg] * o_w[rows]
                if which != par:
                    o = pltpu.roll(o, ATT_HEAD_DIM, axis=1)
                halves.append(o)
            out_tiles.append(jnp.where(low_t, halves[0], halves[1]))
    o_ref[bb] = jnp.concatenate(out_tiles, axis=1)


def attn_sample(page_table, cache_t, win_t, li, proj, cmp_kv, ts, amat, past_len):
    B, n_pages = page_table.shape
    T = proj.shape[1]
    nb = SAMPLE_SEQS_PER_STEP

    def page_spec(bb, p):
        return pl.BlockSpec((1, 1, 2, ATT_KV, ATT_HEAD_DIM, PAGE_SIZE),
                            lambda b, pt: (li, pt[nb * b + bb, p], 1, 0, 0, 0))

    grid_spec = pltpu.PrefetchScalarGridSpec(
        num_scalar_prefetch=1,
        grid=(B // nb,),
        in_specs=[page_spec(bb, p) for bb in range(nb) for p in range(n_pages)]
        + [pl.BlockSpec((nb, T, 1024), lambda b, pt: (b, 0, 0)),
           pl.BlockSpec((nb, T, 512), lambda b, pt: (b, 0, 5)),
           pl.BlockSpec((nb, T, 512), lambda b, pt: (b, 0, 3)),
           pl.BlockSpec((nb, T, 512), lambda b, pt: (b, 0, 4)),
           pl.BlockSpec((1, nb) + win_t.shape[2:], lambda b, pt: (li, b, 0, 0, 0, 0)),
           pl.BlockSpec((nb,) + cmp_kv.shape[1:], lambda b, pt: (b, 0, 0, 0, 0)),
           pl.BlockSpec(ts.shape, lambda b, pt: (0, 0, 0)),
           pl.BlockSpec(amat.shape, lambda b, pt: (0, 0))],
        out_specs=pl.BlockSpec((nb, T, 1024), lambda b, pt: (b, 0, 0)),
    )
    return pl.pallas_call(
        functools.partial(_attn_sample_kernel, n_pages=n_pages, past_len=past_len, nb=nb),
        grid_spec=grid_spec,
        out_shape=jax.ShapeDtypeStruct((B, T, ATT_HEADS * ATT_HEAD_DIM), F32),
        compiler_params=_cparams("parallel"),
        name="attn_sample",
    )(page_table, *([cache_t] * (nb * n_pages)), proj, proj, proj, proj, win_t, cmp_kv, ts, amat)


def _ssd_weights(i, li, norm_mix, ssd_w_in, ssd_conv_w, ssd_conv_b, ssd_dt_bias, ssd_a_log, ssd_d, ssd_norm, ssd_w_out):
    d_inner = ssd_w_out.shape[1]
    conv_dim = ssd_conv_w.shape[2]
    heads = ssd_dt_bias.shape[1]
    hpg = heads // SSD_GROUPS
    w = ssd_w_in[li]

    def per_group(v):
        v = v.reshape(v.shape[:-1] + (SSD_GROUPS, hpg))
        return jnp.pad(v, [(0, 0)] * (v.ndim - 1) + [(0, LANES - hpg)]).reshape(v.shape[:-2] + (SSD_GROUPS * LANES,))

    w_dt = per_group(w[:, d_inner + conv_dim:])
    w_in = jnp.concatenate([w[:, :d_inner], w_dt, jnp.zeros_like(w_dt), w[:, d_inner:d_inner + conv_dim]], axis=1)
    return dict(
        norm=norm_mix[i][None],
        w_in=w_in.astype(BF16),
        conv_w=ssd_conv_w[li], conv_b=ssd_conv_b[li][None],
        dt_bias=per_group(ssd_dt_bias[li])[None],
        a_log=per_group(ssd_a_log[li]).reshape(SSD_GROUPS, 1, LANES),
        d_exp=jnp.repeat(ssd_d[li], SSD_HEAD_DIM)[None],
        norm_g=ssd_norm[li][None],
        w_out=ssd_w_out[li].astype(BF16),
    )


def _nsa_in_weight(w, q_scale):
    q_dim = ATT_HEADS * ATT_HEAD_DIM
    kv_dim = 6 * ATT_KV * ATT_HEAD_DIM
    idx = np.zeros((ATT_KV, LANES), np.int32)
    ok = np.zeros((ATT_KV, LANES), bool)
    for kv in range(ATT_KV):
        for br in range(3):
            for hg in range(ATT_HG):
                idx[kv, br * ATT_HG + hg] = q_dim + kv_dim + (kv * ATT_HG + hg) * 3 + br
                ok[kv, br * ATT_HG + hg] = True
    w_g = jnp.where(jnp.asarray(ok.reshape(-1))[None, :], w[:, idx.reshape(-1)], 0.0)
    return jnp.concatenate([w[:, :q_dim] * q_scale, w[:, q_dim:q_dim + kv_dim], w_g], axis=1).astype(BF16)


def _cmp_pair_weight(w1):
    w1r = w1.reshape(2, 2, CMP_STRIDE, ATT_HEAD_DIM, CMP_HIDDEN)
    eye = jnp.eye(2, dtype=w1.dtype)
    wp = jnp.einsum("fjsde,wv->fswdvje", w1r, eye)
    return wp.reshape(2, CMP_STRIDE * 2 * ATT_HEAD_DIM, 2 * 2 * CMP_HIDDEN).astype(BF16)


def kernel(x_prompt, x_sample, cache_nsa_kv, state_nsa_win, state_ssm, state_conv, state_pool, page_table, rel_bias,
           norm_mix, norm_ffn, norm_out, ffn_w_up, ffn_w_down, ssd_w_in, ssd_conv_w, ssd_conv_b, ssd_dt_bias,
           ssd_a_log, ssd_d, ssd_norm, ssd_w_out, pool_w, pool_scale, nsa_w_in, nsa_cmp_pe, nsa_cmp_w1, nsa_cmp_w2,
           nsa_w_out):
    bp, lp, d_model = x_prompt.shape
    bs, ls, _ = x_sample.shape
    depth = norm_mix.shape[0]
    n_pages = page_table.shape[1]
    past_len = n_pages * PAGE_SIZE
    assert ls == SUBLANES and lp % (16 * Q_TILE) == 0 and past_len >= WINDOW and state_nsa_win.shape[2] == WINDOW
    xp = x_prompt.reshape(bp * lp, d_model)
    xs = x_sample.reshape(bs * ls, d_model)
    outs = {k: [] for k in ("kv_p", "kv_s", "win_p", "win_s", "ssm_p", "ssm_s", "conv_p", "conv_s", "pool_p", "pool_s")}
    for i in range(depth):
        kind, li = i % 3, i // 3
        if kind == 0:
            w = _ssd_weights(i, li, norm_mix, ssd_w_in, ssd_conv_w, ssd_conv_b, ssd_dt_bias, ssd_a_log, ssd_d,
                             ssd_norm, ssd_w_out)
            conv_dim = ssd_conv_w.shape[2]
            xp, c_p, s_p = ssd_layer(xp, bp, jnp.zeros((bp, SSD_CONV - 1, conv_dim), F32),
                                     jnp.zeros((1, bp) + state_ssm.shape[2:], F32), 0, SSD_CHUNK, w)
            xs, c_s, s_s = ssd_layer(xs, bs, state_conv[li], state_ssm, li, ls, w)
            outs["conv_p"].append(c_p)
            outs["conv_s"].append(c_s)
            outs["ssm_p"].append(s_p)
            outs["ssm_s"].append(s_s)
        elif kind == 1:
            g = norm_mix[i][None]
            pw = pool_w[li].astype(BF16)
            sc = pool_scale[li][None]
            xp3, tail_p = pool_mixer_residual(xp.reshape(bp, lp, d_model), jnp.zeros((bp, POOL_HALO, d_model), F32),
                                              g, pw, sc, 0, POOL_ROWS)
            halo_s = jnp.pad(state_pool[li], ((0, 0), (1, 0), (0, 0)))
            xs3, tail_s = pool_mixer_residual(xs.reshape(bs, ls, d_model), halo_s, g, pw, sc, past_len, ls)
            xp, xs = xp3.reshape(bp * lp, d_model), xs3.reshape(bs * ls, d_model)
            outs["pool_p"].append(tail_p[:, 1:])
            outs["pool_s"].append(tail_s[:, 1:])
        else:
            g = norm_mix[i][None]
            w_in = _nsa_in_weight(nsa_w_in[li], ATT_HEAD_DIM ** -0.5)
            w_in2 = _nsa_in_weight(nsa_w_in[li], ATT_HEAD_DIM ** -0.5 * LOG2E)
            w_pair = _cmp_pair_weight(nsa_cmp_w1[li])
            pe8 = jnp.broadcast_to(nsa_cmp_pe[li].reshape(2, 1, -1), (2, SUBLANES, CMP_BLOCK * ATT_HEAD_DIM))
            w1 = nsa_cmp_w1[li].astype(BF16)
            w2d = jnp.concatenate([nsa_cmp_w2[li], nsa_cmp_w2[li]], axis=-1).astype(BF16)
            tp, ts = bias_tables(rel_bias, past_len)
            w_out = nsa_w_out[li].astype(BF16)
            kvw = ATT_KV * ATT_HEAD_DIM
            cache_t = jnp.transpose(cache_nsa_kv, (0, 1, 3, 4, 5, 2))
            win_t = jnp.transpose(state_nsa_win, (0, 1, 3, 4, 5, 2))
            w_row = jnp.concatenate([w_in2[:, :1024 + 2 * kvw], w_in2[:, 1024 + 6 * kvw:]], axis=1)
            w_kvt = w_in[:, 1024:1024 + 6 * kvw].T
            rowp, new_cache_t, new_win_p, kvt16 = nsa_in_proj_prompt(xp.reshape(bp, lp, d_model), g, w_row, w_kvt,
                                                                     WIDE_ROWS)
            pre_p = cmp_pre_prompt(rowp, w_pair, CMP_ROWS)
            cmp_p = jnp.pad(cmp_finish(pre_p, pe8, w1, w2d), ((0, 0), (0, 0), (0, 0), (CMP_FRONT, CMP_BACK), (0, 0)))
            amat_p = _importance_matrix(lp // CMP_STRIDE + CMP_FRONT + CMP_BACK, CMP_FRONT)
            o_p = attn_prompt(rowp, cmp_p, kvt16, tp, amat_p)
            xp = matmul_residual(o_p.reshape(bp * lp, -1), w_out, xp)
            outs["kv_p"].append(jnp.transpose(new_cache_t.reshape(bp, 4, ATT_KV, ATT_HEAD_DIM, lp), (0, 4, 1, 2, 3)))
            outs["win_p"].append(jnp.transpose(new_win_p.reshape(bp, 2, ATT_KV, ATT_HEAD_DIM, lp)[..., lp - WINDOW:],
                                               (0, 4, 1, 2, 3)))
            proj_s = norm_matmul(xs, g, w_in).reshape(bs, ls, -1)
            new_page = jnp.pad(proj_s[:, :, 1024:1024 + 2 * kvw], ((0, 0), (0, PAGE_SIZE - ls), (0, 0)))
            pre_s = cmp_pre_sample(page_table, cache_t, li, new_page, w_pair)
            cmp_s = cmp_finish(pre_s, pe8, w1, w2d)
            amat_s = _importance_matrix(256, 0)
            o_s = attn_sample(page_table, cache_t, win_t, li, proj_s, cmp_s, ts, amat_s, past_len)
            xs = matmul_residual(o_s.reshape(bs * ls, -1), w_out, xs)
            kv6s = proj_s[:, :, 1024:2560].reshape(bs, ls, 6, ATT_KV, ATT_HEAD_DIM)
            outs["kv_s"].append(kv6s[:, :, 0:4])
            new_win_t = jnp.concatenate([win_t[li][..., ls:], jnp.transpose(kv6s[:, :, 4:6], (0, 2, 3, 4, 1))], axis=-1)
            outs["win_s"].append(jnp.transpose(new_win_t, (0, 4, 1, 2, 3)))
        last = i == depth - 1
        xp = sqrelu_mlp_residual(xp, norm_ffn[i][None], ffn_w_up[i].astype(BF16), ffn_w_down[i].astype(BF16),
                                 norm_out[None], last)
        xs = sqrelu_mlp_residual(xs, norm_ffn[i][None], ffn_w_up[i].astype(BF16), ffn_w_down[i].astype(BF16),
                                 norm_out[None], last)
    st = lambda k: jnp.stack(outs[k])
    return (xp.reshape(bp, lp, d_model), xs.reshape(bs, ls, d_model), st("kv_p"), st("kv_s"), st("win_p"),
            st("win_s"), st("ssm_p"), st("ssm_s"), st("conv_p"), st("conv_s"), st("pool_p"), st("pool_s"))
```
